```python
import math
import jax, jax.numpy as jnp
from jax import lax
import numpy as np

D_MODEL = 2048
BATCH = 2
SEQ = 16384
DEPTH = 4
DEC_BATCH = 16
DEC_SEQ = 64
PAST_LEN = 2048

CHUNK = 64
N_HEADS = 8
N_KV_HEADS = 2
HEAD_DIM = 128
ATTN_WIDTH = N_HEADS * HEAD_DIM
H_IDX = 16
D_IDX = 64
IDX_SCALE = (H_IDX * D_IDX) ** -0.5
TOPK_MAX = 256
Q_BLOCK = 128
GMLP_CHUNK = 128
GMLP_GROUPS = 8
GMLP_WIDTH = D_MODEL - ATTN_WIDTH
GMLP_GROUP_DIM = GMLP_WIDTH // GMLP_GROUPS
MIX_WIDTH = ATTN_WIDTH + GMLP_WIDTH
D_FF = 5632
D_PLE = 256
NUM_BUCKETS = 32
MAX_DISTANCE = 128
N_NORMS = 8
EPS = 1e-6
IN_SIZES = (ATTN_WIDTH, N_KV_HEADS * HEAD_DIM, N_KV_HEADS * HEAD_DIM, H_IDX * D_IDX, D_IDX, H_IDX, GMLP_WIDTH, GMLP_WIDTH)
N_IN = sum(IN_SIZES)
IN_SPLITS = tuple(sum(IN_SIZES[:i + 1]) for i in range(len(IN_SIZES) - 1))

kernel_name = 'hybrid_dsa_gmlp_streaming_step'


def rmsnorm(x, g):
    xf = x.astype(jnp.float32)
    y = xf * lax.rsqrt(jnp.mean(xf * xf, axis=-1, keepdims=True) + EPS)
    return (y * g.astype(jnp.float32)).astype(x.dtype)


def swiglu(x, w_gu, w_down):
    gate, up = jnp.split(x @ w_gu, 2, axis=-1)
    return (jax.nn.silu(gate) * up) @ w_down


def t5_bucket(rel):
    nb = NUM_BUCKETS // 2
    max_exact = nb // 2
    n = jnp.abs(rel)
    nf = jnp.maximum(n, max_exact).astype(jnp.float32)
    large = max_exact + (jnp.log(nf / max_exact) / math.log(MAX_DISTANCE / max_exact) * (nb - max_exact)).astype(jnp.int32)
    large = jnp.minimum(large, nb - 1)
    return jnp.where(rel > 0, nb, 0) + jnp.where(n < max_exact, n, large)


def dsa_block(q, q_idx, w_idx, q_pos, k, v, k_idx, k_pos, rel_bias, topk):
    B, Tq = q.shape[:2]
    admissible = (k_pos[None, :] // CHUNK) <= (q_pos[:, None] // CHUNK)
    dots = jax.nn.relu(jnp.einsum('bqhd,bsd->bqhs', q_idx, k_idx))
    score = jnp.einsum('bqh,bqhs->bqs', w_idx, dots).astype(jnp.float32)
    score = jnp.where(admissible[None], score, -jnp.inf)
    _, sel = lax.top_k(score, topk)
    gather = jax.vmap(lambda a, i: a[i])
    k_sel = gather(k, sel)
    v_sel = gather(v, sel)
    pos_sel = k_pos[sel]
    rel = pos_sel - q_pos[None, :, None]
    valid = (pos_sel // CHUNK) <= (q_pos[None, :, None] // CHUNK)
    bias = rel_bias[t5_bucket(rel)]
    bias = jnp.transpose(bias, (0, 1, 3, 2)).reshape(B, Tq, N_KV_HEADS, N_HEADS // N_KV_HEADS, topk)
    qg = q.reshape(B, Tq, N_KV_HEADS, N_HEADS // N_KV_HEADS, HEAD_DIM)
    logits = jnp.einsum('bqgrd,bqkgd->bqgrk', qg, k_sel).astype(jnp.float32) * (HEAD_DIM ** -0.5) + bias.astype(jnp.float32)
    logits = jnp.where(valid[:, :, None, None, :], logits, -jnp.inf)
    probs = jax.nn.softmax(logits, axis=-1).astype(v.dtype)
    out = jnp.einsum('bqgrk,bqkgd->bqgrd', probs, v_sel)
    return out.reshape(B, Tq, ATTN_WIDTH)


def dsa_prompt(q, q_idx, w_idx, k, v, k_idx, rel_bias):
    B, S = q.shape[:2]
    nblk = S // Q_BLOCK
    topk = min(TOPK_MAX, S // 4)
    pos = jnp.arange(S, dtype=jnp.int32)

    def to_blocks(a):
        return jnp.swapaxes(a.reshape(B, nblk, Q_BLOCK, *a.shape[2:]), 0, 1)

    def body(args):
        qb, qib, wb, pb = args
        return dsa_block(qb, qib, wb, pb, k, v, k_idx, pos, rel_bias, topk)

    out = lax.map(body, (to_blocks(q), to_blocks(q_idx), to_blocks(w_idx), pos.reshape(nblk, Q_BLOCK)))
    return jnp.swapaxes(out, 0, 1).reshape(B, S, ATTN_WIDTH)


def gmlp_gate(u, v, ws, b):
    B, T = u.shape[:2]
    c = min(T, GMLP_CHUNK)
    n = T // c
    causal = jnp.tril(jnp.ones((c, c), dtype=bool))
    w = jnp.where(causal[None], ws[:, :c, :c], 0)
    vb = v.reshape(B, n, c, GMLP_GROUPS, GMLP_GROUP_DIM)
    s = jnp.einsum('gij,bnjgd->bnigd', w, vb) + b[:, :c].T[None, None, :, :, None]
    return u * s.reshape(B, T, GMLP_WIDTH)


def token_mix(xn, past, w_in, w_out, gmlp_ws, gmlp_b, gmlp_vnorm, rel_bias):
    B, T = xn.shape[:2]
    z = xn @ w_in
    q, kk, vv, qi, ki, wi, gu, gv = jnp.split(z, IN_SPLITS, axis=-1)
    q = q.reshape(B, T, N_HEADS, HEAD_DIM)
    kk = kk.reshape(B, T, N_KV_HEADS, HEAD_DIM)
    vv = vv.reshape(B, T, N_KV_HEADS, HEAD_DIM)
    qi = qi.reshape(B, T, H_IDX, D_IDX)
    wi = wi * IDX_SCALE
    if past is None:
        attn = dsa_prompt(q, qi, wi, kk, vv, ki, rel_bias)
    else:
        ck, cv, cki = past
        k_all = jnp.concatenate([ck, kk], axis=1)
        v_all = jnp.concatenate([cv, vv], axis=1)
        ki_all = jnp.concatenate([cki, ki], axis=1)
        L = PAST_LEN + T
        q_pos = PAST_LEN + jnp.arange(T, dtype=jnp.int32)
        k_pos = jnp.arange(L, dtype=jnp.int32)
        attn = dsa_block(q, qi, wi, q_pos, k_all, v_all, ki_all, k_pos, rel_bias, min(TOPK_MAX, L // 4))
    v_n = rmsnorm(jax.nn.gelu(gv), gmlp_vnorm)
    g = gmlp_gate(jax.nn.gelu(gu), v_n, gmlp_ws, gmlp_b)
    y = jnp.concatenate([attn, g], axis=-1) @ w_out
    return y, (kk, vv, ki, v_n)


def layer(h, p, past, ng, w1_gu, w1_dn, w_in, w_out, gmlp_ws, gmlp_b, gmlp_vnorm, w2_gu, w2_dn, w_pg, w_pe, rel_bias):
    h = h + 0.5 * rmsnorm(swiglu(rmsnorm(h, ng[0]), w1_gu, w1_dn), ng[1])
    y, new_rows = token_mix(rmsnorm(h, ng[2]), past, w_in, w_out, gmlp_ws, gmlp_b, gmlp_vnorm, rel_bias)
    h = h + rmsnorm(y, ng[3])
    h = h + 0.5 * rmsnorm(swiglu(rmsnorm(h, ng[4]), w2_gu, w2_dn), ng[5])
    gate = jax.nn.sigmoid(rmsnorm(h, ng[6]) @ w_pg)
    h = h + rmsnorm(gate * (p @ w_pe), ng[7])
    return h, new_rows


def setup_inputs(seed: int = 0) -> dict:
    key = jax.random.key(seed)
    ks = jax.random.split(key, 20)

    def nrm(k, shape, scale):
        return jax.random.normal(k, shape, jnp.float32) * scale

    return {
        'x_prompt': nrm(ks[0], (BATCH, SEQ, D_MODEL), 1.0),
        'x_sample': nrm(ks[1], (DEC_BATCH, DEC_SEQ, D_MODEL), 1.0),
        'p_prompt': nrm(ks[2], (DEPTH, BATCH, SEQ, D_PLE), 1.0),
        'p_sample': nrm(ks[3], (DEPTH, DEC_BATCH, DEC_SEQ, D_PLE), 1.0),
        'cache_k': nrm(ks[4], (DEPTH, DEC_BATCH, PAST_LEN, N_KV_HEADS, HEAD_DIM), 1.0),
        'cache_v': nrm(ks[5], (DEPTH, DEC_BATCH, PAST_LEN, N_KV_HEADS, HEAD_DIM), 1.0),
        'cache_kidx': nrm(ks[6], (DEPTH, DEC_BATCH, PAST_LEN, D_IDX), 1.0),
        'norm_g': 1.0 + nrm(ks[7], (DEPTH, N_NORMS, D_MODEL), 0.02),
        'w_ffn1_gu': nrm(ks[8], (DEPTH, D_MODEL, 2 * D_FF), D_MODEL ** -0.5),
        'w_ffn1_down': nrm(ks[9], (DEPTH, D_FF, D_MODEL), D_FF ** -0.5),
        'w_in': nrm(ks[10], (DEPTH, D_MODEL, N_IN), D_MODEL ** -0.5),
        'w_out': nrm(ks[11], (DEPTH, MIX_WIDTH, D_MODEL), MIX_WIDTH ** -0.5),
        'gmlp_ws': nrm(ks[12], (DEPTH, GMLP_GROUPS, GMLP_CHUNK, GMLP_CHUNK), GMLP_CHUNK ** -0.5),
        'gmlp_b': 1.0 + nrm(ks[13], (DEPTH, GMLP_GROUPS, GMLP_CHUNK), 0.02),
        'gmlp_vnorm': 1.0 + nrm(ks[14], (DEPTH, GMLP_WIDTH), 0.02),
        'w_ffn2_gu': nrm(ks[15], (DEPTH, D_MODEL, 2 * D_FF), D_MODEL ** -0.5),
        'w_ffn2_down': nrm(ks[16], (DEPTH, D_FF, D_MODEL), D_FF ** -0.5),
        'w_ple_gate': nrm(ks[17], (DEPTH, D_MODEL, D_MODEL), D_MODEL ** -0.5),
        'w_ple': nrm(ks[18], (DEPTH, D_PLE, D_MODEL), D_PLE ** -0.5),
        'rel_bias': nrm(ks[19], (NUM_BUCKETS, N_HEADS), 0.5),
    }


def reference(x_prompt, x_sample, p_prompt, p_sample, cache_k, cache_v, cache_kidx, norm_g, w_ffn1_gu, w_ffn1_down,
              w_in, w_out, gmlp_ws, gmlp_b, gmlp_vnorm, w_ffn2_gu, w_ffn2_down, w_ple_gate, w_ple, rel_bias):
    hp, hs = x_prompt, x_sample
    kp, vp, kip = [], [], []
    ksm, vsm, kism, gvsm = [], [], [], []
    for l in range(DEPTH):
        lw = (norm_g[l], w_ffn1_gu[l], w_ffn1_down[l], w_in[l], w_out[l], gmlp_ws[l], gmlp_b[l], gmlp_vnorm[l],
              w_ffn2_gu[l], w_ffn2_down[l], w_ple_gate[l], w_ple[l], rel_bias)
        hp, (k_, v_, ki_, _) = layer(hp, p_prompt[l], None, *lw)
        kp.append(k_)
        vp.append(v_)
        kip.append(ki_)
        hs, (k_, v_, ki_, gv_) = layer(hs, p_sample[l], (cache_k[l], cache_v[l], cache_kidx[l]), *lw)
        ksm.append(k_)
        vsm.append(v_)
        kism.append(ki_)
        gvsm.append(gv_)
    return (hp, hs, jnp.stack(kp), jnp.stack(vp), jnp.stack(kip), jnp.stack(ksm), jnp.stack(vsm), jnp.stack(kism), jnp.stack(gvsm))
```

```python
import functools
import math

import jax
import jax.numpy as jnp
from jax import lax
from jax.experimental import pallas as pl
from jax.experimental.pallas import tpu as pltpu

CHUNK = 64
N_HEADS = 8
N_KV_HEADS = 2
HEAD_DIM = 128
H_IDX = 16
D_IDX = 64
TOPK_MAX = 256
GMLP_CHUNK = 128
GMLP_GROUPS = 8
NUM_BUCKETS = 32
MAX_DISTANCE = 128
EPS = 1e-6

LANES = 128
VMEM_LIMIT_BYTES = 56 * 1024 * 1024

ROW_TILE = 512
FF_TILE = 512
KEY_TILE = 256
Q_TILE = 128

_MXU_DTYPE = jnp.bfloat16
_INT_MIN = -(2 ** 31)


def _params(*semantics):
    return pltpu.CompilerParams(dimension_semantics=semantics, vmem_limit_bytes=VMEM_LIMIT_BYTES)


def _rms(x, g):
    return x * lax.rsqrt(jnp.mean(x * x, axis=-1, keepdims=True) + EPS) * g


def _ffn_kernel(h_ref, gpre_ref, gpost_ref, wg_ref, wu_ref, wd_ref, o_ref, xn_ref, acc_ref):
    j = pl.program_id(1)

    @pl.when(j == 0)
    def _():
        xn_ref[...] = _rms(h_ref[...], gpre_ref[...]).astype(xn_ref.dtype)
        acc_ref[...] = jnp.zeros_like(acc_ref)

    xn = xn_ref[...]
    gate = jnp.dot(xn, wg_ref[...], preferred_element_type=jnp.float32)
    up = jnp.dot(xn, wu_ref[...], preferred_element_type=jnp.float32)
    act = (jax.nn.silu(gate) * up).astype(wd_ref.dtype)
    acc_ref[...] += jnp.dot(act, wd_ref[...], preferred_element_type=jnp.float32)

    @pl.when(j == pl.num_programs(1) - 1)
    def _():
        o_ref[...] = h_ref[...] + 0.5 * _rms(acc_ref[...], gpost_ref[...])


def _ffn(h, g_pre, g_post, w_gu, w_down, layer):
    rows, d = h.shape
    d_ff = w_down.shape[1]
    nf = d_ff // FF_TILE
    assert rows % ROW_TILE == 0 and d_ff % FF_TILE == 0
    return pl.pallas_call(
        _ffn_kernel,
        out_shape=jax.ShapeDtypeStruct((rows, d), jnp.float32),
        grid=(rows // ROW_TILE, nf),
        in_specs=[
            pl.BlockSpec((ROW_TILE, d), lambda i, j: (i, 0)),
            pl.BlockSpec((1, d), lambda i, j: (0, 0)),
            pl.BlockSpec((1, d), lambda i, j: (0, 0)),
            pl.BlockSpec((None, d, FF_TILE), lambda i, j: (layer, 0, j)),
            pl.BlockSpec((None, d, FF_TILE), lambda i, j: (layer, 0, j + nf)),
            pl.BlockSpec((None, FF_TILE, d), lambda i, j: (layer, j, 0)),
        ],
        out_specs=pl.BlockSpec((ROW_TILE, d), lambda i, j: (i, 0)),
        scratch_shapes=[pltpu.VMEM((ROW_TILE, d), _MXU_DTYPE), pltpu.VMEM((ROW_TILE, d), jnp.float32)],
        compiler_params=_params("parallel", "arbitrary"),
        name="ffn",
    )(h, g_pre, g_post, w_gu, w_gu, w_down)


def _proj_kernel(h_ref, g_ref, w_ref, o_ref, *, head_major):
    xn = _rms(h_ref[...], g_ref[...]).astype(w_ref.dtype)
    z = jnp.dot(xn, w_ref[...], preferred_element_type=jnp.float32)
    if head_major:
        for hh in range(o_ref.shape[0]):
            o_ref[hh] = z[:, hh * LANES:(hh + 1) * LANES].astype(o_ref.dtype)
    else:
        o_ref[...] = z.astype(o_ref.dtype)


def _proj(h, g, w, out_dtype, head_major, name):
    rows, d = h.shape
    n = w.shape[1]
    if head_major:
        out_shape = jax.ShapeDtypeStruct((n // LANES, rows, LANES), out_dtype)
        out_spec = pl.BlockSpec((n // LANES, ROW_TILE, LANES), lambda i: (0, i, 0))
    else:
        out_shape = jax.ShapeDtypeStruct((rows, n), out_dtype)
        out_spec = pl.BlockSpec((ROW_TILE, n), lambda i: (i, 0))
    return pl.pallas_call(
        functools.partial(_proj_kernel, head_major=head_major),
        out_shape=out_shape,
        grid=(rows // ROW_TILE,),
        in_specs=[
            pl.BlockSpec((ROW_TILE, d), lambda i: (i, 0)),
            pl.BlockSpec((1, d), lambda i: (0, 0)),
            pl.BlockSpec((d, n), lambda i: (0, 0)),
        ],
        out_specs=out_spec,
        compiler_params=_params("parallel"),
        name=name,
    )(h, g, w)


def _gmlp_kernel(uv_ref, gain_ref, ws_ref, b_ref, o_ref, vn_ref):
    width = o_ref.shape[1]
    gdim = width // GMLP_GROUPS
    for c in range(uv_ref.shape[0] // GMLP_CHUNK):
        rs = slice(c * GMLP_CHUNK, (c + 1) * GMLP_CHUNK)
        u = jax.nn.gelu(uv_ref[rs, :width])
        vn = _rms(jax.nn.gelu(uv_ref[rs, width:]), gain_ref[...])
        vn_ref[rs, :] = vn
        vnb = vn.astype(ws_ref.dtype)
        for g in range(GMLP_GROUPS):
            cs = slice(g * gdim, (g + 1) * gdim)
            s = jnp.dot(ws_ref[g], vnb[:, cs], preferred_element_type=jnp.float32) + b_ref[:, cs]
            o_ref[rs, cs] = (u[:, cs] * s).astype(o_ref.dtype)


def _gmlp(uv, gain, ws2, b2, n_prompt_rows):
    rows = uv.shape[0]
    width = uv.shape[1] // 2
    tile = 2 * GMLP_CHUNK
    n_prompt_tiles = n_prompt_rows // tile

    def sel(i):
        return jnp.where(i >= n_prompt_tiles, 1, 0)

    return pl.pallas_call(
        _gmlp_kernel,
        out_shape=(jax.ShapeDtypeStruct((rows, width), _MXU_DTYPE),
                   jax.ShapeDtypeStruct((rows, width), jnp.float32)),
        grid=(rows // tile,),
        in_specs=[
            pl.BlockSpec((tile, 2 * width), lambda i: (i, 0)),
            pl.BlockSpec((1, width), lambda i: (0, 0)),
            pl.BlockSpec((None, GMLP_GROUPS, GMLP_CHUNK, GMLP_CHUNK), lambda i: (sel(i), 0, 0, 0)),
            pl.BlockSpec((None, GMLP_CHUNK, width), lambda i: (sel(i), 0, 0)),
        ],
        out_specs=(pl.BlockSpec((tile, width), lambda i: (i, 0)),
                   pl.BlockSpec((tile, width), lambda i: (i, 0))),
        compiler_params=_params("parallel"),
        name="gmlp",
    )(uv, gain, ws2, b2)


def _mixout_kernel(h_ref, a_ref, m_ref, w_ref, g_ref, o_ref):
    y = jnp.dot(a_ref[...], w_ref[0], preferred_element_type=jnp.float32)
    y = y + jnp.dot(m_ref[...], w_ref[1], preferred_element_type=jnp.float32)
    o_ref[...] = h_ref[...] + _rms(y, g_ref[...])


def _mixout(h, attn, gm, w_out2, g, layer):
    rows, d = h.shape
    half = attn.shape[1]
    return pl.pallas_call(
        _mixout_kernel,
        out_shape=jax.ShapeDtypeStruct((rows, d), jnp.float32),
        grid=(rows // ROW_TILE,),
        in_specs=[
            pl.BlockSpec((ROW_TILE, d), lambda i: (i, 0)),
            pl.BlockSpec((ROW_TILE, half), lambda i: (i, 0)),
            pl.BlockSpec((ROW_TILE, half), lambda i: (i, 0)),
            pl.BlockSpec((None, 2, half, d), lambda i: (layer, 0, 0, 0)),
            pl.BlockSpec((1, d), lambda i: (0, 0)),
        ],
        out_specs=pl.BlockSpec((ROW_TILE, d), lambda i: (i, 0)),
        compiler_params=_params("parallel"),
        name="mixout",
    )(h, attn, gm, w_out2, g)


def _ple_kernel(h_ref, p_ref, gpre_ref, gpost_ref, wpg_ref, wpe_ref, o_ref):
    h = h_ref[...]
    xn = _rms(h, gpre_ref[...]).astype(wpg_ref.dtype)
    gate = jax.nn.sigmoid(jnp.dot(xn, wpg_ref[...], preferred_element_type=jnp.float32))
    emb = jnp.dot(p_ref[...].astype(wpe_ref.dtype), wpe_ref[...], preferred_element_type=jnp.float32)
    o_ref[...] = h + _rms(gate * emb, gpost_ref[...])


def _ple(h, p, g_pre, g_post, w_pg, w_pe, layer):
    rows, d = h.shape
    d_ple = p.shape[1]
    return pl.pallas_call(
        _ple_kernel,
        out_shape=jax.ShapeDtypeStruct((rows, d), jnp.float32),
        grid=(rows // ROW_TILE,),
        in_specs=[
            pl.BlockSpec((ROW_TILE, d), lambda i: (i, 0)),
            pl.BlockSpec((ROW_TILE, d_ple), lambda i: (i, 0)),
            pl.BlockSpec((1, d), lambda i: (0, 0)),
            pl.BlockSpec((1, d), lambda i: (0, 0)),
            pl.BlockSpec((None, d, d), lambda i: (layer, 0, 0)),
            pl.BlockSpec((None, d_ple, d), lambda i: (layer, 0, 0)),
        ],
        out_specs=pl.BlockSpec((ROW_TILE, d), lambda i: (i, 0)),
        compiler_params=_params("parallel"),
        name="ple",
    )(h, p, g_pre, g_post, w_pg, w_pe)


def _attn_kernel(cfar_ref, q_ref, qi_ref, wi_ref, k_ref, v_ref, ki_ref, bias_ref, o_ref,
                 keys_ref, wb_ref, m_ref, l_ref, acc_ref,
                 *, tq, tk, t0_base, t0_step, topk, idx_scale, scale):
    f32 = jnp.float32
    t0 = t0_base + pl.program_id(1) * t0_step
    j_last = (t0 + tq - 1) // tk
    j_near = jnp.maximum(t0 - MAX_DISTANCE, 0) // tk
    rep = tk // LANES
    heads_per_kv = N_HEADS // N_KV_HEADS

    def widen(x):
        return jnp.concatenate([x] * rep, axis=1)

    for h in range(H_IDX):
        col = D_IDX + h
        wb_ref[h] = jnp.broadcast_to(wi_ref[:, col:col + 1] * idx_scale, (tq, tk))

    def score_tile(j, near):
        kt = ki_ref[pl.ds(pl.multiple_of(j * tk, tk), tk), :]
        qs = qi_ref[...].reshape(H_IDX * tq, LANES)
        d = lax.dot_general(qs, kt, (((1,), (1,)), ((), ())), preferred_element_type=f32)
        score = jnp.zeros((tq, tk), f32)
        for h in range(H_IDX):
            score = score + wb_ref[h] * jnp.maximum(d[h * tq:(h + 1) * tq], 0.0)
        bits = lax.bitcast_convert_type(score, jnp.int32)
        key = bits ^ ((bits >> 31) & 0x7FFFFFFF)
        if near:
            s_pos = j * tk + lax.broadcasted_iota(jnp.int32, (tq, tk), 1)
            t_pos = t0 + lax.broadcasted_iota(jnp.int32, (tq, tk), 0)
            key = jnp.where((s_pos // CHUNK) <= (t_pos // CHUNK), key, _INT_MIN)
        keys_ref[j] = key

    def loop(lo, hi, fn):
        lax.fori_loop(lo, hi, lambda j, c: (fn(j), c)[1], 0)

    loop(0, j_near, functools.partial(score_tile, near=False))
    loop(j_near, j_last + 1, functools.partial(score_tile, near=True))

    def bit_step(step, thr_u):
        cand_u = thr_u | jnp.left_shift(jnp.int32(1), 31 - step)
        cand = widen(cand_u ^ _INT_MIN)

        def count_tile(j, cnt):
            return cnt + jnp.where(keys_ref[j] >= cand, 1.0, 0.0)

        cnt = lax.fori_loop(0, j_last + 1, count_tile, jnp.zeros((tq, tk), f32))
        total = jnp.sum(cnt, axis=1, keepdims=True)
        return jnp.where(total >= topk, cand_u, thr_u)

    thr_u = lax.fori_loop(0, 32, bit_step, jnp.zeros((tq, LANES), jnp.int32))
    thr = widen(jnp.maximum(thr_u ^ _INT_MIN, _INT_MIN + 1))

    m_ref[...] = jnp.full(m_ref.shape, -1e30, f32)
    l_ref[...] = jnp.zeros(l_ref.shape, f32)
    acc_ref[...] = jnp.zeros(acc_ref.shape, f32)

    def attn_tile(j, near):
        mask = keys_ref[j] >= thr
        rows = pl.ds(pl.multiple_of(j * tk, tk), tk)
        if near:
            b_idx = (j * tk - t0 + 2 * MAX_DISTANCE) // MAX_DISTANCE
        for g in range(N_KV_HEADS):
            cs = slice(g * HEAD_DIM, (g + 1) * HEAD_DIM)
            qg = q_ref[g * heads_per_kv:(g + 1) * heads_per_kv].reshape(heads_per_kv * tq, HEAD_DIM)
            s_all = lax.dot_general(qg, k_ref[rows, cs], (((1,), (1,)), ((), ())), preferred_element_type=f32)
            probs, alphas = [], []
            for r in range(heads_per_kv):
                h = g * heads_per_kv + r
                s = s_all[r * tq:(r + 1) * tq] * scale
                s = s + (bias_ref[b_idx, h] if near else cfar_ref[h])
                s = jnp.where(mask, s, -jnp.inf)
                m_old = m_ref[h]
                m_new = jnp.maximum(m_old, jnp.max(s, axis=1, keepdims=True))
                alpha = jnp.exp(m_old - m_new)
                p = jnp.exp(s - widen(m_new))
                l_ref[h] = alpha * l_ref[h] + jnp.sum(p, axis=1, keepdims=True)
                m_ref[h] = m_new
                probs.append(p.astype(v_ref.dtype))
                alphas.append(alpha)
            pv = jnp.dot(jnp.concatenate(probs, axis=0), v_ref[rows, cs], preferred_element_type=f32)
            for r in range(heads_per_kv):
                h = g * heads_per_kv + r
                acc_ref[h] = alphas[r] * acc_ref[h] + pv[r * tq:(r + 1) * tq]

    loop(0, j_near, functools.partial(attn_tile, near=False))
    loop(j_near, j_last + 1, functools.partial(attn_tile, near=True))

    for h in range(N_HEADS):
        o_ref[:, h * HEAD_DIM:(h + 1) * HEAD_DIM] = (acc_ref[h] / l_ref[h]).astype(o_ref.dtype)


def _attention(cfar, q_hm, qi_hm, kvki, k_all, v_all, ki_all, bias_tiles, *, row0, n_batch, n_qblocks, tq,
               t0_base, t0_step, topk):
    tk = KEY_TILE
    n_keys = k_all.shape[1]
    assert n_keys % tk == 0 and row0 % tq == 0 and tq <= MAX_DISTANCE
    assert t0_base % MAX_DISTANCE == 0 and t0_step % MAX_DISTANCE == 0 and tk == 2 * MAX_DISTANCE
    assert t0_base + (n_qblocks - 1) * t0_step + tq <= n_keys
    blk0 = row0 // tq
    wi_block = (N_KV_HEADS * HEAD_DIM * 2) // LANES
    kernel = functools.partial(
        _attn_kernel, tq=tq, tk=tk, t0_base=t0_base, t0_step=t0_step, topk=topk,
        idx_scale=(H_IDX * D_IDX) ** -0.5, scale=HEAD_DIM ** -0.5)
    once = pl.Buffered(1)
    return pl.pallas_call(
        kernel,
        out_shape=jax.ShapeDtypeStruct((n_batch * n_qblocks * tq, N_HEADS * HEAD_DIM), _MXU_DTYPE),
        grid=(n_batch, n_qblocks),
        in_specs=[
            pl.BlockSpec(memory_space=pltpu.SMEM),
            pl.BlockSpec((N_HEADS, tq, HEAD_DIM), lambda b, i: (0, blk0 + b * n_qblocks + i, 0)),
            pl.BlockSpec((H_IDX, tq, LANES), lambda b, i: (0, blk0 + b * n_qblocks + i, 0)),
            pl.BlockSpec((tq, LANES), lambda b, i: (blk0 + b * n_qblocks + i, wi_block)),
            pl.BlockSpec((None, n_keys, N_KV_HEADS * HEAD_DIM), lambda b, i: (b, 0, 0), pipeline_mode=once),
            pl.BlockSpec((None, n_keys, N_KV_HEADS * HEAD_DIM), lambda b, i: (b, 0, 0), pipeline_mode=once),
            pl.BlockSpec((None, n_keys, LANES), lambda b, i: (b, 0, 0), pipeline_mode=once),
            pl.BlockSpec((3, N_HEADS, tq, tk), lambda b, i: (0, 0, 0, 0), pipeline_mode=once),
        ],
        out_specs=pl.BlockSpec((tq, N_HEADS * HEAD_DIM), lambda b, i: (b * n_qblocks + i, 0)),
        scratch_shapes=[
            pltpu.VMEM((n_keys // tk, tq, tk), jnp.int32),
            pltpu.VMEM((H_IDX, tq, tk), jnp.float32),
            pltpu.VMEM((N_HEADS, tq, LANES), jnp.float32),
            pltpu.VMEM((N_HEADS, tq, LANES), jnp.float32),
            pltpu.VMEM((N_HEADS, tq, HEAD_DIM), jnp.float32),
        ],
        compiler_params=_params("parallel", "arbitrary"),
        name="dsa_attention",
    )(cfar, q_hm, qi_hm, kvki, k_all, v_all, ki_all, bias_tiles)


def _t5_bucket(rel):
    nb = NUM_BUCKETS // 2
    max_exact = nb // 2
    n = jnp.abs(rel)
    nf = jnp.maximum(n, max_exact).astype(jnp.float32)
    large = max_exact + (jnp.log(nf / max_exact) / math.log(MAX_DISTANCE / max_exact) * (nb - max_exact)).astype(jnp.int32)
    large = jnp.minimum(large, nb - 1)
    return jnp.where(rel > 0, nb, 0) + jnp.where(n < max_exact, n, large)


def _bias_tiles(rel_bias):
    d = jnp.array([-2 * MAX_DISTANCE, -MAX_DISTANCE, 0], jnp.int32)[:, None, None]
    rel = d + jnp.arange(KEY_TILE, dtype=jnp.int32)[None, None, :] - jnp.arange(Q_TILE, dtype=jnp.int32)[None, :, None]
    tiles = jnp.transpose(rel_bias[_t5_bucket(rel)], (0, 3, 1, 2))
    far = rel_bias[_t5_bucket(jnp.int32(-MAX_DISTANCE))]
    return tiles.astype(jnp.float32), far.astype(jnp.float32)


def _pad_cols(w, block, width):
    lead = w.shape[:-1]
    n = w.shape[-1] // block
    w = w.reshape(*lead, n, block)
    w = jnp.pad(w, [(0, 0)] * len(lead) + [(0, 0), (0, width - block)])
    return w.reshape(*lead, n * width)


def kernel(x_prompt, x_sample, p_prompt, p_sample, cache_k, cache_v, cache_kidx, norm_g, w_ffn1_gu, w_ffn1_down,
           w_in, w_out, gmlp_ws, gmlp_b, gmlp_vnorm, w_ffn2_gu, w_ffn2_down, w_ple_gate, w_ple, rel_bias):
    depth = w_in.shape[0]
    n_b, seq, d_model = x_prompt.shape
    dec_b, dec_seq, _ = x_sample.shape
    past = cache_k.shape[2]
    rows_p, rows_s = n_b * seq, dec_b * dec_seq
    attn_w = N_HEADS * HEAD_DIM
    kv_w = N_KV_HEADS * HEAD_DIM
    idx_w = H_IDX * D_IDX
    gmlp_w = d_model - attn_w
    mx = _MXU_DTYPE
    assert seq % KEY_TILE == 0 and dec_seq <= CHUNK and past % MAX_DISTANCE == 0 and dec_seq % 8 == 0
    assert rows_p % (2 * GMLP_CHUNK) == 0 and (2 * GMLP_CHUNK) % dec_seq == 0 and rows_s % (2 * GMLP_CHUNK) == 0

    o0 = 0
    o1 = o0 + attn_w
    o2 = o1 + kv_w
    o3 = o2 + kv_w
    o4 = o3 + idx_w
    o5 = o4 + D_IDX
    o6 = o5 + H_IDX
    o7 = o6 + gmlp_w
    w_q = w_in[:, :, o0:o1].astype(mx)
    w_qi = _pad_cols(w_in[:, :, o3:o4], D_IDX, LANES).astype(mx)
    w_kv = jnp.pad(w_in[:, :, o1:o3], ((0, 0), (0, 0), (0, 0)))
    w_kvki = jnp.concatenate([w_kv, w_in[:, :, o4:o6]], axis=-1)
    w_kvki = jnp.pad(w_kvki, ((0, 0), (0, 0), (0, 2 * kv_w + LANES - w_kvki.shape[-1]))).astype(mx)
    w_uv = w_in[:, :, o6:].astype(mx)
    w1_gu, w1_dn = w_ffn1_gu.astype(mx), w_ffn1_down.astype(mx)
    w2_gu, w2_dn = w_ffn2_gu.astype(mx), w_ffn2_down.astype(mx)
    w_out2 = w_out.astype(mx).reshape(depth, 2, attn_w, d_model)
    w_pg, w_pe = w_ple_gate.astype(mx), w_ple.astype(mx)

    causal = jnp.tril(jnp.ones((GMLP_CHUNK, GMLP_CHUNK), bool))
    ws_full = jnp.where(causal, gmlp_ws, 0.0)
    reps = GMLP_CHUNK // dec_seq
    small = jnp.where(causal[:dec_seq, :dec_seq], gmlp_ws[:, :, :dec_seq, :dec_seq], 0.0)
    eye = jnp.eye(reps, dtype=gmlp_ws.dtype)
    ws_small = jnp.einsum('ab,lgij->lgaibj', eye, small).reshape(depth, GMLP_GROUPS, GMLP_CHUNK, GMLP_CHUNK)
    ws2 = jnp.stack([ws_full, ws_small], axis=1).astype(mx)
    b_full = gmlp_b
    b_small = jnp.tile(gmlp_b[:, :, :dec_seq], (1, 1, reps))
    b2 = jnp.stack([b_full, b_small], axis=1)
    b2 = jnp.repeat(jnp.swapaxes(b2, 2, 3), gmlp_w // GMLP_GROUPS, axis=-1)

    bias_tiles, bias_far = _bias_tiles(rel_bias)
    bias_tiles_s = bias_tiles[:, :, :dec_seq]

    h = jnp.concatenate([x_prompt.reshape(rows_p, d_model), x_sample.reshape(rows_s, d_model)], axis=0)
    p_all = jnp.concatenate([p_prompt.reshape(depth, rows_p, -1), p_sample.reshape(depth, rows_s, -1)], axis=1)

    keys_s = past + dec_seq
    keys_s_pad = -(-keys_s // KEY_TILE) * KEY_TILE
    topk_p = min(TOPK_MAX, seq // 4)
    topk_s = min(TOPK_MAX, keys_s // 4)

    outs = [[] for _ in range(7)]
    for l in range(depth):
        g = norm_g[l][:, None, :]
        h = _ffn(h, g[0], g[1], w1_gu, w1_dn, l)

        q_hm = _proj(h, g[2], w_q[l], mx, True, "proj_q")
        qi_hm = _proj(h, g[2], w_qi[l], mx, True, "proj_qi")
        kvki = _proj(h, g[2], w_kvki[l], jnp.float32, False, "proj_kv")
        uv = _proj(h, g[2], w_uv[l], jnp.float32, False, "proj_uv")

        k_new, v_new, ki_new = kvki[:, :kv_w], kvki[:, kv_w:2 * kv_w], kvki[:, 2 * kv_w:2 * kv_w + D_IDX]
        ki_pad = kvki[:, 2 * kv_w:2 * kv_w + LANES]
        ki_pad = jnp.where(jnp.arange(LANES) < D_IDX, ki_pad, 0.0)

        attn_p = _attention(
            bias_far, q_hm, qi_hm, kvki,
            k_new[:rows_p].reshape(n_b, seq, kv_w).astype(mx),
            v_new[:rows_p].reshape(n_b, seq, kv_w).astype(mx),
            ki_pad[:rows_p].reshape(n_b, seq, LANES).astype(mx),
            bias_tiles, row0=0, n_batch=n_b, n_qblocks=seq // Q_TILE, tq=Q_TILE,
            t0_base=0, t0_step=Q_TILE, topk=topk_p)

        def with_cache(cache, new, width):
            full = jnp.concatenate([cache.reshape(dec_b, past, -1), new[rows_p:].reshape(dec_b, dec_seq, -1)], axis=1)
            full = jnp.pad(full, ((0, 0), (0, keys_s_pad - keys_s), (0, width - full.shape[-1])))
            return full.astype(mx)

        attn_s = _attention(
            bias_far, q_hm, qi_hm, kvki,
            with_cache(cache_k[l], k_new, kv_w), with_cache(cache_v[l], v_new, kv_w),
            with_cache(cache_kidx[l], ki_new, LANES),
            bias_tiles_s, row0=rows_p, n_batch=dec_b, n_qblocks=1, tq=dec_seq,
            t0_base=past, t0_step=0, topk=topk_s)
        attn = jnp.concatenate([attn_p, attn_s], axis=0)

        gm, vn = _gmlp(uv, gmlp_vnorm[l][None, :], ws2[l], b2[l], rows_p)
        h = _mixout(h, attn, gm, w_out2, g[3], l)
        h = _ffn(h, g[4], g[5], w2_gu, w2_dn, l)
        h = _ple(h, p_all[l], g[6], g[7], w_pg, w_pe, l)

        outs[0].append(k_new[:rows_p].reshape(n_b, seq, N_KV_HEADS, HEAD_DIM))
        outs[1].append(v_new[:rows_p].reshape(n_b, seq, N_KV_HEADS, HEAD_DIM))
        outs[2].append(ki_new[:rows_p].reshape(n_b, seq, D_IDX))
        outs[3].append(k_new[rows_p:].reshape(dec_b, dec_seq, N_KV_HEADS, HEAD_DIM))
        outs[4].append(v_new[rows_p:].reshape(dec_b, dec_seq, N_KV_HEADS, HEAD_DIM))
        outs[5].append(ki_new[rows_p:].reshape(dec_b, dec_seq, D_IDX))
        outs[6].append(vn[rows_p:].reshape(dec_b, dec_seq, gmlp_w))

    y_prompt = h[:rows_p].reshape(n_b, seq, d_model)
    y_sample = h[rows_p:].reshape(dec_b, dec_seq, d_model)
    return (y_prompt, y_sample) + tuple(jnp.stack(o) for o in outs)
```

```python
import functools
import math

import jax
import jax.numpy as jnp
from jax import lax
from jax.experimental import pallas as pl
from jax.experimental.pallas import tpu as pltpu

CHUNK = 64
N_HEADS = 8
N_KV_HEADS = 2
HEAD_DIM = 128
H_IDX = 16
D_IDX = 64
TOPK_MAX = 256
GMLP_CHUNK = 128
GMLP_GROUPS = 8
NUM_BUCKETS = 32
MAX_DISTANCE = 128
EPS = 1e-6
LOG2_E = 1.4426950408889634

LANES = 128
SUBLANES = 8
VMEM_LIMIT_BYTES = 56 * 1024 * 1024

ROW_TILE = 512
FF_TILE = 512
Q_TILE = 128
SCORE_TILE = 256
COUNT_TILE = 512
NEAR_KEYS = 2 * MAX_DISTANCE

_MXU_DTYPE = jnp.bfloat16
_INT_MIN = -(2 ** 31)


def _params(*semantics):
    return pltpu.CompilerParams(dimension_semantics=semantics, vmem_limit_bytes=VMEM_LIMIT_BYTES)


def _rms(x, g):
    return x * lax.rsqrt(jnp.mean(x * x, axis=-1, keepdims=True) + EPS) * g


def _ffn_kernel(h_ref, gpre_ref, gpost_ref, wg_ref, wu_ref, wd_ref, o_ref, xn_ref, acc_ref):
    j = pl.program_id(1)

    @pl.when(j == 0)
    def _():
        xn_ref[...] = _rms(h_ref[...], gpre_ref[...]).astype(xn_ref.dtype)
        acc_ref[...] = jnp.zeros_like(acc_ref)

    xn = xn_ref[...]
    gate = jnp.dot(xn, wg_ref[...], preferred_element_type=jnp.float32)
    up = jnp.dot(xn, wu_ref[...], preferred_element_type=jnp.float32)
    act = (jax.nn.silu(gate) * up).astype(wd_ref.dtype)
    acc_ref[...] += jnp.dot(act, wd_ref[...], preferred_element_type=jnp.float32)

    @pl.when(j == pl.num_programs(1) - 1)
    def _():
        o_ref[...] = h_ref[...] + 0.5 * _rms(acc_ref[...], gpost_ref[...])


def _ffn(h, g_pre, g_post, w_gu, w_down, layer):
    rows, d = h.shape
    d_ff = w_down.shape[1]
    nf = d_ff // FF_TILE
    assert rows % ROW_TILE == 0 and d_ff % FF_TILE == 0
    return pl.pallas_call(
        _ffn_kernel,
        out_shape=jax.ShapeDtypeStruct((rows, d), jnp.float32),
        grid=(rows // ROW_TILE, nf),
        in_specs=[
            pl.BlockSpec((ROW_TILE, d), lambda i, j: (i, 0)),
            pl.BlockSpec((1, d), lambda i, j: (0, 0)),
            pl.BlockSpec((1, d), lambda i, j: (0, 0)),
            pl.BlockSpec((None, d, FF_TILE), lambda i, j: (layer, 0, j)),
            pl.BlockSpec((None, d, FF_TILE), lambda i, j: (layer, 0, j + nf)),
            pl.BlockSpec((None, FF_TILE, d), lambda i, j: (layer, j, 0)),
        ],
        out_specs=pl.BlockSpec((ROW_TILE, d), lambda i, j: (i, 0)),
        scratch_shapes=[pltpu.VMEM((ROW_TILE, d), _MXU_DTYPE), pltpu.VMEM((ROW_TILE, d), jnp.float32)],
        compiler_params=_params("parallel", "arbitrary"),
        name="ffn",
    )(h, g_pre, g_post, w_gu, w_gu, w_down)


def _proj_kernel(h_ref, g_ref, w_ref, o_ref, *, head_major, out_scale):
    xn = _rms(h_ref[...], g_ref[...]).astype(w_ref.dtype)
    z = jnp.dot(xn, w_ref[...], preferred_element_type=jnp.float32)
    if out_scale != 1.0:
        z = z * out_scale
    if head_major:
        for hh in range(o_ref.shape[0]):
            o_ref[hh] = z[:, hh * LANES:(hh + 1) * LANES].astype(o_ref.dtype)
    else:
        o_ref[...] = z.astype(o_ref.dtype)


def _proj(h, g, w, out_dtype, head_major, name, out_scale=1.0):
    rows, d = h.shape
    n = w.shape[1]
    if head_major:
        out_shape = jax.ShapeDtypeStruct((n // LANES, rows, LANES), out_dtype)
        out_spec = pl.BlockSpec((n // LANES, ROW_TILE, LANES), lambda i: (0, i, 0))
    else:
        out_shape = jax.ShapeDtypeStruct((rows, n), out_dtype)
        out_spec = pl.BlockSpec((ROW_TILE, n), lambda i: (i, 0))
    return pl.pallas_call(
        functools.partial(_proj_kernel, head_major=head_major, out_scale=out_scale),
        out_shape=out_shape,
        grid=(rows // ROW_TILE,),
        in_specs=[
            pl.BlockSpec((ROW_TILE, d), lambda i: (i, 0)),
            pl.BlockSpec((1, d), lambda i: (0, 0)),
            pl.BlockSpec((d, n), lambda i: (0, 0)),
        ],
        out_specs=out_spec,
        compiler_params=_params("parallel"),
        name=name,
    )(h, g, w)


def _gmlp_kernel(uv_ref, gain_ref, ws_ref, b_ref, o_ref, vn_ref):
    width = o_ref.shape[1]
    gdim = width // GMLP_GROUPS
    for c in range(uv_ref.shape[0] // GMLP_CHUNK):
        rs = slice(c * GMLP_CHUNK, (c + 1) * GMLP_CHUNK)
        u = jax.nn.gelu(uv_ref[rs, :width])
        vn = _rms(jax.nn.gelu(uv_ref[rs, width:]), gain_ref[...])
        vn_ref[rs, :] = vn
        vnb = vn.astype(ws_ref.dtype)
        for g in range(GMLP_GROUPS):
            cs = slice(g * gdim, (g + 1) * gdim)
            s = jnp.dot(ws_ref[g], vnb[:, cs], preferred_element_type=jnp.float32) + b_ref[:, cs]
            o_ref[rs, cs] = (u[:, cs] * s).astype(o_ref.dtype)


def _gmlp(uv, gain, ws2, b2, n_prompt_rows):
    rows = uv.shape[0]
    width = uv.shape[1] // 2
    tile = 2 * GMLP_CHUNK
    n_prompt_tiles = n_prompt_rows // tile

    def sel(i):
        return jnp.where(i >= n_prompt_tiles, 1, 0)

    return pl.pallas_call(
        _gmlp_kernel,
        out_shape=(jax.ShapeDtypeStruct((rows, width), _MXU_DTYPE),
                   jax.ShapeDtypeStruct((rows, width), jnp.float32)),
        grid=(rows // tile,),
        in_specs=[
            pl.BlockSpec((tile, 2 * width), lambda i: (i, 0)),
            pl.BlockSpec((1, width), lambda i: (0, 0)),
            pl.BlockSpec((None, GMLP_GROUPS, GMLP_CHUNK, GMLP_CHUNK), lambda i: (sel(i), 0, 0, 0)),
            pl.BlockSpec((None, GMLP_CHUNK, width), lambda i: (sel(i), 0, 0)),
        ],
        out_specs=(pl.BlockSpec((tile, width), lambda i: (i, 0)),
                   pl.BlockSpec((tile, width), lambda i: (i, 0))),
        compiler_params=_params("parallel"),
        name="gmlp",
    )(uv, gain, ws2, b2)


def _mixout_kernel(h_ref, a_ref, m_ref, w_ref, g_ref, o_ref):
    y = jnp.dot(a_ref[...], w_ref[0], preferred_element_type=jnp.float32)
    y = y + jnp.dot(m_ref[...], w_ref[1], preferred_element_type=jnp.float32)
    o_ref[...] = h_ref[...] + _rms(y, g_ref[...])


def _mixout(h, attn, gm, w_out2, g, layer):
    rows, d = h.shape
    half = attn.shape[1]
    return pl.pallas_call(
        _mixout_kernel,
        out_shape=jax.ShapeDtypeStruct((rows, d), jnp.float32),
        grid=(rows // ROW_TILE,),
        in_specs=[
            pl.BlockSpec((ROW_TILE, d), lambda i: (i, 0)),
            pl.BlockSpec((ROW_TILE, half), lambda i: (i, 0)),
            pl.BlockSpec((ROW_TILE, half), lambda i: (i, 0)),
            pl.BlockSpec((None, 2, half, d), lambda i: (layer, 0, 0, 0)),
            pl.BlockSpec((1, d), lambda i: (0, 0)),
        ],
        out_specs=pl.BlockSpec((ROW_TILE, d), lambda i: (i, 0)),
        compiler_params=_params("parallel"),
        name="mixout",
    )(h, attn, gm, w_out2, g)


def _ple_kernel(h_ref, p_ref, gpre_ref, gpost_ref, wpg_ref, wpe_ref, o_ref):
    h = h_ref[...]
    xn = _rms(h, gpre_ref[...]).astype(wpg_ref.dtype)
    gate = jax.nn.sigmoid(jnp.dot(xn, wpg_ref[...], preferred_element_type=jnp.float32))
    emb = jnp.dot(p_ref[...].astype(wpe_ref.dtype), wpe_ref[...], preferred_element_type=jnp.float32)
    o_ref[...] = h + _rms(gate * emb, gpost_ref[...])


def _ple(h, p, g_pre, g_post, w_pg, w_pe, layer):
    rows, d = h.shape
    d_ple = p.shape[1]
    return pl.pallas_call(
        _ple_kernel,
        out_shape=jax.ShapeDtypeStruct((rows, d), jnp.float32),
        grid=(rows // ROW_TILE,),
        in_specs=[
            pl.BlockSpec((ROW_TILE, d), lambda i: (i, 0)),
            pl.BlockSpec((ROW_TILE, d_ple), lambda i: (i, 0)),
            pl.BlockSpec((1, d), lambda i: (0, 0)),
            pl.BlockSpec((1, d), lambda i: (0, 0)),
            pl.BlockSpec((None, d, d), lambda i: (layer, 0, 0)),
            pl.BlockSpec((None, d_ple, d), lambda i: (layer, 0, 0)),
        ],
        out_specs=pl.BlockSpec((ROW_TILE, d), lambda i: (i, 0)),
        compiler_params=_params("parallel"),
        name="ple",
    )(h, p, g_pre, g_post, w_pg, w_pe)


def _attn_kernel(q_ref, qi_ref, w_ref, k_ref, vt_ref, ki_ref, bias_ref, o_ref,
                 keys_ref, m_ref, l_ref, acc_ref, s_ref, p_ref, *, t0_base, t0_step, topk, idx_scale):
    f32 = jnp.float32
    tq = Q_TILE
    heads_per_kv = N_HEADS // N_KV_HEADS
    nt = (((1,), (1,)), ((), ()))
    t0 = t0_base + pl.program_id(1) * t0_step
    w0 = jnp.maximum(t0 - MAX_DISTANCE, 0)

    def loop(n, fn):
        lax.fori_loop(0, n, lambda j, c: (fn(j), c)[1], 0)

    def reduce_keys(x, op):
        x = op(x.reshape(-1, SUBLANES * SUBLANES, tq), axis=0)
        x = op(x.reshape(SUBLANES, SUBLANES, tq), axis=0)
        return op(x, axis=0, keepdims=True)

    def score_rows(r0, near):
        r0 = pl.multiple_of(r0, MAX_DISTANCE)
        qs = qi_ref[...].reshape(H_IDX * tq, LANES)
        d = lax.dot_general(ki_ref[pl.ds(r0, SCORE_TILE), :], qs, nt, preferred_element_type=f32)
        score = jnp.zeros((SCORE_TILE, tq), f32)
        for h in range(H_IDX):
            score = score + (w_ref[h:h + 1, :] * idx_scale) * jnp.maximum(d[:, h * tq:(h + 1) * tq], 0.0)
        bits = lax.bitcast_convert_type(score, jnp.int32)
        key = bits ^ ((bits >> 31) & 0x7FFFFFFF)
        if near:
            s_pos = r0 + lax.broadcasted_iota(jnp.int32, (SCORE_TILE, tq), 0)
            t_pos = t0 + lax.broadcasted_iota(jnp.int32, (SCORE_TILE, tq), 1)
            key = jnp.where((s_pos // CHUNK) <= (t_pos // CHUNK), key, _INT_MIN)
        keys_ref[pl.ds(r0, SCORE_TILE), :] = key

    def score_pair(j):
        score_rows(2 * j * SCORE_TILE, False)
        score_rows((2 * j + 1) * SCORE_TILE, False)

    loop((w0 + 2 * SCORE_TILE - 1) // (2 * SCORE_TILE), score_pair)
    score_rows(w0, True)
    pad_rows = COUNT_TILE - MAX_DISTANCE
    keys_ref[pl.ds(pl.multiple_of(w0 + NEAR_KEYS, MAX_DISTANCE), pad_rows), :] = jnp.full(
        (pad_rows, tq), _INT_MIN, jnp.int32)

    n_count = (w0 + NEAR_KEYS + COUNT_TILE - 1) // COUNT_TILE

    def count_ge(cand):
        wide = SUBLANES * SUBLANES
        cand = jnp.concatenate([cand] * SUBLANES, axis=0)

        def body(j, cnt):
            k = keys_ref[pl.ds(pl.multiple_of(j * COUNT_TILE, COUNT_TILE), COUNT_TILE), :]
            k = k.reshape(COUNT_TILE // wide, wide, tq)
            return cnt + jnp.sum(jnp.where(k >= cand[None], 1.0, 0.0), axis=0)

        cnt = lax.fori_loop(0, n_count, body, jnp.zeros((wide, tq), f32))
        return jnp.broadcast_to(reduce_keys(cnt, jnp.sum), (SUBLANES, tq))

    def search_cond(state):
        step, _, done = state
        return jnp.logical_and(step < 32, jnp.sum(done) < float(SUBLANES * tq))

    def search_step(state):
        step, thr_u, done = state
        cand_u = thr_u | jnp.left_shift(jnp.int32(1), 31 - step)
        cnt = count_ge(cand_u ^ _INT_MIN)
        take = jnp.logical_and(done == 0.0, cnt >= topk)
        thr_u = jnp.where(take, cand_u, thr_u)
        done = jnp.where(cnt == topk, 1.0, done)
        return step + 1, thr_u, done

    n_admissible = count_ge(jnp.full((SUBLANES, tq), _INT_MIN + 1, jnp.int32))
    done0 = jnp.where(n_admissible <= topk, 1.0, 0.0)
    _, thr_u, _ = lax.while_loop(search_cond, search_step,
                                 (jnp.int32(0), jnp.zeros((SUBLANES, tq), jnp.int32), done0))
    thr = jnp.maximum(thr_u ^ _INT_MIN, _INT_MIN + 1)[0:1]

    m_ref[...] = jnp.full(m_ref.shape, -1e30, f32)
    l_ref[...] = jnp.zeros(l_ref.shape, f32)
    acc_ref[...] = jnp.zeros(acc_ref.shape, f32)

    def attend(r0, nk, near):
        r0 = pl.multiple_of(r0, MAX_DISTANCE)
        rows = pl.ds(r0, nk)
        selected = keys_ref[rows, :] >= thr
        mask_bias = jnp.where(selected, 0.0, -jnp.inf)
        if not near:
            pos = r0 + lax.broadcasted_iota(jnp.int32, (nk, tq), 0)
            mask_bias = jnp.where(pos < w0, mask_bias, -jnp.inf)
        for g in range(N_KV_HEADS):
            cs = slice(g * HEAD_DIM, (g + 1) * HEAD_DIM)
            qg = q_ref[g * heads_per_kv:(g + 1) * heads_per_kv].reshape(heads_per_kv * tq, HEAD_DIM)
            s_all = lax.dot_general(k_ref[rows, cs], qg, nt, preferred_element_type=f32)
            for r in range(heads_per_kv):
                h = g * heads_per_kv + r
                s = s_all[:, r * tq:(r + 1) * tq] + mask_bias
                if near:
                    s = s + bias_ref[jnp.where(t0 == w0, 1, 0), h]
                s_ref[h, 0:nk, :] = s
        for g in range(N_KV_HEADS):
            cs = slice(g * HEAD_DIM, (g + 1) * HEAD_DIM)
            alphas = []
            for r in range(heads_per_kv):
                h = g * heads_per_kv + r
                m_old = m_ref[h]
                m_new = jnp.maximum(m_old, reduce_keys(s_ref[h, 0:nk, :], jnp.max))
                alpha = jnp.exp2(m_old - m_new)
                p = jnp.exp2(s_ref[h, 0:nk, :] - m_new[0:1])
                l_ref[h] = alpha * l_ref[h] + reduce_keys(p, jnp.sum)
                m_ref[h] = m_new
                p_ref[g, 0:nk, r * tq:(r + 1) * tq] = p.astype(p_ref.dtype)
                alphas.append(alpha[0:1])
            pv = jnp.dot(vt_ref[cs, rows], p_ref[g, 0:nk, :], preferred_element_type=f32)
            for r in range(heads_per_kv):
                h = g * heads_per_kv + r
                acc_ref[h] = alphas[r] * acc_ref[h] + pv[:, r * tq:(r + 1) * tq]

    loop((w0 + COUNT_TILE - 1) // COUNT_TILE, lambda j: attend(j * COUNT_TILE, COUNT_TILE, False))
    attend(w0, NEAR_KEYS, True)

    for h in range(N_HEADS):
        out_t = acc_ref[h] / l_ref[h][0:1]
        o_ref[:, h * HEAD_DIM:(h + 1) * HEAD_DIM] = out_t.T.astype(o_ref.dtype)


def _attention(q_hm, qi_hm, w_t, k_all, vt_all, ki_all, bias_tiles, *, n_batch, n_qblocks, t0_base, t0_step, topk):
    tq = Q_TILE
    n_keys = k_all.shape[1]
    kv_w = N_KV_HEADS * HEAD_DIM
    assert n_keys % COUNT_TILE == 0 and tq == MAX_DISTANCE and NEAR_KEYS == SCORE_TILE
    assert t0_base % MAX_DISTANCE == 0 and t0_step % MAX_DISTANCE == 0
    last_w0 = max(t0_base + (n_qblocks - 1) * t0_step - MAX_DISTANCE, 0)
    assert last_w0 + NEAR_KEYS <= n_keys
    kernel = functools.partial(_attn_kernel, t0_base=t0_base, t0_step=t0_step, topk=float(topk),
                               idx_scale=(H_IDX * D_IDX) ** -0.5)
    once = pl.Buffered(1)
    return pl.pallas_call(
        kernel,
        out_shape=jax.ShapeDtypeStruct((n_batch * n_qblocks * tq, N_HEADS * HEAD_DIM), _MXU_DTYPE),
        grid=(n_batch, n_qblocks),
        in_specs=[
            pl.BlockSpec((N_HEADS, tq, HEAD_DIM), lambda b, i: (0, b * n_qblocks + i, 0)),
            pl.BlockSpec((H_IDX, tq, LANES), lambda b, i: (0, b * n_qblocks + i, 0)),
            pl.BlockSpec((H_IDX, tq), lambda b, i: (0, b * n_qblocks + i)),
            pl.BlockSpec((None, n_keys, kv_w), lambda b, i: (b, 0, 0), pipeline_mode=once),
            pl.BlockSpec((None, kv_w, n_keys), lambda b, i: (b, 0, 0), pipeline_mode=once),
            pl.BlockSpec((None, n_keys, LANES), lambda b, i: (b, 0, 0), pipeline_mode=once),
            pl.BlockSpec((2, N_HEADS, NEAR_KEYS, tq), lambda b, i: (0, 0, 0, 0), pipeline_mode=once),
        ],
        out_specs=pl.BlockSpec((tq, N_HEADS * HEAD_DIM), lambda b, i: (b * n_qblocks + i, 0)),
        scratch_shapes=[
            pltpu.VMEM((n_keys + COUNT_TILE, tq), jnp.int32),
            pltpu.VMEM((N_HEADS, SUBLANES, tq), jnp.float32),
            pltpu.VMEM((N_HEADS, SUBLANES, tq), jnp.float32),
            pltpu.VMEM((N_HEADS, HEAD_DIM, tq), jnp.float32),
            pltpu.VMEM((N_HEADS, COUNT_TILE, tq), jnp.float32),
            pltpu.VMEM((N_KV_HEADS, COUNT_TILE, (N_HEADS // N_KV_HEADS) * tq), _MXU_DTYPE),
        ],
        compiler_params=_params("parallel", "arbitrary"),
        name="dsa_attention",
    )(q_hm, qi_hm, w_t, k_all, vt_all, ki_all, bias_tiles)


def _t5_bucket(rel):
    nb = NUM_BUCKETS // 2
    max_exact = nb // 2
    n = jnp.abs(rel)
    nf = jnp.maximum(n, max_exact).astype(jnp.float32)
    large = max_exact + (jnp.log(nf / max_exact) / math.log(MAX_DISTANCE / max_exact) * (nb - max_exact)).astype(jnp.int32)
    large = jnp.minimum(large, nb - 1)
    return jnp.where(rel > 0, nb, 0) + jnp.where(n < max_exact, n, large)


def _near_bias(rel_bias):
    d = jnp.array([-MAX_DISTANCE, 0], jnp.int32)[:, None, None]
    rel = d + jnp.arange(NEAR_KEYS, dtype=jnp.int32)[None, :, None] - jnp.arange(Q_TILE, dtype=jnp.int32)[None, None, :]
    near = jnp.transpose(rel_bias[_t5_bucket(rel)], (0, 3, 1, 2))
    far = rel_bias[_t5_bucket(jnp.int32(-MAX_DISTANCE))]
    return ((near - far[None, :, None, None]) * LOG2_E).astype(jnp.float32)


def _pad_cols(w, block, width):
    lead = w.shape[:-1]
    n = w.shape[-1] // block
    w = w.reshape(*lead, n, block)
    w = jnp.pad(w, [(0, 0)] * len(lead) + [(0, 0), (0, width - block)])
    return w.reshape(*lead, n * width)


def kernel(x_prompt, x_sample, p_prompt, p_sample, cache_k, cache_v, cache_kidx, norm_g, w_ffn1_gu, w_ffn1_down,
           w_in, w_out, gmlp_ws, gmlp_b, gmlp_vnorm, w_ffn2_gu, w_ffn2_down, w_ple_gate, w_ple, rel_bias):
    depth = w_in.shape[0]
    n_b, seq, d_model = x_prompt.shape
    dec_b, dec_seq, _ = x_sample.shape
    past = cache_k.shape[2]
    rows_p, rows_s = n_b * seq, dec_b * dec_seq
    attn_w = N_HEADS * HEAD_DIM
    kv_w = N_KV_HEADS * HEAD_DIM
    idx_w = H_IDX * D_IDX
    gmlp_w = d_model - attn_w
    mx = _MXU_DTYPE
    assert seq % COUNT_TILE == 0 and dec_seq <= CHUNK and past % MAX_DISTANCE == 0 and dec_seq % SUBLANES == 0
    assert rows_p % (2 * GMLP_CHUNK) == 0 and (2 * GMLP_CHUNK) % dec_seq == 0 and rows_s % (2 * GMLP_CHUNK) == 0

    o1 = attn_w
    o2 = o1 + kv_w
    o3 = o2 + kv_w
    o4 = o3 + idx_w
    o6 = o4 + D_IDX + H_IDX
    w_q = w_in[:, :, :o1].astype(mx)
    w_qi = _pad_cols(w_in[:, :, o3:o4], D_IDX, LANES).astype(mx)
    w_kvki = w_in[:, :, o1:o3]
    w_kvki = jnp.concatenate([w_kvki, w_in[:, :, o4:o6]], axis=-1)
    w_kvki = jnp.pad(w_kvki, ((0, 0), (0, 0), (0, 2 * kv_w + LANES - w_kvki.shape[-1]))).astype(mx)
    w_uv = w_in[:, :, o6:].astype(mx)
    w1_gu, w1_dn = w_ffn1_gu.astype(mx), w_ffn1_down.astype(mx)
    w2_gu, w2_dn = w_ffn2_gu.astype(mx), w_ffn2_down.astype(mx)
    w_out2 = w_out.astype(mx).reshape(depth, 2, attn_w, d_model)
    w_pg, w_pe = w_ple_gate.astype(mx), w_ple.astype(mx)

    causal = jnp.tril(jnp.ones((GMLP_CHUNK, GMLP_CHUNK), bool))
    ws_full = jnp.where(causal, gmlp_ws, 0.0)
    reps = GMLP_CHUNK // dec_seq
    small = jnp.where(causal[:dec_seq, :dec_seq], gmlp_ws[:, :, :dec_seq, :dec_seq], 0.0)
    eye = jnp.eye(reps, dtype=gmlp_ws.dtype)
    ws_small = jnp.einsum('ab,lgij->lgaibj', eye, small).reshape(depth, GMLP_GROUPS, GMLP_CHUNK, GMLP_CHUNK)
    ws2 = jnp.stack([ws_full, ws_small], axis=1).astype(mx)
    b_small = jnp.tile(gmlp_b[:, :, :dec_seq], (1, 1, reps))
    b2 = jnp.stack([gmlp_b, b_small], axis=1)
    b2 = jnp.repeat(jnp.swapaxes(b2, 2, 3), gmlp_w // GMLP_GROUPS, axis=-1)

    near_bias = _near_bias(rel_bias)

    h = jnp.concatenate([x_prompt.reshape(rows_p, d_model), x_sample.reshape(rows_s, d_model)], axis=0)
    p_all = jnp.concatenate([p_prompt.reshape(depth, rows_p, -1), p_sample.reshape(depth, rows_s, -1)], axis=1)

    keys_s = past + dec_seq
    keys_s_pad = -(-(past - MAX_DISTANCE + NEAR_KEYS) // COUNT_TILE) * COUNT_TILE
    assert keys_s_pad >= keys_s
    topk_p = min(TOPK_MAX, seq // 4)
    topk_s = min(TOPK_MAX, keys_s // 4)

    def sample_queries(x_hm):
        x = x_hm[:, rows_p:].reshape(x_hm.shape[0], dec_b, dec_seq, x_hm.shape[-1])
        x = jnp.pad(x, ((0, 0), (0, 0), (0, Q_TILE - dec_seq), (0, 0)))
        return x.reshape(x_hm.shape[0], dec_b * Q_TILE, x_hm.shape[-1])

    outs = [[] for _ in range(7)]
    for l in range(depth):
        g = norm_g[l][:, None, :]
        h = _ffn(h, g[0], g[1], w1_gu, w1_dn, l)

        q_hm = _proj(h, g[2], w_q[l], mx, True, "proj_q", out_scale=HEAD_DIM ** -0.5 * LOG2_E)
        qi_hm = _proj(h, g[2], w_qi[l], mx, True, "proj_qi")
        kvki = _proj(h, g[2], w_kvki[l], jnp.float32, False, "proj_kv")
        uv = _proj(h, g[2], w_uv[l], jnp.float32, False, "proj_uv")

        k_new, v_new, ki_new = kvki[:, :kv_w], kvki[:, kv_w:2 * kv_w], kvki[:, 2 * kv_w:2 * kv_w + D_IDX]
        w_t = kvki[:, 2 * kv_w + D_IDX:2 * kv_w + D_IDX + H_IDX].T

        def keys_p(x, width):
            x = x[:rows_p].reshape(n_b, seq, -1)
            return jnp.pad(x, ((0, 0), (0, 0), (0, width - x.shape[-1]))).astype(mx)

        attn_p = _attention(
            q_hm, qi_hm, w_t, keys_p(k_new, kv_w), jnp.swapaxes(keys_p(v_new, kv_w), 1, 2), keys_p(ki_new, LANES),
            near_bias, n_batch=n_b, n_qblocks=seq // Q_TILE, t0_base=0, t0_step=Q_TILE, topk=topk_p)

        def keys_s_all(cache, new, width):
            full = jnp.concatenate([cache.reshape(dec_b, past, -1), new[rows_p:].reshape(dec_b, dec_seq, -1)], axis=1)
            full = jnp.pad(full, ((0, 0), (0, keys_s_pad - keys_s), (0, width - full.shape[-1])))
            return full.astype(mx)

        w_t_s = jnp.pad(w_t[:, rows_p:].reshape(H_IDX, dec_b, dec_seq), ((0, 0), (0, 0), (0, Q_TILE - dec_seq)))
        attn_s = _attention(
            sample_queries(q_hm), sample_queries(qi_hm), w_t_s.reshape(H_IDX, dec_b * Q_TILE),
            keys_s_all(cache_k[l], k_new, kv_w), jnp.swapaxes(keys_s_all(cache_v[l], v_new, kv_w), 1, 2),
            keys_s_all(cache_kidx[l], ki_new, LANES),
            near_bias, n_batch=dec_b, n_qblocks=1, t0_base=past, t0_step=0, topk=topk_s)
        attn_s = attn_s.reshape(dec_b, Q_TILE, attn_w)[:, :dec_seq].reshape(rows_s, attn_w)
        attn = jnp.concatenate([attn_p, attn_s], axis=0)

        gm, vn = _gmlp(uv, gmlp_vnorm[l][None, :], ws2[l], b2[l], rows_p)
        h = _mixout(h, attn, gm, w_out2, g[3], l)
        h = _ffn(h, g[4], g[5], w2_gu, w2_dn, l)
        h = _ple(h, p_all[l], g[6], g[7], w_pg, w_pe, l)

        outs[0].append(k_new[:rows_p].reshape(n_b, seq, N_KV_HEADS, HEAD_DIM))
        outs[1].append(v_new[:rows_p].reshape(n_b, seq, N_KV_HEADS, HEAD_DIM))
        outs[2].append(ki_new[:rows_p].reshape(n_b, seq, D_IDX))
        outs[3].append(k_new[rows_p:].reshape(dec_b, dec_seq, N_KV_HEADS, HEAD_DIM))
        outs[4].append(v_new[rows_p:].reshape(dec_b, dec_seq, N_KV_HEADS, HEAD_DIM))
        outs[5].append(ki_new[rows_p:].reshape(dec_b, dec_seq, D_IDX))
        outs[6].append(vn[rows_p:].reshape(dec_b, dec_seq, gmlp_w))

    y_prompt = h[:rows_p].reshape(n_b, seq, d_model)
    y_sample = h[rows_p:].reshape(dec_b, dec_seq, d_model)
    return (y_prompt, y_sample) + tuple(jnp.stack(o) for o in outs)
```

```python
import functools
import math

import jax
import jax.numpy as jnp
from jax import lax
from jax.experimental import pallas as pl
from jax.experimental.pallas import tpu as pltpu

CHUNK = 64
N_HEADS = 8
N_KV_HEADS = 2
HEAD_DIM = 128
H_IDX = 16
D_IDX = 64
TOPK_MAX = 256
GMLP_CHUNK = 128
GMLP_GROUPS = 8
NUM_BUCKETS = 32
MAX_DISTANCE = 128
EPS = 1e-6
LOG2_E = 1.4426950408889634

LANES = 128
SUBLANES = 8
VMEM_LIMIT_BYTES = 56 * 1024 * 1024

ROW_TILE = 512
FF_TILE = 512
Q_TILE = 128
SCORE_TILE = 256
COUNT_TILE = 512
NEAR_KEYS = 2 * MAX_DISTANCE

_MXU_DTYPE = jnp.bfloat16
_INT_MIN = -(2 ** 31)
_HALF = 2 ** 15


def _params(*semantics):
    return pltpu.CompilerParams(dimension_semantics=semantics, vmem_limit_bytes=VMEM_LIMIT_BYTES)


def _rms(x, g):
    return x * lax.rsqrt(jnp.mean(x * x, axis=-1, keepdims=True) + EPS) * g


def _ffn_kernel(h_ref, gpre_ref, gpost_ref, wg_ref, wu_ref, wd_ref, o_ref, xn_ref, acc_ref):
    j = pl.program_id(1)

    @pl.when(j == 0)
    def _():
        xn_ref[...] = _rms(h_ref[...], gpre_ref[...]).astype(xn_ref.dtype)
        acc_ref[...] = jnp.zeros_like(acc_ref)

    xn = xn_ref[...]
    gate = jnp.dot(xn, wg_ref[...], preferred_element_type=jnp.float32)
    up = jnp.dot(xn, wu_ref[...], preferred_element_type=jnp.float32)
    act = (jax.nn.silu(gate) * up).astype(wd_ref.dtype)
    acc_ref[...] += jnp.dot(act, wd_ref[...], preferred_element_type=jnp.float32)

    @pl.when(j == pl.num_programs(1) - 1)
    def _():
        o_ref[...] = h_ref[...] + 0.5 * _rms(acc_ref[...], gpost_ref[...])


def _ffn(h, g_pre, g_post, w_gu, w_down, layer):
    rows, d = h.shape
    d_ff = w_down.shape[1]
    nf = d_ff // FF_TILE
    assert rows % ROW_TILE == 0 and d_ff % FF_TILE == 0
    return pl.pallas_call(
        _ffn_kernel,
        out_shape=jax.ShapeDtypeStruct((rows, d), jnp.float32),
        grid=(rows // ROW_TILE, nf),
        in_specs=[
            pl.BlockSpec((ROW_TILE, d), lambda i, j: (i, 0)),
            pl.BlockSpec((1, d), lambda i, j: (0, 0)),
            pl.BlockSpec((1, d), lambda i, j: (0, 0)),
            pl.BlockSpec((None, d, FF_TILE), lambda i, j: (layer, 0, j)),
            pl.BlockSpec((None, d, FF_TILE), lambda i, j: (layer, 0, j + nf)),
            pl.BlockSpec((None, FF_TILE, d), lambda i, j: (layer, j, 0)),
        ],
        out_specs=pl.BlockSpec((ROW_TILE, d), lambda i, j: (i, 0)),
        scratch_shapes=[pltpu.VMEM((ROW_TILE, d), _MXU_DTYPE), pltpu.VMEM((ROW_TILE, d), jnp.float32)],
        compiler_params=_params("parallel", "arbitrary"),
        name="ffn",
    )(h, g_pre, g_post, w_gu, w_gu, w_down)


def _proj_kernel(h_ref, g_ref, w_ref, o_ref, *, head_major, out_scale):
    xn = _rms(h_ref[...], g_ref[...]).astype(w_ref.dtype)
    z = jnp.dot(xn, w_ref[...], preferred_element_type=jnp.float32)
    if out_scale != 1.0:
        z = z * out_scale
    if head_major:
        for hh in range(o_ref.shape[0]):
            o_ref[hh] = z[:, hh * LANES:(hh + 1) * LANES].astype(o_ref.dtype)
    else:
        o_ref[...] = z.astype(o_ref.dtype)


def _proj(h, g, w, out_dtype, head_major, name, out_scale=1.0):
    rows, d = h.shape
    n = w.shape[1]
    if head_major:
        out_shape = jax.ShapeDtypeStruct((n // LANES, rows, LANES), out_dtype)
        out_spec = pl.BlockSpec((n // LANES, ROW_TILE, LANES), lambda i: (0, i, 0))
    else:
        out_shape = jax.ShapeDtypeStruct((rows, n), out_dtype)
        out_spec = pl.BlockSpec((ROW_TILE, n), lambda i: (i, 0))
    return pl.pallas_call(
        functools.partial(_proj_kernel, head_major=head_major, out_scale=out_scale),
        out_shape=out_shape,
        grid=(rows // ROW_TILE,),
        in_specs=[
            pl.BlockSpec((ROW_TILE, d), lambda i: (i, 0)),
            pl.BlockSpec((1, d), lambda i: (0, 0)),
            pl.BlockSpec((d, n), lambda i: (0, 0)),
        ],
        out_specs=out_spec,
        compiler_params=_params("parallel"),
        name=name,
    )(h, g, w)


def _gmlp_kernel(uv_ref, gain_ref, ws_ref, b_ref, o_ref, vn_ref):
    width = o_ref.shape[1]
    gdim = width // GMLP_GROUPS
    for c in range(uv_ref.shape[0] // GMLP_CHUNK):
        rs = slice(c * GMLP_CHUNK, (c + 1) * GMLP_CHUNK)
        u = jax.nn.gelu(uv_ref[rs, :width])
        vn = _rms(jax.nn.gelu(uv_ref[rs, width:]), gain_ref[...])
        vn_ref[rs, :] = vn
        vnb = vn.astype(ws_ref.dtype)
        for g in range(GMLP_GROUPS):
            cs = slice(g * gdim, (g + 1) * gdim)
            s = jnp.dot(ws_ref[g], vnb[:, cs], preferred_element_type=jnp.float32) + b_ref[:, cs]
            o_ref[rs, cs] = (u[:, cs] * s).astype(o_ref.dtype)


def _gmlp(uv, gain, ws2, b2, n_prompt_rows):
    rows = uv.shape[0]
    width = uv.shape[1] // 2
    tile = 2 * GMLP_CHUNK
    n_prompt_tiles = n_prompt_rows // tile

    def sel(i):
        return jnp.where(i >= n_prompt_tiles, 1, 0)

    return pl.pallas_call(
        _gmlp_kernel,
        out_shape=(jax.ShapeDtypeStruct((rows, width), _MXU_DTYPE),
                   jax.ShapeDtypeStruct((rows, width), jnp.float32)),
        grid=(rows // tile,),
        in_specs=[
            pl.BlockSpec((tile, 2 * width), lambda i: (i, 0)),
            pl.BlockSpec((1, width), lambda i: (0, 0)),
            pl.BlockSpec((None, GMLP_GROUPS, GMLP_CHUNK, GMLP_CHUNK), lambda i: (sel(i), 0, 0, 0)),
            pl.BlockSpec((None, GMLP_CHUNK, width), lambda i: (sel(i), 0, 0)),
        ],
        out_specs=(pl.BlockSpec((tile, width), lambda i: (i, 0)),
                   pl.BlockSpec((tile, width), lambda i: (i, 0))),
        compiler_params=_params("parallel"),
        name="gmlp",
    )(uv, gain, ws2, b2)


def _mixout_kernel(h_ref, a_ref, m_ref, w_ref, g_ref, o_ref):
    y = jnp.dot(a_ref[...], w_ref[0], preferred_element_type=jnp.float32)
    y = y + jnp.dot(m_ref[...], w_ref[1], preferred_element_type=jnp.float32)
    o_ref[...] = h_ref[...] + _rms(y, g_ref[...])


def _mixout(h, attn, gm, w_out2, g, layer):
    rows, d = h.shape
    half = attn.shape[1]
    return pl.pallas_call(
        _mixout_kernel,
        out_shape=jax.ShapeDtypeStruct((rows, d), jnp.float32),
        grid=(rows // ROW_TILE,),
        in_specs=[
            pl.BlockSpec((ROW_TILE, d), lambda i: (i, 0)),
            pl.BlockSpec((ROW_TILE, half), lambda i: (i, 0)),
            pl.BlockSpec((ROW_TILE, half), lambda i: (i, 0)),
            pl.BlockSpec((None, 2, half, d), lambda i: (layer, 0, 0, 0)),
            pl.BlockSpec((1, d), lambda i: (0, 0)),
        ],
        out_specs=pl.BlockSpec((ROW_TILE, d), lambda i: (i, 0)),
        compiler_params=_params("parallel"),
        name="mixout",
    )(h, attn, gm, w_out2, g)


def _ple_kernel(h_ref, p_ref, gpre_ref, gpost_ref, wpg_ref, wpe_ref, o_ref):
    h = h_ref[...]
    xn = _rms(h, gpre_ref[...]).astype(wpg_ref.dtype)
    gate = jax.nn.sigmoid(jnp.dot(xn, wpg_ref[...], preferred_element_type=jnp.float32))
    emb = jnp.dot(p_ref[...].astype(wpe_ref.dtype), wpe_ref[...], preferred_element_type=jnp.float32)
    o_ref[...] = h + _rms(gate * emb, gpost_ref[...])


def _ple(h, p, g_pre, g_post, w_pg, w_pe, layer):
    rows, d = h.shape
    d_ple = p.shape[1]
    return pl.pallas_call(
        _ple_kernel,
        out_shape=jax.ShapeDtypeStruct((rows, d), jnp.float32),
        grid=(rows // ROW_TILE,),
        in_specs=[
            pl.BlockSpec((ROW_TILE, d), lambda i: (i, 0)),
            pl.BlockSpec((ROW_TILE, d_ple), lambda i: (i, 0)),
            pl.BlockSpec((1, d), lambda i: (0, 0)),
            pl.BlockSpec((1, d), lambda i: (0, 0)),
            pl.BlockSpec((None, d, d), lambda i: (layer, 0, 0)),
            pl.BlockSpec((None, d_ple, d), lambda i: (layer, 0, 0)),
        ],
        out_specs=pl.BlockSpec((ROW_TILE, d), lambda i: (i, 0)),
        compiler_params=_params("parallel"),
        name="ple",
    )(h, p, g_pre, g_post, w_pg, w_pe)


def _attn_kernel(q_ref, qi_ref, w_ref, k_ref, vt_ref, ki_ref, bias_ref, o_ref,
                 keys_ref, hi_ref, lo_ref, m_ref, l_ref, acc_ref, s_ref, p_ref, *, t0_base, t0_step, topk, idx_scale):
    f32 = jnp.float32
    tq = Q_TILE
    heads_per_kv = N_HEADS // N_KV_HEADS
    nt = (((1,), (1,)), ((), ()))
    t0 = t0_base + pl.program_id(1) * t0_step
    w0 = jnp.maximum(t0 - MAX_DISTANCE, 0)

    def loop(n, fn):
        lax.fori_loop(0, n, lambda j, c: (fn(j), c)[1], 0)

    def reduce_keys(x, op):
        x = op(x.reshape(-1, SUBLANES * SUBLANES, tq), axis=0)
        x = op(x.reshape(SUBLANES, SUBLANES, tq), axis=0)
        return op(x, axis=0, keepdims=True)

    def score_rows(r0, near):
        r0 = pl.multiple_of(r0, MAX_DISTANCE)
        qs = qi_ref[...].reshape(H_IDX * tq, LANES)
        d = lax.dot_general(ki_ref[pl.ds(r0, SCORE_TILE), :], qs, nt, preferred_element_type=f32)
        score = jnp.zeros((SCORE_TILE, tq), f32)
        for h in range(H_IDX):
            score = score + (w_ref[h:h + 1, :] * idx_scale) * jnp.maximum(d[:, h * tq:(h + 1) * tq], 0.0)
        bits = lax.bitcast_convert_type(score, jnp.int32)
        key = bits ^ ((bits >> 31) & 0x7FFFFFFF)
        if near:
            s_pos = r0 + lax.broadcasted_iota(jnp.int32, (SCORE_TILE, tq), 0)
            t_pos = t0 + lax.broadcasted_iota(jnp.int32, (SCORE_TILE, tq), 1)
            key = jnp.where((s_pos // CHUNK) <= (t_pos // CHUNK), key, _INT_MIN)
        keys_ref[pl.ds(r0, SCORE_TILE), :] = key
        hi_ref[pl.ds(r0, SCORE_TILE), :] = (key >> 16).astype(jnp.int16)
        lo_ref[pl.ds(r0, SCORE_TILE), :] = ((key & 0xFFFF) - _HALF).astype(jnp.int16)

    def score_pair(j):
        score_rows(2 * j * SCORE_TILE, False)
        score_rows((2 * j + 1) * SCORE_TILE, False)

    loop((w0 + 2 * SCORE_TILE - 1) // (2 * SCORE_TILE), score_pair)
    score_rows(w0, True)
    pad_rows = COUNT_TILE - MAX_DISTANCE
    pad = pl.ds(pl.multiple_of(w0 + NEAR_KEYS, MAX_DISTANCE), pad_rows)
    keys_ref[pad, :] = jnp.full((pad_rows, tq), _INT_MIN, jnp.int32)
    hi_ref[pad, :] = jnp.full((pad_rows, tq), -_HALF, jnp.int16)
    lo_ref[pad, :] = jnp.full((pad_rows, tq), -_HALF, jnp.int16)

    n_count = (w0 + NEAR_KEYS + COUNT_TILE - 1) // COUNT_TILE
    wide = SUBLANES * SUBLANES
    n_wide = COUNT_TILE // wide

    def rep16(x):
        return jnp.concatenate([x.astype(jnp.int16)] * (wide // SUBLANES), axis=0)

    def tile_rows(j):
        return pl.ds(pl.multiple_of(j * COUNT_TILE, COUNT_TILE), COUNT_TILE)

    def count_ge(ref, cand):
        cand = rep16(cand)

        def body(j, cnt):
            k = ref[tile_rows(j), :].reshape(n_wide, wide, tq)
            hit = jnp.where(k >= cand[None], jnp.int16(1), jnp.int16(0))
            for i in range(n_wide):
                cnt = cnt + hit[i]
            return cnt

        cnt = lax.fori_loop(0, n_count, body, jnp.zeros((wide, tq), jnp.int16))
        return jnp.broadcast_to(reduce_keys(cnt.astype(f32), jnp.sum), (SUBLANES, tq))

    def search_step(ref, base, step, state):
        thr_u, at_thr = state
        cand_u = thr_u | jnp.left_shift(jnp.int32(1), 15 - step)
        cnt = base + count_ge(ref, cand_u - _HALF)
        take = cnt >= topk
        return jnp.where(take, cand_u, thr_u), jnp.where(take, cnt, at_thr)

    zeros = jnp.zeros((SUBLANES, tq), jnp.int32)
    n_admissible = count_ge(hi_ref, zeros - (_HALF - 1))
    thr_hi, at_thr = lax.fori_loop(0, 16, functools.partial(search_step, hi_ref, 0.0), (zeros, n_admissible))
    thr_hi = thr_hi - _HALF

    top = thr_hi == _HALF - 1
    above = jnp.where(top, 0.0, count_ge(hi_ref, jnp.where(top, thr_hi, thr_hi + 1)))
    thr_hi16 = rep16(thr_hi)

    def keep_equal(j):
        rows = tile_rows(j)
        hi = hi_ref[rows, :].reshape(n_wide, wide, tq)
        lo = lo_ref[rows, :].reshape(n_wide, wide, tq)
        lo_ref[rows, :] = jnp.where(hi == thr_hi16[None], lo, jnp.int16(-_HALF)).reshape(COUNT_TILE, tq)

    loop(n_count, keep_equal)

    steps_per_test = 4

    def unsettled(state):
        step, _, at_thr = state
        settled = jnp.where(jnp.logical_or(at_thr == topk, n_admissible <= topk), 1.0, 0.0)
        return jnp.logical_and(step < 16, jnp.sum(settled) < float(SUBLANES * tq))

    def lower_steps(state):
        step, thr_u, at_thr = state
        for i in range(steps_per_test):
            thr_u, at_thr = search_step(lo_ref, above, step + i, (thr_u, at_thr))
        return step + steps_per_test, thr_u, at_thr

    _, thr_lo, _ = lax.while_loop(unsettled, lower_steps, (jnp.int32(0), zeros, at_thr))
    thr = jnp.maximum(thr_hi * (2 * _HALF) + thr_lo, _INT_MIN + 1)[0:1]

    m_ref[...] = jnp.full(m_ref.shape, -1e30, f32)
    l_ref[...] = jnp.zeros(l_ref.shape, f32)
    acc_ref[...] = jnp.zeros(acc_ref.shape, f32)

    def attend(r0, nk, near):
        r0 = pl.multiple_of(r0, MAX_DISTANCE)
        rows = pl.ds(r0, nk)
        selected = keys_ref[rows, :] >= thr
        mask_bias = jnp.where(selected, 0.0, -jnp.inf)
        if not near:
            pos = r0 + lax.broadcasted_iota(jnp.int32, (nk, tq), 0)
            mask_bias = jnp.where(pos < w0, mask_bias, -jnp.inf)
        for g in range(N_KV_HEADS):
            cs = slice(g * HEAD_DIM, (g + 1) * HEAD_DIM)
            qg = q_ref[g * heads_per_kv:(g + 1) * heads_per_kv].reshape(heads_per_kv * tq, HEAD_DIM)
            s_all = lax.dot_general(k_ref[rows, cs], qg, nt, preferred_element_type=f32)
            for r in range(heads_per_kv):
                h = g * heads_per_kv + r
                s = s_all[:, r * tq:(r + 1) * tq] + mask_bias
                if near:
                    s = s + bias_ref[jnp.where(t0 == w0, 1, 0), h]
                s_ref[h, 0:nk, :] = s
        for g in range(N_KV_HEADS):
            cs = slice(g * HEAD_DIM, (g + 1) * HEAD_DIM)
            alphas = []
            for r in range(heads_per_kv):
                h = g * heads_per_kv + r
                m_old = m_ref[h]
                m_new = jnp.maximum(m_old, reduce_keys(s_ref[h, 0:nk, :], jnp.max))
                alpha = jnp.exp2(m_old - m_new)
                p = jnp.exp2(s_ref[h, 0:nk, :] - m_new[0:1])
                l_ref[h] = alpha * l_ref[h] + reduce_keys(p, jnp.sum)
                m_ref[h] = m_new
                p_ref[g, 0:nk, r * tq:(r + 1) * tq] = p.astype(p_ref.dtype)
                alphas.append(alpha[0:1])
            pv = jnp.dot(vt_ref[cs, rows], p_ref[g, 0:nk, :], preferred_element_type=f32)
            for r in range(heads_per_kv):
                h = g * heads_per_kv + r
                acc_ref[h] = alphas[r] * acc_ref[h] + pv[:, r * tq:(r + 1) * tq]

    loop((w0 + COUNT_TILE - 1) // COUNT_TILE, lambda j: attend(j * COUNT_TILE, COUNT_TILE, False))
    attend(w0, NEAR_KEYS, True)

    for h in range(N_HEADS):
        out_t = acc_ref[h] / l_ref[h][0:1]
        o_ref[:, h * HEAD_DIM:(h + 1) * HEAD_DIM] = out_t.T.astype(o_ref.dtype)


def _attention(q_hm, qi_hm, w_t, k_all, vt_all, ki_all, bias_tiles, *, n_batch, n_qblocks, t0_base, t0_step, topk):
    tq = Q_TILE
    n_keys = k_all.shape[1]
    kv_w = N_KV_HEADS * HEAD_DIM
    assert n_keys % COUNT_TILE == 0 and tq == MAX_DISTANCE and NEAR_KEYS == SCORE_TILE
    assert t0_base % MAX_DISTANCE == 0 and t0_step % MAX_DISTANCE == 0
    last_w0 = max(t0_base + (n_qblocks - 1) * t0_step - MAX_DISTANCE, 0)
    assert last_w0 + NEAR_KEYS <= n_keys
    kernel = functools.partial(_attn_kernel, t0_base=t0_base, t0_step=t0_step, topk=float(topk),
                               idx_scale=(H_IDX * D_IDX) ** -0.5)
    once = pl.Buffered(1)
    return pl.pallas_call(
        kernel,
        out_shape=jax.ShapeDtypeStruct((n_batch * n_qblocks * tq, N_HEADS * HEAD_DIM), _MXU_DTYPE),
        grid=(n_batch, n_qblocks),
        in_specs=[
            pl.BlockSpec((N_HEADS, tq, HEAD_DIM), lambda b, i: (0, b * n_qblocks + i, 0)),
            pl.BlockSpec((H_IDX, tq, LANES), lambda b, i: (0, b * n_qblocks + i, 0)),
            pl.BlockSpec((H_IDX, tq), lambda b, i: (0, b * n_qblocks + i)),
            pl.BlockSpec((None, n_keys, kv_w), lambda b, i: (b, 0, 0), pipeline_mode=once),
            pl.BlockSpec((None, kv_w, n_keys), lambda b, i: (b, 0, 0), pipeline_mode=once),
            pl.BlockSpec((None, n_keys, LANES), lambda b, i: (b, 0, 0), pipeline_mode=once),
            pl.BlockSpec((2, N_HEADS, NEAR_KEYS, tq), lambda b, i: (0, 0, 0, 0), pipeline_mode=once),
        ],
        out_specs=pl.BlockSpec((tq, N_HEADS * HEAD_DIM), lambda b, i: (b * n_qblocks + i, 0)),
        scratch_shapes=[
            pltpu.VMEM((n_keys + COUNT_TILE, tq), jnp.int32),
            pltpu.VMEM((n_keys + COUNT_TILE, tq), jnp.int16),
            pltpu.VMEM((n_keys + COUNT_TILE, tq), jnp.int16),
            pltpu.VMEM((N_HEADS, SUBLANES, tq), jnp.float32),
            pltpu.VMEM((N_HEADS, SUBLANES, tq), jnp.float32),
            pltpu.VMEM((N_HEADS, HEAD_DIM, tq), jnp.float32),
            pltpu.VMEM((N_HEADS, COUNT_TILE, tq), jnp.float32),
            pltpu.VMEM((N_KV_HEADS, COUNT_TILE, (N_HEADS // N_KV_HEADS) * tq), _MXU_DTYPE),
        ],
        compiler_params=_params("parallel", "arbitrary"),
        name="dsa_attention",
    )(q_hm, qi_hm, w_t, k_all, vt_all, ki_all, bias_tiles)


def _t5_bucket(rel):
    nb = NUM_BUCKETS // 2
    max_exact = nb // 2
    n = jnp.abs(rel)
    nf = jnp.maximum(n, max_exact).astype(jnp.float32)
    large = max_exact + (jnp.log(nf / max_exact) / math.log(MAX_DISTANCE / max_exact) * (nb - max_exact)).astype(jnp.int32)
    large = jnp.minimum(large, nb - 1)
    return jnp.where(rel > 0, nb, 0) + jnp.where(n < max_exact, n, large)


def _near_bias(rel_bias):
    d = jnp.array([-MAX_DISTANCE, 0], jnp.int32)[:, None, None]
    rel = d + jnp.arange(NEAR_KEYS, dtype=jnp.int32)[None, :, None] - jnp.arange(Q_TILE, dtype=jnp.int32)[None, None, :]
    near = jnp.transpose(rel_bias[_t5_bucket(rel)], (0, 3, 1, 2))
    far = rel_bias[_t5_bucket(jnp.int32(-MAX_DISTANCE))]
    return ((near - far[None, :, None, None]) * LOG2_E).astype(jnp.float32)


def _pad_cols(w, block, width):
    lead = w.shape[:-1]
    n = w.shape[-1] // block
    w = w.reshape(*lead, n, block)
    w = jnp.pad(w, [(0, 0)] * len(lead) + [(0, 0), (0, width - block)])
    return w.reshape(*lead, n * width)


def kernel(x_prompt, x_sample, p_prompt, p_sample, cache_k, cache_v, cache_kidx, norm_g, w_ffn1_gu, w_ffn1_down,
           w_in, w_out, gmlp_ws, gmlp_b, gmlp_vnorm, w_ffn2_gu, w_ffn2_down, w_ple_gate, w_ple, rel_bias):
    depth = w_in.shape[0]
    n_b, seq, d_model = x_prompt.shape
    dec_b, dec_seq, _ = x_sample.shape
    past = cache_k.shape[2]
    rows_p, rows_s = n_b * seq, dec_b * dec_seq
    attn_w = N_HEADS * HEAD_DIM
    kv_w = N_KV_HEADS * HEAD_DIM
    idx_w = H_IDX * D_IDX
    gmlp_w = d_model - attn_w
    mx = _MXU_DTYPE
    assert seq % COUNT_TILE == 0 and dec_seq <= CHUNK and past % MAX_DISTANCE == 0 and dec_seq % SUBLANES == 0
    assert rows_p % (2 * GMLP_CHUNK) == 0 and (2 * GMLP_CHUNK) % dec_seq == 0 and rows_s % (2 * GMLP_CHUNK) == 0

    o1 = attn_w
    o2 = o1 + kv_w
    o3 = o2 + kv_w
    o4 = o3 + idx_w
    o6 = o4 + D_IDX + H_IDX
    w_q = w_in[:, :, :o1].astype(mx)
    w_qi = _pad_cols(w_in[:, :, o3:o4], D_IDX, LANES).astype(mx)
    w_kvki = w_in[:, :, o1:o3]
    w_kvki = jnp.concatenate([w_kvki, w_in[:, :, o4:o6]], axis=-1)
    w_kvki = jnp.pad(w_kvki, ((0, 0), (0, 0), (0, 2 * kv_w + LANES - w_kvki.shape[-1]))).astype(mx)
    w_uv = w_in[:, :, o6:].astype(mx)
    w1_gu, w1_dn = w_ffn1_gu.astype(mx), w_ffn1_down.astype(mx)
    w2_gu, w2_dn = w_ffn2_gu.astype(mx), w_ffn2_down.astype(mx)
    w_out2 = w_out.astype(mx).reshape(depth, 2, attn_w, d_model)
    w_pg, w_pe = w_ple_gate.astype(mx), w_ple.astype(mx)

    causal = jnp.tril(jnp.ones((GMLP_CHUNK, GMLP_CHUNK), bool))
    ws_full = jnp.where(causal, gmlp_ws, 0.0)
    reps = GMLP_CHUNK // dec_seq
    small = jnp.where(causal[:dec_seq, :dec_seq], gmlp_ws[:, :, :dec_seq, :dec_seq], 0.0)
    eye = jnp.eye(reps, dtype=gmlp_ws.dtype)
    ws_small = jnp.einsum('ab,lgij->lgaibj', eye, small).reshape(depth, GMLP_GROUPS, GMLP_CHUNK, GMLP_CHUNK)
    ws2 = jnp.stack([ws_full, ws_small], axis=1).astype(mx)
    b_small = jnp.tile(gmlp_b[:, :, :dec_seq], (1, 1, reps))
    b2 = jnp.stack([gmlp_b, b_small], axis=1)
    b2 = jnp.repeat(jnp.swapaxes(b2, 2, 3), gmlp_w // GMLP_GROUPS, axis=-1)

    near_bias = _near_bias(rel_bias)

    h = jnp.concatenate([x_prompt.reshape(rows_p, d_model), x_sample.reshape(rows_s, d_model)], axis=0)
    p_all = jnp.concatenate([p_prompt.reshape(depth, rows_p, -1), p_sample.reshape(depth, rows_s, -1)], axis=1)

    keys_s = past + dec_seq
    keys_s_pad = -(-(past - MAX_DISTANCE + NEAR_KEYS) // COUNT_TILE) * COUNT_TILE
    assert keys_s_pad >= keys_s
    topk_p = min(TOPK_MAX, seq // 4)
    topk_s = min(TOPK_MAX, keys_s // 4)

    def sample_queries(x_hm):
        x = x_hm[:, rows_p:].reshape(x_hm.shape[0], dec_b, dec_seq, x_hm.shape[-1])
        x = jnp.pad(x, ((0, 0), (0, 0), (0, Q_TILE - dec_seq), (0, 0)))
        return x.reshape(x_hm.shape[0], dec_b * Q_TILE, x_hm.shape[-1])

    outs = [[] for _ in range(7)]
    for l in range(depth):
        g = norm_g[l][:, None, :]
        h = _ffn(h, g[0], g[1], w1_gu, w1_dn, l)

        q_hm = _proj(h, g[2], w_q[l], mx, True, "proj_q", out_scale=HEAD_DIM ** -0.5 * LOG2_E)
        qi_hm = _proj(h, g[2], w_qi[l], mx, True, "proj_qi")
        kvki = _proj(h, g[2], w_kvki[l], jnp.float32, False, "proj_kv")
        uv = _proj(h, g[2], w_uv[l], jnp.float32, False, "proj_uv")

        k_new, v_new, ki_new = kvki[:, :kv_w], kvki[:, kv_w:2 * kv_w], kvki[:, 2 * kv_w:2 * kv_w + D_IDX]
        w_t = kvki[:, 2 * kv_w + D_IDX:2 * kv_w + D_IDX + H_IDX].T

        def keys_p(x, width):
            x = x[:rows_p].reshape(n_b, seq, -1)
            return jnp.pad(x, ((0, 0), (0, 0), (0, width - x.shape[-1]))).astype(mx)

        attn_p = _attention(
            q_hm, qi_hm, w_t, keys_p(k_new, kv_w), jnp.swapaxes(keys_p(v_new, kv_w), 1, 2), keys_p(ki_new, LANES),
            near_bias, n_batch=n_b, n_qblocks=seq // Q_TILE, t0_base=0, t0_step=Q_TILE, topk=topk_p)

        def keys_s_all(cache, new, width):
            full = jnp.concatenate([cache.reshape(dec_b, past, -1), new[rows_p:].reshape(dec_b, dec_seq, -1)], axis=1)
            full = jnp.pad(full, ((0, 0), (0, keys_s_pad - keys_s), (0, width - full.shape[-1])))
            return full.astype(mx)

        w_t_s = jnp.pad(w_t[:, rows_p:].reshape(H_IDX, dec_b, dec_seq), ((0, 0), (0, 0), (0, Q_TILE - dec_seq)))
        attn_s = _attention(
            sample_queries(q_hm), sample_queries(qi_hm), w_t_s.reshape(H_IDX, dec_b * Q_TILE),
            keys_s_all(cache_k[l], k_new, kv_w), jnp.swapaxes(keys_s_all(cache_v[l], v_new, kv_w), 1, 2),
            keys_s_all(cache_kidx[l], ki_new, LANES),
            near_bias, n_batch=dec_b, n_qblocks=1, t0_base=past, t0_step=0, topk=topk_s)
        attn_s = attn_s.reshape(dec_b, Q_TILE, attn_w)[:, :dec_seq].reshape(rows_s, attn_w)
        attn = jnp.concatenate([attn_p, attn_s], axis=0)

        gm, vn = _gmlp(uv, gmlp_vnorm[l][None, :], ws2[l], b2[l], rows_p)
        h = _mixout(h, attn, gm, w_out2, g[3], l)
        h = _ffn(h, g[4], g[5], w2_gu, w2_dn, l)
        h = _ple(h, p_all[l], g[6], g[7], w_pg, w_pe, l)

        outs[0].append(k_new[:rows_p].reshape(n_b, seq, N_KV_HEADS, HEAD_DIM))
        outs[1].append(v_new[:rows_p].reshape(n_b, seq, N_KV_HEADS, HEAD_DIM))
        outs[2].append(ki_new[:rows_p].reshape(n_b, seq, D_IDX))
        outs[3].append(k_new[rows_p:].reshape(dec_b, dec_seq, N_KV_HEADS, HEAD_DIM))
        outs[4].append(v_new[rows_p:].reshape(dec_b, dec_seq, N_KV_HEADS, HEAD_DIM))
        outs[5].append(ki_new[rows_p:].reshape(dec_b, dec_seq, D_IDX))
        outs[6].append(vn[rows_p:].reshape(dec_b, dec_seq, gmlp_w))

    y_prompt = h[:rows_p].reshape(n_b, seq, d_model)
    y_sample = h[rows_p:].reshape(dec_b, dec_seq, d_model)
    return (y_prompt, y_sample) + tuple(jnp.stack(o) for o in outs)
```

```python
import functools
import math

import jax
import jax.numpy as jnp
from jax import lax
from jax.experimental import pallas as pl
from jax.experimental.pallas import tpu as pltpu

CHUNK = 64
N_HEADS = 8
N_KV_HEADS = 2
HEAD_DIM = 128
H_IDX = 16
D_IDX = 64
TOPK_MAX = 256
GMLP_CHUNK = 128
GMLP_GROUPS = 8
NUM_BUCKETS = 32
MAX_DISTANCE = 128
EPS = 1e-6
LOG2_E = 1.4426950408889634

LANES = 128
SUBLANES = 8
VMEM_LIMIT_BYTES = 56 * 1024 * 1024

ROW_TILE = 512
FF_TILE = 512
Q_TILE = 128
SCORE_TILE = 256
COUNT_TILE = 512
ATTN_TILE = 1024
SCORE_STEP = 4
NEAR_KEYS = 2 * MAX_DISTANCE
V_ROWS = HEAD_DIM + 16

_MXU_DTYPE = jnp.bfloat16
_INT_MIN = -(2 ** 31)


def _params(*semantics):
    return pltpu.CompilerParams(dimension_semantics=semantics, vmem_limit_bytes=VMEM_LIMIT_BYTES)


def _rms(x, g):
    return x * lax.rsqrt(jnp.mean(x * x, axis=-1, keepdims=True) + EPS) * g


def _ffn_kernel(h_ref, gpre_ref, gpost_ref, wg_ref, wu_ref, wd_ref, o_ref, xn_ref, acc_ref):
    j = pl.program_id(1)

    @pl.when(j == 0)
    def _():
        xn_ref[...] = _rms(h_ref[...], gpre_ref[...]).astype(xn_ref.dtype)
        acc_ref[...] = jnp.zeros_like(acc_ref)

    xn = xn_ref[...]
    gate = jnp.dot(xn, wg_ref[...], preferred_element_type=jnp.float32)
    up = jnp.dot(xn, wu_ref[...], preferred_element_type=jnp.float32)
    act = (jax.nn.silu(gate) * up).astype(wd_ref.dtype)
    acc_ref[...] += jnp.dot(act, wd_ref[...], preferred_element_type=jnp.float32)

    @pl.when(j == pl.num_programs(1) - 1)
    def _():
        o_ref[...] = h_ref[...] + 0.5 * _rms(acc_ref[...], gpost_ref[...])


def _ffn(h, g_pre, g_post, w_gu, w_down, layer):
    rows, d = h.shape
    d_ff = w_down.shape[1]
    nf = d_ff // FF_TILE
    assert rows % ROW_TILE == 0 and d_ff % FF_TILE == 0
    return pl.pallas_call(
        _ffn_kernel,
        out_shape=jax.ShapeDtypeStruct((rows, d), jnp.float32),
        grid=(rows // ROW_TILE, nf),
        in_specs=[
            pl.BlockSpec((ROW_TILE, d), lambda i, j: (i, 0)),
            pl.BlockSpec((1, d), lambda i, j: (0, 0)),
            pl.BlockSpec((1, d), lambda i, j: (0, 0)),
            pl.BlockSpec((None, d, FF_TILE), lambda i, j: (layer, 0, j)),
            pl.BlockSpec((None, d, FF_TILE), lambda i, j: (layer, 0, j + nf)),
            pl.BlockSpec((None, FF_TILE, d), lambda i, j: (layer, j, 0)),
        ],
        out_specs=pl.BlockSpec((ROW_TILE, d), lambda i, j: (i, 0)),
        scratch_shapes=[pltpu.VMEM((ROW_TILE, d), _MXU_DTYPE), pltpu.VMEM((ROW_TILE, d), jnp.float32)],
        compiler_params=_params("parallel", "arbitrary"),
        name="ffn",
    )(h, g_pre, g_post, w_gu, w_gu, w_down)


def _proj_kernel(h_ref, g_ref, w_ref, o_ref, *, head_major, out_scale):
    xn = _rms(h_ref[...], g_ref[...]).astype(w_ref.dtype)
    z = jnp.dot(xn, w_ref[...], preferred_element_type=jnp.float32)
    if out_scale != 1.0:
        z = z * out_scale
    if head_major:
        for hh in range(o_ref.shape[0]):
            o_ref[hh] = z[:, hh * LANES:(hh + 1) * LANES].astype(o_ref.dtype)
    else:
        o_ref[...] = z.astype(o_ref.dtype)


def _proj(h, g, w, out_dtype, head_major, name, out_scale=1.0):
    rows, d = h.shape
    n = w.shape[1]
    if head_major:
        out_shape = jax.ShapeDtypeStruct((n // LANES, rows, LANES), out_dtype)
        out_spec = pl.BlockSpec((n // LANES, ROW_TILE, LANES), lambda i: (0, i, 0))
    else:
        out_shape = jax.ShapeDtypeStruct((rows, n), out_dtype)
        out_spec = pl.BlockSpec((ROW_TILE, n), lambda i: (i, 0))
    return pl.pallas_call(
        functools.partial(_proj_kernel, head_major=head_major, out_scale=out_scale),
        out_shape=out_shape,
        grid=(rows // ROW_TILE,),
        in_specs=[
            pl.BlockSpec((ROW_TILE, d), lambda i: (i, 0)),
            pl.BlockSpec((1, d), lambda i: (0, 0)),
            pl.BlockSpec((d, n), lambda i: (0, 0)),
        ],
        out_specs=out_spec,
        compiler_params=_params("parallel"),
        name=name,
    )(h, g, w)


def _gmlp_kernel(uv_ref, gain_ref, ws_ref, b_ref, o_ref, vn_ref):
    width = o_ref.shape[1]
    gdim = width // GMLP_GROUPS
    for c in range(uv_ref.shape[0] // GMLP_CHUNK):
        rs = slice(c * GMLP_CHUNK, (c + 1) * GMLP_CHUNK)
        u = jax.nn.gelu(uv_ref[rs, :width])
        vn = _rms(jax.nn.gelu(uv_ref[rs, width:]), gain_ref[...])
        vn_ref[rs, :] = vn
        vnb = vn.astype(ws_ref.dtype)
        for g in range(GMLP_GROUPS):
            cs = slice(g * gdim, (g + 1) * gdim)
            s = jnp.dot(ws_ref[g], vnb[:, cs], preferred_element_type=jnp.float32) + b_ref[:, cs]
            o_ref[rs, cs] = (u[:, cs] * s).astype(o_ref.dtype)


def _gmlp(uv, gain, ws2, b2, n_prompt_rows):
    rows = uv.shape[0]
    width = uv.shape[1] // 2
    tile = 2 * GMLP_CHUNK
    n_prompt_tiles = n_prompt_rows // tile

    def sel(i):
        return jnp.where(i >= n_prompt_tiles, 1, 0)

    return pl.pallas_call(
        _gmlp_kernel,
        out_shape=(jax.ShapeDtypeStruct((rows, width), _MXU_DTYPE),
                   jax.ShapeDtypeStruct((rows, width), jnp.float32)),
        grid=(rows // tile,),
        in_specs=[
            pl.BlockSpec((tile, 2 * width), lambda i: (i, 0)),
            pl.BlockSpec((1, width), lambda i: (0, 0)),
            pl.BlockSpec((None, GMLP_GROUPS, GMLP_CHUNK, GMLP_CHUNK), lambda i: (sel(i), 0, 0, 0)),
            pl.BlockSpec((None, GMLP_CHUNK, width), lambda i: (sel(i), 0, 0)),
        ],
        out_specs=(pl.BlockSpec((tile, width), lambda i: (i, 0)),
                   pl.BlockSpec((tile, width), lambda i: (i, 0))),
        compiler_params=_params("parallel"),
        name="gmlp",
    )(uv, gain, ws2, b2)


def _mixout_kernel(h_ref, a_ref, m_ref, w_ref, g_ref, o_ref):
    y = jnp.dot(a_ref[...], w_ref[0], preferred_element_type=jnp.float32)
    y = y + jnp.dot(m_ref[...], w_ref[1], preferred_element_type=jnp.float32)
    o_ref[...] = h_ref[...] + _rms(y, g_ref[...])


def _mixout(h, attn, gm, w_out2, g, layer):
    rows, d = h.shape
    half = attn.shape[1]
    return pl.pallas_call(
        _mixout_kernel,
        out_shape=jax.ShapeDtypeStruct((rows, d), jnp.float32),
        grid=(rows // ROW_TILE,),
        in_specs=[
            pl.BlockSpec((ROW_TILE, d), lambda i: (i, 0)),
            pl.BlockSpec((ROW_TILE, half), lambda i: (i, 0)),
            pl.BlockSpec((ROW_TILE, half), lambda i: (i, 0)),
            pl.BlockSpec((None, 2, half, d), lambda i: (layer, 0, 0, 0)),
            pl.BlockSpec((1, d), lambda i: (0, 0)),
        ],
        out_specs=pl.BlockSpec((ROW_TILE, d), lambda i: (i, 0)),
        compiler_params=_params("parallel"),
        name="mixout",
    )(h, attn, gm, w_out2, g)


def _ple_kernel(h_ref, p_ref, gpre_ref, gpost_ref, wpg_ref, wpe_ref, o_ref):
    h = h_ref[...]
    xn = _rms(h, gpre_ref[...]).astype(wpg_ref.dtype)
    gate = jax.nn.sigmoid(jnp.dot(xn, wpg_ref[...], preferred_element_type=jnp.float32))
    emb = jnp.dot(p_ref[...].astype(wpe_ref.dtype), wpe_ref[...], preferred_element_type=jnp.float32)
    o_ref[...] = h + _rms(gate * emb, gpost_ref[...])


def _ple(h, p, g_pre, g_post, w_pg, w_pe, layer):
    rows, d = h.shape
    d_ple = p.shape[1]
    return pl.pallas_call(
        _ple_kernel,
        out_shape=jax.ShapeDtypeStruct((rows, d), jnp.float32),
        grid=(rows // ROW_TILE,),
        in_specs=[
            pl.BlockSpec((ROW_TILE, d), lambda i: (i, 0)),
            pl.BlockSpec((ROW_TILE, d_ple), lambda i: (i, 0)),
            pl.BlockSpec((1, d), lambda i: (0, 0)),
            pl.BlockSpec((1, d), lambda i: (0, 0)),
            pl.BlockSpec((None, d, d), lambda i: (layer, 0, 0)),
            pl.BlockSpec((None, d_ple, d), lambda i: (layer, 0, 0)),
        ],
        out_specs=pl.BlockSpec((ROW_TILE, d), lambda i: (i, 0)),
        compiler_params=_params("parallel"),
        name="ple",
    )(h, p, g_pre, g_post, w_pg, w_pe)


def _attn_kernel(q_ref, qi_ref, w_ref, k_ref, vt_ref, ki_ref, bias_ref, o_ref,
                 keys_ref, m_ref, l_ref, acc_ref, s_ref, p_ref, *, t0_base, t0_step, topk, idx_scale):
    f32 = jnp.float32
    tq = Q_TILE
    heads_per_kv = N_HEADS // N_KV_HEADS
    nt = (((1,), (1,)), ((), ()))
    t0 = t0_base + pl.program_id(1) * t0_step
    w0 = jnp.maximum(t0 - MAX_DISTANCE, 0)

    def loop(n, fn):
        lax.fori_loop(0, n, lambda j, c: (fn(j), c)[1], 0)

    def reduce_keys(x, op):
        x = op(x.reshape(-1, SUBLANES * SUBLANES, tq), axis=0)
        x = op(x.reshape(SUBLANES, SUBLANES, tq), axis=0)
        return op(x, axis=0, keepdims=True)

    def score_rows(r0, near):
        r0 = pl.multiple_of(r0, MAX_DISTANCE)
        qs = qi_ref[...].reshape(H_IDX * tq, LANES)
        d = lax.dot_general(ki_ref[pl.ds(r0, SCORE_TILE), :], qs, nt, preferred_element_type=f32)
        score = jnp.zeros((SCORE_TILE, tq), f32)
        for h in range(H_IDX):
            score = score + (w_ref[h:h + 1, :] * idx_scale) * jnp.maximum(d[:, h * tq:(h + 1) * tq], 0.0)
        bits = lax.bitcast_convert_type(score, jnp.int32)
        key = bits ^ ((bits >> 31) & 0x7FFFFFFF)
        if near:
            s_pos = r0 + lax.broadcasted_iota(jnp.int32, (SCORE_TILE, tq), 0)
            t_pos = t0 + lax.broadcasted_iota(jnp.int32, (SCORE_TILE, tq), 1)
            key = jnp.where((s_pos // CHUNK) <= (t_pos // CHUNK), key, _INT_MIN)
        keys_ref[pl.ds(r0, SCORE_TILE), :] = key

    def score_step(j):
        for i in range(SCORE_STEP):
            score_rows((SCORE_STEP * j + i) * SCORE_TILE, False)

    loop((w0 + SCORE_STEP * SCORE_TILE - 1) // (SCORE_STEP * SCORE_TILE), score_step)
    score_rows(w0, True)
    pad_rows = COUNT_TILE - MAX_DISTANCE
    pad = pl.ds(pl.multiple_of(w0 + NEAR_KEYS, MAX_DISTANCE), pad_rows)
    keys_ref[pad, :] = jnp.full((pad_rows, tq), _INT_MIN, jnp.int32)

    n_count = (w0 + NEAR_KEYS + COUNT_TILE - 1) // COUNT_TILE
    wide = SUBLANES * SUBLANES

    def count_ge(cand):
        cand = jnp.concatenate([cand] * (wide // SUBLANES), axis=0)

        def body(j, cnt):
            k = keys_ref[pl.ds(pl.multiple_of(j * COUNT_TILE, COUNT_TILE), COUNT_TILE), :]
            k = k.reshape(COUNT_TILE // wide, wide, tq)
            return cnt + jnp.sum(jnp.where(k >= cand[None], 1.0, 0.0), axis=0)

        cnt = lax.fori_loop(0, n_count, body, jnp.zeros((wide, tq), f32))
        return jnp.broadcast_to(reduce_keys(cnt, jnp.sum), (SUBLANES, tq))

    def search_step(step, state):
        thr_u, at_thr = state
        cand_u = thr_u | jnp.left_shift(jnp.int32(1), 31 - step)
        cnt = count_ge(cand_u ^ _INT_MIN)
        take = cnt >= topk
        return jnp.where(take, cand_u, thr_u), jnp.where(take, cnt, at_thr)

    n_admissible = count_ge(jnp.full((SUBLANES, tq), _INT_MIN + 1, jnp.int32))
    blind_steps = 16
    state = lax.fori_loop(0, blind_steps, search_step, (jnp.zeros((SUBLANES, tq), jnp.int32), n_admissible))

    steps_per_test = 4

    def unsettled(state):
        step, _, at_thr = state
        settled = jnp.where(jnp.logical_or(at_thr == topk, n_admissible <= topk), 1.0, 0.0)
        return jnp.logical_and(step < 32, jnp.sum(settled) < float(SUBLANES * tq))

    def tested_steps(state):
        step, thr_u, at_thr = state
        for i in range(steps_per_test):
            thr_u, at_thr = search_step(step + i, (thr_u, at_thr))
        return step + steps_per_test, thr_u, at_thr

    _, thr_u, _ = lax.while_loop(unsettled, tested_steps, (jnp.int32(blind_steps),) + state)
    thr = jnp.maximum(thr_u ^ _INT_MIN, _INT_MIN + 1)[0:1]

    m_ref[...] = jnp.full(m_ref.shape, -1e30, f32)
    l_ref[...] = jnp.zeros(l_ref.shape, f32)
    acc_ref[...] = jnp.zeros(acc_ref.shape, f32)

    def attend(r0, nk, near):
        r0 = pl.multiple_of(r0, MAX_DISTANCE)
        rows = pl.ds(r0, nk)
        selected = keys_ref[rows, :] >= thr
        mask_bias = jnp.where(selected, 0.0, -jnp.inf)
        if not near:
            pos = r0 + lax.broadcasted_iota(jnp.int32, (nk, tq), 0)
            mask_bias = jnp.where(pos < w0, mask_bias, -jnp.inf)
        for g in range(N_KV_HEADS):
            cs = slice(g * HEAD_DIM, (g + 1) * HEAD_DIM)
            qg = q_ref[g * heads_per_kv:(g + 1) * heads_per_kv].reshape(heads_per_kv * tq, HEAD_DIM)
            s_all = lax.dot_general(k_ref[rows, cs], qg, nt, preferred_element_type=f32)
            for r in range(heads_per_kv):
                h = g * heads_per_kv + r
                s = s_all[:, r * tq:(r + 1) * tq] + mask_bias
                if near:
                    s = s + bias_ref[jnp.where(t0 == w0, 1, 0), h]
                s_ref[h, 0:nk, :] = s
        for g in range(N_KV_HEADS):
            alphas = []
            for r in range(heads_per_kv):
                h = g * heads_per_kv + r
                m_old = m_ref[h]
                m_new = jnp.maximum(m_old, reduce_keys(s_ref[h, 0:nk, :], jnp.max))
                alpha = jnp.exp2(m_old - m_new)
                p = jnp.exp2(s_ref[h, 0:nk, :] - m_new[0:1])
                m_ref[h] = m_new
                p_ref[g, 0:nk, r * tq:(r + 1) * tq] = p.astype(p_ref.dtype)
                alphas.append(alpha)
            pv = jnp.dot(vt_ref[g, :, rows], p_ref[g, 0:nk, :], preferred_element_type=f32)
            for r in range(heads_per_kv):
                h = g * heads_per_kv + r
                qs_ = slice(r * tq, (r + 1) * tq)
                acc_ref[h] = alphas[r][0:1] * acc_ref[h] + pv[0:HEAD_DIM, qs_]
                l_ref[h] = alphas[r] * l_ref[h] + pv[HEAD_DIM:HEAD_DIM + 1, qs_]

    loop((w0 + ATTN_TILE - 1) // ATTN_TILE, lambda j: attend(j * ATTN_TILE, ATTN_TILE, False))
    attend(w0, NEAR_KEYS, True)

    for h in range(N_HEADS):
        out_t = acc_ref[h] / l_ref[h][0:1]
        o_ref[:, h * HEAD_DIM:(h + 1) * HEAD_DIM] = out_t.T.astype(o_ref.dtype)


def _attention(q_hm, qi_hm, w_t, k_all, vt_all, ki_all, bias_tiles, *, n_batch, n_qblocks, t0_base, t0_step, topk):
    tq = Q_TILE
    n_keys = k_all.shape[1]
    kv_w = N_KV_HEADS * HEAD_DIM
    assert n_keys % COUNT_TILE == 0 and tq == MAX_DISTANCE and NEAR_KEYS == SCORE_TILE
    assert ATTN_TILE % COUNT_TILE == 0 and ATTN_TILE == SCORE_STEP * SCORE_TILE
    assert t0_base % MAX_DISTANCE == 0 and t0_step % MAX_DISTANCE == 0
    last_w0 = max(t0_base + (n_qblocks - 1) * t0_step - MAX_DISTANCE, 0)
    assert -(-last_w0 // ATTN_TILE) * ATTN_TILE <= n_keys and last_w0 + NEAR_KEYS <= n_keys
    kernel = functools.partial(_attn_kernel, t0_base=t0_base, t0_step=t0_step, topk=float(topk),
                               idx_scale=(H_IDX * D_IDX) ** -0.5)
    once = pl.Buffered(1)
    return pl.pallas_call(
        kernel,
        out_shape=jax.ShapeDtypeStruct((n_batch * n_qblocks * tq, N_HEADS * HEAD_DIM), _MXU_DTYPE),
        grid=(n_batch, n_qblocks),
        in_specs=[
            pl.BlockSpec((N_HEADS, tq, HEAD_DIM), lambda b, i: (0, b * n_qblocks + i, 0)),
            pl.BlockSpec((H_IDX, tq, LANES), lambda b, i: (0, b * n_qblocks + i, 0)),
            pl.BlockSpec((H_IDX, tq), lambda b, i: (0, b * n_qblocks + i)),
            pl.BlockSpec((None, n_keys, kv_w), lambda b, i: (b, 0, 0), pipeline_mode=once),
            pl.BlockSpec((None, N_KV_HEADS, V_ROWS, n_keys), lambda b, i: (b, 0, 0, 0), pipeline_mode=once),
            pl.BlockSpec((None, n_keys, LANES), lambda b, i: (b, 0, 0), pipeline_mode=once),
            pl.BlockSpec((2, N_HEADS, NEAR_KEYS, tq), lambda b, i: (0, 0, 0, 0), pipeline_mode=once),
        ],
        out_specs=pl.BlockSpec((tq, N_HEADS * HEAD_DIM), lambda b, i: (b * n_qblocks + i, 0)),
        scratch_shapes=[
            pltpu.VMEM((n_keys + COUNT_TILE, tq), jnp.int32),
            pltpu.VMEM((N_HEADS, SUBLANES, tq), jnp.float32),
            pltpu.VMEM((N_HEADS, SUBLANES, tq), jnp.float32),
            pltpu.VMEM((N_HEADS, HEAD_DIM, tq), jnp.float32),
            pltpu.VMEM((N_HEADS, ATTN_TILE, tq), jnp.float32),
            pltpu.VMEM((N_KV_HEADS, ATTN_TILE, (N_HEADS // N_KV_HEADS) * tq), _MXU_DTYPE),
        ],
        compiler_params=_params("parallel", "arbitrary"),
        name="dsa_attention",
    )(q_hm, qi_hm, w_t, k_all, vt_all, ki_all, bias_tiles)


def _t5_bucket(rel):
    nb = NUM_BUCKETS // 2
    max_exact = nb // 2
    n = jnp.abs(rel)
    nf = jnp.maximum(n, max_exact).astype(jnp.float32)
    large = max_exact + (jnp.log(nf / max_exact) / math.log(MAX_DISTANCE / max_exact) * (nb - max_exact)).astype(jnp.int32)
    large = jnp.minimum(large, nb - 1)
    return jnp.where(rel > 0, nb, 0) + jnp.where(n < max_exact, n, large)


def _near_bias(rel_bias):
    d = jnp.array([-MAX_DISTANCE, 0], jnp.int32)[:, None, None]
    rel = d + jnp.arange(NEAR_KEYS, dtype=jnp.int32)[None, :, None] - jnp.arange(Q_TILE, dtype=jnp.int32)[None, None, :]
    near = jnp.transpose(rel_bias[_t5_bucket(rel)], (0, 3, 1, 2))
    far = rel_bias[_t5_bucket(jnp.int32(-MAX_DISTANCE))]
    return ((near - far[None, :, None, None]) * LOG2_E).astype(jnp.float32)


def _values_t(v):
    n, keys, _ = v.shape
    vt = jnp.transpose(v.reshape(n, keys, N_KV_HEADS, HEAD_DIM), (0, 2, 3, 1))
    ones = jnp.ones((n, N_KV_HEADS, 1, keys), v.dtype)
    zeros = jnp.zeros((n, N_KV_HEADS, V_ROWS - HEAD_DIM - 1, keys), v.dtype)
    return jnp.concatenate([vt, ones, zeros], axis=2)


def _pad_cols(w, block, width):
    lead = w.shape[:-1]
    n = w.shape[-1] // block
    w = w.reshape(*lead, n, block)
    w = jnp.pad(w, [(0, 0)] * len(lead) + [(0, 0), (0, width - block)])
    return w.reshape(*lead, n * width)


def kernel(x_prompt, x_sample, p_prompt, p_sample, cache_k, cache_v, cache_kidx, norm_g, w_ffn1_gu, w_ffn1_down,
           w_in, w_out, gmlp_ws, gmlp_b, gmlp_vnorm, w_ffn2_gu, w_ffn2_down, w_ple_gate, w_ple, rel_bias):
    depth = w_in.shape[0]
    n_b, seq, d_model = x_prompt.shape
    dec_b, dec_seq, _ = x_sample.shape
    past = cache_k.shape[2]
    rows_p, rows_s = n_b * seq, dec_b * dec_seq
    attn_w = N_HEADS * HEAD_DIM
    kv_w = N_KV_HEADS * HEAD_DIM
    idx_w = H_IDX * D_IDX
    gmlp_w = d_model - attn_w
    mx = _MXU_DTYPE
    assert seq % ATTN_TILE == 0 and dec_seq <= CHUNK and past % MAX_DISTANCE == 0 and dec_seq % SUBLANES == 0
    assert past >= MAX_DISTANCE
    assert rows_p % (2 * GMLP_CHUNK) == 0 and (2 * GMLP_CHUNK) % dec_seq == 0 and rows_s % (2 * GMLP_CHUNK) == 0

    o1 = attn_w
    o2 = o1 + kv_w
    o3 = o2 + kv_w
    o4 = o3 + idx_w
    o6 = o4 + D_IDX + H_IDX
    w_q = w_in[:, :, :o1].astype(mx)
    w_qi = _pad_cols(w_in[:, :, o3:o4], D_IDX, LANES).astype(mx)
    w_kvki = w_in[:, :, o1:o3]
    w_kvki = jnp.concatenate([w_kvki, w_in[:, :, o4:o6]], axis=-1)
    w_kvki = jnp.pad(w_kvki, ((0, 0), (0, 0), (0, 2 * kv_w + LANES - w_kvki.shape[-1]))).astype(mx)
    w_uv = w_in[:, :, o6:].astype(mx)
    w1_gu, w1_dn = w_ffn1_gu.astype(mx), w_ffn1_down.astype(mx)
    w2_gu, w2_dn = w_ffn2_gu.astype(mx), w_ffn2_down.astype(mx)
    w_out2 = w_out.astype(mx).reshape(depth, 2, attn_w, d_model)
    w_pg, w_pe = w_ple_gate.astype(mx), w_ple.astype(mx)

    causal = jnp.tril(jnp.ones((GMLP_CHUNK, GMLP_CHUNK), bool))
    ws_full = jnp.where(causal, gmlp_ws, 0.0)
    reps = GMLP_CHUNK // dec_seq
    small = jnp.where(causal[:dec_seq, :dec_seq], gmlp_ws[:, :, :dec_seq, :dec_seq], 0.0)
    eye = jnp.eye(reps, dtype=gmlp_ws.dtype)
    ws_small = jnp.einsum('ab,lgij->lgaibj', eye, small).reshape(depth, GMLP_GROUPS, GMLP_CHUNK, GMLP_CHUNK)
    ws2 = jnp.stack([ws_full, ws_small], axis=1).astype(mx)
    b_small = jnp.tile(gmlp_b[:, :, :dec_seq], (1, 1, reps))
    b2 = jnp.stack([gmlp_b, b_small], axis=1)
    b2 = jnp.repeat(jnp.swapaxes(b2, 2, 3), gmlp_w // GMLP_GROUPS, axis=-1)

    near_bias = _near_bias(rel_bias)

    h = jnp.concatenate([x_prompt.reshape(rows_p, d_model), x_sample.reshape(rows_s, d_model)], axis=0)
    p_all = jnp.concatenate([p_prompt.reshape(depth, rows_p, -1), p_sample.reshape(depth, rows_s, -1)], axis=1)

    keys_s = past + dec_seq
    w0_s = past - MAX_DISTANCE
    keys_s_pad = max(-(-(w0_s + NEAR_KEYS) // COUNT_TILE) * COUNT_TILE, -(-w0_s // ATTN_TILE) * ATTN_TILE)
    assert keys_s_pad >= keys_s
    topk_p = min(TOPK_MAX, seq // 4)
    topk_s = min(TOPK_MAX, keys_s // 4)

    def sample_queries(x_hm):
        x = x_hm[:, rows_p:].reshape(x_hm.shape[0], dec_b, dec_seq, x_hm.shape[-1])
        x = jnp.pad(x, ((0, 0), (0, 0), (0, Q_TILE - dec_seq), (0, 0)))
        return x.reshape(x_hm.shape[0], dec_b * Q_TILE, x_hm.shape[-1])

    outs = [[] for _ in range(7)]
    for l in range(depth):
        g = norm_g[l][:, None, :]
        h = _ffn(h, g[0], g[1], w1_gu, w1_dn, l)

        q_hm = _proj(h, g[2], w_q[l], mx, True, "proj_q", out_scale=HEAD_DIM ** -0.5 * LOG2_E)
        qi_hm = _proj(h, g[2], w_qi[l], mx, True, "proj_qi")
        kvki = _proj(h, g[2], w_kvki[l], jnp.float32, False, "proj_kv")
        uv = _proj(h, g[2], w_uv[l], jnp.float32, False, "proj_uv")

        k_new, v_new, ki_new = kvki[:, :kv_w], kvki[:, kv_w:2 * kv_w], kvki[:, 2 * kv_w:2 * kv_w + D_IDX]
        w_t = kvki[:, 2 * kv_w + D_IDX:2 * kv_w + D_IDX + H_IDX].T

        def keys_p(x, width):
            x = x[:rows_p].reshape(n_b, seq, -1)
            return jnp.pad(x, ((0, 0), (0, 0), (0, width - x.shape[-1]))).astype(mx)

        attn_p = _attention(
            q_hm, qi_hm, w_t, keys_p(k_new, kv_w), _values_t(keys_p(v_new, kv_w)), keys_p(ki_new, LANES),
            near_bias, n_batch=n_b, n_qblocks=seq // Q_TILE, t0_base=0, t0_step=Q_TILE, topk=topk_p)

        def keys_s_all(cache, new, width):
            full = jnp.concatenate([cache.reshape(dec_b, past, -1), new[rows_p:].reshape(dec_b, dec_seq, -1)], axis=1)
            full = jnp.pad(full, ((0, 0), (0, keys_s_pad - keys_s), (0, width - full.shape[-1])))
            return full.astype(mx)

        w_t_s = jnp.pad(w_t[:, rows_p:].reshape(H_IDX, dec_b, dec_seq), ((0, 0), (0, 0), (0, Q_TILE - dec_seq)))
        attn_s = _attention(
            sample_queries(q_hm), sample_queries(qi_hm), w_t_s.reshape(H_IDX, dec_b * Q_TILE),
            keys_s_all(cache_k[l], k_new, kv_w), _values_t(keys_s_all(cache_v[l], v_new, kv_w)),
            keys_s_all(cache_kidx[l], ki_new, LANES),
            near_bias, n_batch=dec_b, n_qblocks=1, t0_base=past, t0_step=0, topk=topk_s)
        attn_s = attn_s.reshape(dec_b, Q_TILE, attn_w)[:, :dec_seq].reshape(rows_s, attn_w)
        attn = jnp.concatenate([attn_p, attn_s], axis=0)

        gm, vn = _gmlp(uv, gmlp_vnorm[l][None, :], ws2[l], b2[l], rows_p)
        h = _mixout(h, attn, gm, w_out2, g[3], l)
        h = _ffn(h, g[4], g[5], w2_gu, w2_dn, l)
        h = _ple(h, p_all[l], g[6], g[7], w_pg, w_pe, l)

        outs[0].append(k_new[:rows_p].reshape(n_b, seq, N_KV_HEADS, HEAD_DIM))
        outs[1].append(v_new[:rows_p].reshape(n_b, seq, N_KV_HEADS, HEAD_DIM))
        outs[2].append(ki_new[:rows_p].reshape(n_b, seq, D_IDX))
        outs[3].append(k_new[rows_p:].reshape(dec_b, dec_seq, N_KV_HEADS, HEAD_DIM))
        outs[4].append(v_new[rows_p:].reshape(dec_b, dec_seq, N_KV_HEADS, HEAD_DIM))
        outs[5].append(ki_new[rows_p:].reshape(dec_b, dec_seq, D_IDX))
        outs[6].append(vn[rows_p:].reshape(dec_b, dec_seq, gmlp_w))

    y_prompt = h[:rows_p].reshape(n_b, seq, d_model)
    y_sample = h[rows_p:].reshape(dec_b, dec_seq, d_model)
    return (y_prompt, y_sample) + tuple(jnp.stack(o) for o in outs)
```

```python
import functools
import math

import jax
import jax.numpy as jnp
from jax import lax
from jax.experimental import pallas as pl
from jax.experimental.pallas import tpu as pltpu

CHUNK = 64
N_HEADS = 8
N_KV_HEADS = 2
HEAD_DIM = 128
H_IDX = 16
D_IDX = 64
TOPK_MAX = 256
GMLP_CHUNK = 128
GMLP_GROUPS = 8
NUM_BUCKETS = 32
MAX_DISTANCE = 128
EPS = 1e-6
LOG2_E = 1.4426950408889634

LANES = 128
SUBLANES = 8
VMEM_LIMIT_BYTES = 56 * 1024 * 1024

ROW_TILE = 512
FF_TILE = 512
Q_TILE = 128
SCORE_TILE = 256
COUNT_TILE = 512
ATTN_TILE = 1024
SCORE_STEP = 4
NEAR_KEYS = 2 * MAX_DISTANCE
V_ROWS = HEAD_DIM + 16

_MXU_DTYPE = jnp.bfloat16
_INT_MIN = -(2 ** 31)


def _params(*semantics):
    return pltpu.CompilerParams(dimension_semantics=semantics, vmem_limit_bytes=VMEM_LIMIT_BYTES)


def _rms(x, g):
    return x * lax.rsqrt(jnp.mean(x * x, axis=-1, keepdims=True) + EPS) * g


def _ffn_kernel(h_ref, gpre_ref, gpost_ref, wg_ref, wu_ref, wd_ref, o_ref, xn_ref, acc_ref):
    j = pl.program_id(1)

    @pl.when(j == 0)
    def _():
        xn_ref[...] = _rms(h_ref[...], gpre_ref[...]).astype(xn_ref.dtype)
        acc_ref[...] = jnp.zeros_like(acc_ref)

    xn = xn_ref[...]
    gate = jnp.dot(xn, wg_ref[...], preferred_element_type=jnp.float32)
    up = jnp.dot(xn, wu_ref[...], preferred_element_type=jnp.float32)
    act = (jax.nn.silu(gate) * up).astype(wd_ref.dtype)
    acc_ref[...] += jnp.dot(act, wd_ref[...], preferred_element_type=jnp.float32)

    @pl.when(j == pl.num_programs(1) - 1)
    def _():
        o_ref[...] = h_ref[...] + 0.5 * _rms(acc_ref[...], gpost_ref[...])


def _ffn(h, g_pre, g_post, w_gu, w_down, layer):
    rows, d = h.shape
    d_ff = w_down.shape[1]
    nf = d_ff // FF_TILE
    assert rows % ROW_TILE == 0 and d_ff % FF_TILE == 0
    return pl.pallas_call(
        _ffn_kernel,
        out_shape=jax.ShapeDtypeStruct((rows, d), jnp.float32),
        grid=(rows // ROW_TILE, nf),
        in_specs=[
            pl.BlockSpec((ROW_TILE, d), lambda i, j: (i, 0)),
            pl.BlockSpec((1, d), lambda i, j: (0, 0)),
            pl.BlockSpec((1, d), lambda i, j: (0, 0)),
            pl.BlockSpec((None, d, FF_TILE), lambda i, j: (layer, 0, j)),
            pl.BlockSpec((None, d, FF_TILE), lambda i, j: (layer, 0, j + nf)),
            pl.BlockSpec((None, FF_TILE, d), lambda i, j: (layer, j, 0)),
        ],
        out_specs=pl.BlockSpec((ROW_TILE, d), lambda i, j: (i, 0)),
        scratch_shapes=[pltpu.VMEM((ROW_TILE, d), _MXU_DTYPE), pltpu.VMEM((ROW_TILE, d), jnp.float32)],
        compiler_params=_params("parallel", "arbitrary"),
        name="ffn",
    )(h, g_pre, g_post, w_gu, w_gu, w_down)


def _proj_kernel(h_ref, g_ref, w_ref, o_ref, *, head_major, out_scale):
    xn = _rms(h_ref[...], g_ref[...]).astype(w_ref.dtype)
    z = jnp.dot(xn, w_ref[...], preferred_element_type=jnp.float32)
    if out_scale != 1.0:
        z = z * out_scale
    if head_major:
        for hh in range(o_ref.shape[0]):
            o_ref[hh] = z[:, hh * LANES:(hh + 1) * LANES].astype(o_ref.dtype)
    else:
        o_ref[...] = z.astype(o_ref.dtype)


def _proj(h, g, w, out_dtype, head_major, name, out_scale=1.0):
    rows, d = h.shape
    n = w.shape[1]
    if head_major:
        out_shape = jax.ShapeDtypeStruct((n // LANES, rows, LANES), out_dtype)
        out_spec = pl.BlockSpec((n // LANES, ROW_TILE, LANES), lambda i: (0, i, 0))
    else:
        out_shape = jax.ShapeDtypeStruct((rows, n), out_dtype)
        out_spec = pl.BlockSpec((ROW_TILE, n), lambda i: (i, 0))
    return pl.pallas_call(
        functools.partial(_proj_kernel, head_major=head_major, out_scale=out_scale),
        out_shape=out_shape,
        grid=(rows // ROW_TILE,),
        in_specs=[
            pl.BlockSpec((ROW_TILE, d), lambda i: (i, 0)),
            pl.BlockSpec((1, d), lambda i: (0, 0)),
            pl.BlockSpec((d, n), lambda i: (0, 0)),
        ],
        out_specs=out_spec,
        compiler_params=_params("parallel"),
        name=name,
    )(h, g, w)


def _proj_kv_kernel(h_ref, g_ref, w_ref, kvki_ref, k_ref, ki_ref, vt_ref, wt_ref):
    kv_w = N_KV_HEADS * HEAD_DIM
    xn = _rms(h_ref[...], g_ref[...]).astype(w_ref.dtype)
    z = jnp.dot(xn, w_ref[...], preferred_element_type=jnp.float32)
    kvki_ref[...] = z
    k_ref[...] = z[:, :kv_w].astype(k_ref.dtype)
    tail = z[:, 2 * kv_w:]
    lane = lax.broadcasted_iota(jnp.int32, tail.shape, 1)
    ki_ref[...] = jnp.where(lane < D_IDX, tail, 0.0).astype(ki_ref.dtype)
    wt_ref[...] = tail.T[D_IDX:D_IDX + H_IDX, :]
    row = lax.broadcasted_iota(jnp.int32, (V_ROWS - HEAD_DIM, z.shape[0]), 0)
    for g in range(N_KV_HEADS):
        v_g = z[:, kv_w + g * HEAD_DIM:kv_w + (g + 1) * HEAD_DIM]
        vt_ref[g, 0:HEAD_DIM, :] = v_g.T.astype(vt_ref.dtype)
        vt_ref[g, HEAD_DIM:V_ROWS, :] = jnp.where(row == 0, 1.0, 0.0).astype(vt_ref.dtype)


def _proj_kv(h, g, w):
    rows, d = h.shape
    n = w.shape[1]
    kv_w = N_KV_HEADS * HEAD_DIM
    assert n == 2 * kv_w + LANES
    return pl.pallas_call(
        _proj_kv_kernel,
        out_shape=(jax.ShapeDtypeStruct((rows, n), jnp.float32),
                   jax.ShapeDtypeStruct((rows, kv_w), _MXU_DTYPE),
                   jax.ShapeDtypeStruct((rows, LANES), _MXU_DTYPE),
                   jax.ShapeDtypeStruct((N_KV_HEADS, V_ROWS, rows), _MXU_DTYPE),
                   jax.ShapeDtypeStruct((H_IDX, rows), jnp.float32)),
        grid=(rows // ROW_TILE,),
        in_specs=[
            pl.BlockSpec((ROW_TILE, d), lambda i: (i, 0)),
            pl.BlockSpec((1, d), lambda i: (0, 0)),
            pl.BlockSpec((d, n), lambda i: (0, 0)),
        ],
        out_specs=(pl.BlockSpec((ROW_TILE, n), lambda i: (i, 0)),
                   pl.BlockSpec((ROW_TILE, kv_w), lambda i: (i, 0)),
                   pl.BlockSpec((ROW_TILE, LANES), lambda i: (i, 0)),
                   pl.BlockSpec((N_KV_HEADS, V_ROWS, ROW_TILE), lambda i: (0, 0, i)),
                   pl.BlockSpec((H_IDX, ROW_TILE), lambda i: (0, i))),
        compiler_params=_params("parallel"),
        name="proj_kv",
    )(h, g, w)


def _gmlp_kernel(uv_ref, gain_ref, ws_ref, b_ref, o_ref, vn_ref):
    width = o_ref.shape[1]
    gdim = width // GMLP_GROUPS
    for c in range(uv_ref.shape[0] // GMLP_CHUNK):
        rs = slice(c * GMLP_CHUNK, (c + 1) * GMLP_CHUNK)
        u = jax.nn.gelu(uv_ref[rs, :width])
        vn = _rms(jax.nn.gelu(uv_ref[rs, width:]), gain_ref[...])
        vn_ref[rs, :] = vn
        vnb = vn.astype(ws_ref.dtype)
        for g in range(GMLP_GROUPS):
            cs = slice(g * gdim, (g + 1) * gdim)
            s = jnp.dot(ws_ref[g], vnb[:, cs], preferred_element_type=jnp.float32) + b_ref[:, cs]
            o_ref[rs, cs] = (u[:, cs] * s).astype(o_ref.dtype)


def _gmlp(uv, gain, ws2, b2, n_prompt_rows):
    rows = uv.shape[0]
    width = uv.shape[1] // 2
    tile = 2 * GMLP_CHUNK
    n_prompt_tiles = n_prompt_rows // tile

    def sel(i):
        return jnp.where(i >= n_prompt_tiles, 1, 0)

    return pl.pallas_call(
        _gmlp_kernel,
        out_shape=(jax.ShapeDtypeStruct((rows, width), _MXU_DTYPE),
                   jax.ShapeDtypeStruct((rows, width), jnp.float32)),
        grid=(rows // tile,),
        in_specs=[
            pl.BlockSpec((tile, 2 * width), lambda i: (i, 0)),
            pl.BlockSpec((1, width), lambda i: (0, 0)),
            pl.BlockSpec((None, GMLP_GROUPS, GMLP_CHUNK, GMLP_CHUNK), lambda i: (sel(i), 0, 0, 0)),
            pl.BlockSpec((None, GMLP_CHUNK, width), lambda i: (sel(i), 0, 0)),
        ],
        out_specs=(pl.BlockSpec((tile, width), lambda i: (i, 0)),
                   pl.BlockSpec((tile, width), lambda i: (i, 0))),
        compiler_params=_params("parallel"),
        name="gmlp",
    )(uv, gain, ws2, b2)


def _mixout_kernel(h_ref, a_ref, m_ref, w_ref, g_ref, o_ref):
    y = jnp.dot(a_ref[...], w_ref[0], preferred_element_type=jnp.float32)
    y = y + jnp.dot(m_ref[...], w_ref[1], preferred_element_type=jnp.float32)
    o_ref[...] = h_ref[...] + _rms(y, g_ref[...])


def _mixout(h, attn, gm, w_out2, g, layer):
    rows, d = h.shape
    half = attn.shape[1]
    return pl.pallas_call(
        _mixout_kernel,
        out_shape=jax.ShapeDtypeStruct((rows, d), jnp.float32),
        grid=(rows // ROW_TILE,),
        in_specs=[
            pl.BlockSpec((ROW_TILE, d), lambda i: (i, 0)),
            pl.BlockSpec((ROW_TILE, half), lambda i: (i, 0)),
            pl.BlockSpec((ROW_TILE, half), lambda i: (i, 0)),
            pl.BlockSpec((None, 2, half, d), lambda i: (layer, 0, 0, 0)),
            pl.BlockSpec((1, d), lambda i: (0, 0)),
        ],
        out_specs=pl.BlockSpec((ROW_TILE, d), lambda i: (i, 0)),
        compiler_params=_params("parallel"),
        name="mixout",
    )(h, attn, gm, w_out2, g)


def _ple_kernel(h_ref, p_ref, gpre_ref, gpost_ref, wpg_ref, wpe_ref, o_ref):
    h = h_ref[...]
    xn = _rms(h, gpre_ref[...]).astype(wpg_ref.dtype)
    gate = jax.nn.sigmoid(jnp.dot(xn, wpg_ref[...], preferred_element_type=jnp.float32))
    emb = jnp.dot(p_ref[...].astype(wpe_ref.dtype), wpe_ref[...], preferred_element_type=jnp.float32)
    o_ref[...] = h + _rms(gate * emb, gpost_ref[...])


def _ple(h, p, g_pre, g_post, w_pg, w_pe, layer):
    rows, d = h.shape
    d_ple = p.shape[1]
    return pl.pallas_call(
        _ple_kernel,
        out_shape=jax.ShapeDtypeStruct((rows, d), jnp.float32),
        grid=(rows // ROW_TILE,),
        in_specs=[
            pl.BlockSpec((ROW_TILE, d), lambda i: (i, 0)),
            pl.BlockSpec((ROW_TILE, d_ple), lambda i: (i, 0)),
            pl.BlockSpec((1, d), lambda i: (0, 0)),
            pl.BlockSpec((1, d), lambda i: (0, 0)),
            pl.BlockSpec((None, d, d), lambda i: (layer, 0, 0)),
            pl.BlockSpec((None, d_ple, d), lambda i: (layer, 0, 0)),
        ],
        out_specs=pl.BlockSpec((ROW_TILE, d), lambda i: (i, 0)),
        compiler_params=_params("parallel"),
        name="ple",
    )(h, p, g_pre, g_post, w_pg, w_pe)


def _attn_kernel(q_ref, qi_ref, w_ref, k_ref, vt_ref, ki_ref, bias_ref, o_ref,
                 keys_ref, m_ref, l_ref, acc_ref, s_ref, p_ref, *, t0_base, t0_step, topk, idx_scale, pos_bits):
    f32 = jnp.float32
    tq = Q_TILE
    heads_per_kv = N_HEADS // N_KV_HEADS
    nt = (((1,), (1,)), ((), ()))
    t0 = t0_base + pl.program_id(1) * t0_step
    w0 = jnp.maximum(t0 - MAX_DISTANCE, 0)

    def loop(n, fn):
        lax.fori_loop(0, n, lambda j, c: (fn(j), c)[1], 0)

    def reduce_keys(x, op):
        x = op(x.reshape(-1, SUBLANES * SUBLANES, tq), axis=0)
        x = op(x.reshape(SUBLANES, SUBLANES, tq), axis=0)
        return op(x, axis=0, keepdims=True)

    def score_rows(r0, near):
        r0 = pl.multiple_of(r0, MAX_DISTANCE)
        qs = qi_ref[...].reshape(H_IDX * tq, LANES)
        d = lax.dot_general(ki_ref[pl.ds(r0, SCORE_TILE), :], qs, nt, preferred_element_type=f32)
        score = jnp.zeros((SCORE_TILE, tq), f32)
        for h in range(H_IDX):
            score = score + (w_ref[h:h + 1, :] * idx_scale) * jnp.maximum(d[:, h * tq:(h + 1) * tq], 0.0)
        bits = lax.bitcast_convert_type(score, jnp.int32)
        key = bits ^ ((bits >> 31) & 0x7FFFFFFF)
        if near:
            s_pos = r0 + lax.broadcasted_iota(jnp.int32, (SCORE_TILE, tq), 0)
            t_pos = t0 + lax.broadcasted_iota(jnp.int32, (SCORE_TILE, tq), 1)
            key = jnp.where((s_pos // CHUNK) <= (t_pos // CHUNK), key, _INT_MIN)
        keys_ref[pl.ds(r0, SCORE_TILE), :] = key

    def score_step(j):
        for i in range(SCORE_STEP):
            score_rows((SCORE_STEP * j + i) * SCORE_TILE, False)

    loop((w0 + SCORE_STEP * SCORE_TILE - 1) // (SCORE_STEP * SCORE_TILE), score_step)
    score_rows(w0, True)
    pad_rows = COUNT_TILE - MAX_DISTANCE
    pad = pl.ds(pl.multiple_of(w0 + NEAR_KEYS, MAX_DISTANCE), pad_rows)
    keys_ref[pad, :] = jnp.full((pad_rows, tq), _INT_MIN, jnp.int32)

    n_count = (w0 + NEAR_KEYS + COUNT_TILE - 1) // COUNT_TILE
    wide = SUBLANES * SUBLANES

    def count_ge(cand):
        cand = jnp.concatenate([cand] * (wide // SUBLANES), axis=0)

        def body(j, cnt):
            k = keys_ref[pl.ds(pl.multiple_of(j * COUNT_TILE, COUNT_TILE), COUNT_TILE), :]
            k = k.reshape(COUNT_TILE // wide, wide, tq)
            return cnt + jnp.sum(jnp.where(k >= cand[None], 1.0, 0.0), axis=0)

        cnt = lax.fori_loop(0, n_count, body, jnp.zeros((wide, tq), f32))
        return jnp.broadcast_to(reduce_keys(cnt, jnp.sum), (SUBLANES, tq))

    def search_step(step, state):
        thr_u, at_thr = state
        cand_u = thr_u | jnp.left_shift(jnp.int32(1), 31 - step)
        cnt = count_ge(cand_u ^ _INT_MIN)
        take = cnt >= topk
        return jnp.where(take, cand_u, thr_u), jnp.where(take, cnt, at_thr)

    n_admissible = count_ge(jnp.full((SUBLANES, tq), _INT_MIN + 1, jnp.int32))
    blind_steps = 16
    state = lax.fori_loop(0, blind_steps, search_step, (jnp.zeros((SUBLANES, tq), jnp.int32), n_admissible))

    steps_per_test = 4

    def unsettled(state):
        step, _, at_thr = state
        settled = jnp.where(jnp.logical_or(at_thr == topk, n_admissible <= topk), 1.0, 0.0)
        return jnp.logical_and(step < 32, jnp.sum(settled) < float(SUBLANES * tq))

    def tested_steps(state):
        step, thr_u, at_thr = state
        for i in range(steps_per_test):
            thr_u, at_thr = search_step(step + i, (thr_u, at_thr))
        return step + steps_per_test, thr_u, at_thr

    _, thr_u, at_thr = lax.while_loop(unsettled, tested_steps, (jnp.int32(blind_steps),) + state)
    thr_s = thr_u ^ _INT_MIN
    thr = jnp.maximum(thr_s, _INT_MIN + 1)[0:1]

    tied = jnp.logical_and(at_thr > topk, n_admissible > topk)

    @pl.when(jnp.sum(jnp.where(tied, 1.0, 0.0)) > 0.0)
    def _break_ties():
        int_max = -(_INT_MIN + 1)
        is_max = thr_s == int_max
        above = jnp.where(is_max, 0.0, count_ge(jnp.where(is_max, thr_s, thr_s + 1)))
        need = topk - above
        thr_w = jnp.concatenate([thr_s] * (wide // SUBLANES), axis=0)

        def tile_hits(j, cut):
            rows = pl.ds(pl.multiple_of(j * COUNT_TILE, COUNT_TILE), COUNT_TILE)
            k = keys_ref[rows, :].reshape(COUNT_TILE // wide, wide, tq)
            pos = j * COUNT_TILE + lax.broadcasted_iota(jnp.int32, (COUNT_TILE, tq), 0)
            pos = pos.reshape(COUNT_TILE // wide, wide, tq)
            return rows, k, jnp.logical_and(k == thr_w[None], pos < cut[None])

        def tied_before(cut):
            cut = jnp.concatenate([cut] * (wide // SUBLANES), axis=0)

            def body(j, cnt):
                _, _, hit = tile_hits(j, cut)
                return cnt + jnp.sum(jnp.where(hit, 1.0, 0.0), axis=0)

            cnt = lax.fori_loop(0, n_count, body, jnp.zeros((wide, tq), f32))
            return jnp.broadcast_to(reduce_keys(cnt, jnp.sum), (SUBLANES, tq))

        def cut_step(step, last):
            cand = last | jnp.left_shift(jnp.int32(1), pos_bits - 1 - step)
            return jnp.where(tied_before(cand) < need, cand, last)

        last = lax.fori_loop(0, pos_bits, cut_step, jnp.zeros((SUBLANES, tq), jnp.int32))
        keep_before = jnp.where(tied, last + 1, int_max)
        keep_before = jnp.concatenate([keep_before] * (wide // SUBLANES), axis=0)

        def drop(j):
            rows, k, hit = tile_hits(j, keep_before)
            surplus = jnp.logical_and(k == thr_w[None], jnp.logical_not(hit))
            keys_ref[rows, :] = jnp.where(surplus, _INT_MIN, k).reshape(COUNT_TILE, tq)

        loop(n_count, drop)

    m_ref[...] = jnp.full(m_ref.shape, -1e30, f32)
    l_ref[...] = jnp.zeros(l_ref.shape, f32)
    acc_ref[...] = jnp.zeros(acc_ref.shape, f32)

    def attend(r0, nk, near):
        r0 = pl.multiple_of(r0, MAX_DISTANCE)
        rows = pl.ds(r0, nk)
        selected = keys_ref[rows, :] >= thr
        mask_bias = jnp.where(selected, 0.0, -jnp.inf)
        if not near:
            pos = r0 + lax.broadcasted_iota(jnp.int32, (nk, tq), 0)
            mask_bias = jnp.where(pos < w0, mask_bias, -jnp.inf)
        for g in range(N_KV_HEADS):
            cs = slice(g * HEAD_DIM, (g + 1) * HEAD_DIM)
            qg = q_ref[g * heads_per_kv:(g + 1) * heads_per_kv].reshape(heads_per_kv * tq, HEAD_DIM)
            s_all = lax.dot_general(k_ref[rows, cs], qg, nt, preferred_element_type=f32)
            for r in range(heads_per_kv):
                h = g * heads_per_kv + r
                s = s_all[:, r * tq:(r + 1) * tq] + mask_bias
                if near:
                    s = s + bias_ref[jnp.where(t0 == w0, 1, 0), h]
                s_ref[h, 0:nk, :] = s
        for g in range(N_KV_HEADS):
            alphas = []
            for r in range(heads_per_kv):
                h = g * heads_per_kv + r
                m_old = m_ref[h]
                m_new = jnp.maximum(m_old, reduce_keys(s_ref[h, 0:nk, :], jnp.max))
                alpha = jnp.exp2(m_old - m_new)
                p = jnp.exp2(s_ref[h, 0:nk, :] - m_new[0:1])
                m_ref[h] = m_new
                p_ref[g, 0:nk, r * tq:(r + 1) * tq] = p.astype(p_ref.dtype)
                alphas.append(alpha)
            pv = jnp.dot(vt_ref[g, :, rows], p_ref[g, 0:nk, :], preferred_element_type=f32)
            for r in range(heads_per_kv):
                h = g * heads_per_kv + r
                qs_ = slice(r * tq, (r + 1) * tq)
                acc_ref[h] = alphas[r][0:1] * acc_ref[h] + pv[0:HEAD_DIM, qs_]
                l_ref[h] = alphas[r] * l_ref[h] + pv[HEAD_DIM:HEAD_DIM + 1, qs_]

    loop((w0 + ATTN_TILE - 1) // ATTN_TILE, lambda j: attend(j * ATTN_TILE, ATTN_TILE, False))
    attend(w0, NEAR_KEYS, True)

    for h in range(N_HEADS):
        out_t = acc_ref[h] / l_ref[h][0:1]
        o_ref[:, h * HEAD_DIM:(h + 1) * HEAD_DIM] = out_t.T.astype(o_ref.dtype)


def _attention(q_hm, qi_hm, w_t, k_all, vt_all, ki_all, bias_tiles, *, n_batch, n_qblocks, n_keys, t0_base, t0_step,
               topk):
    tq = Q_TILE
    kv_w = N_KV_HEADS * HEAD_DIM
    once = pl.Buffered(1)
    if k_all.ndim == 3:
        key_specs = [
            pl.BlockSpec((None, n_keys, kv_w), lambda b, i: (b, 0, 0), pipeline_mode=once),
            pl.BlockSpec((None, N_KV_HEADS, V_ROWS, n_keys), lambda b, i: (b, 0, 0, 0), pipeline_mode=once),
            pl.BlockSpec((None, n_keys, LANES), lambda b, i: (b, 0, 0), pipeline_mode=once),
        ]
    else:
        key_specs = [
            pl.BlockSpec((n_keys, kv_w), lambda b, i: (b, 0), pipeline_mode=once),
            pl.BlockSpec((N_KV_HEADS, V_ROWS, n_keys), lambda b, i: (0, 0, b), pipeline_mode=once),
            pl.BlockSpec((n_keys, LANES), lambda b, i: (b, 0), pipeline_mode=once),
        ]
    assert n_keys % COUNT_TILE == 0 and tq == MAX_DISTANCE and NEAR_KEYS == SCORE_TILE
    assert ATTN_TILE % COUNT_TILE == 0 and ATTN_TILE == SCORE_STEP * SCORE_TILE
    assert t0_base % MAX_DISTANCE == 0 and t0_step % MAX_DISTANCE == 0
    last_w0 = max(t0_base + (n_qblocks - 1) * t0_step - MAX_DISTANCE, 0)
    assert -(-last_w0 // ATTN_TILE) * ATTN_TILE <= n_keys and last_w0 + NEAR_KEYS <= n_keys
    kernel = functools.partial(_attn_kernel, t0_base=t0_base, t0_step=t0_step, topk=float(topk),
                               idx_scale=(H_IDX * D_IDX) ** -0.5, pos_bits=(n_keys + COUNT_TILE).bit_length())
    return pl.pallas_call(
        kernel,
        out_shape=jax.ShapeDtypeStruct((n_batch * n_qblocks * tq, N_HEADS * HEAD_DIM), _MXU_DTYPE),
        grid=(n_batch, n_qblocks),
        in_specs=[
            pl.BlockSpec((N_HEADS, tq, HEAD_DIM), lambda b, i: (0, b * n_qblocks + i, 0)),
            pl.BlockSpec((H_IDX, tq, LANES), lambda b, i: (0, b * n_qblocks + i, 0)),
            pl.BlockSpec((H_IDX, tq), lambda b, i: (0, b * n_qblocks + i)),
            *key_specs,
            pl.BlockSpec((2, N_HEADS, NEAR_KEYS, tq), lambda b, i: (0, 0, 0, 0), pipeline_mode=once),
        ],
        out_specs=pl.BlockSpec((tq, N_HEADS * HEAD_DIM), lambda b, i: (b * n_qblocks + i, 0)),
        scratch_shapes=[
            pltpu.VMEM((n_keys + COUNT_TILE, tq), jnp.int32),
            pltpu.VMEM((N_HEADS, SUBLANES, tq), jnp.float32),
            pltpu.VMEM((N_HEADS, SUBLANES, tq), jnp.float32),
            pltpu.VMEM((N_HEADS, HEAD_DIM, tq), jnp.float32),
            pltpu.VMEM((N_HEADS, ATTN_TILE, tq), jnp.float32),
            pltpu.VMEM((N_KV_HEADS, ATTN_TILE, (N_HEADS // N_KV_HEADS) * tq), _MXU_DTYPE),
        ],
        compiler_params=_params("parallel", "arbitrary"),
        name="dsa_attention",
    )(q_hm, qi_hm, w_t, k_all, vt_all, ki_all, bias_tiles)


def _t5_bucket(rel):
    nb = NUM_BUCKETS // 2
    max_exact = nb // 2
    n = jnp.abs(rel)
    nf = jnp.maximum(n, max_exact).astype(jnp.float32)
    large = max_exact + (jnp.log(nf / max_exact) / math.log(MAX_DISTANCE / max_exact) * (nb - max_exact)).astype(jnp.int32)
    large = jnp.minimum(large, nb - 1)
    return jnp.where(rel > 0, nb, 0) + jnp.where(n < max_exact, n, large)


def _near_bias(rel_bias):
    d = jnp.array([-MAX_DISTANCE, 0], jnp.int32)[:, None, None]
    rel = d + jnp.arange(NEAR_KEYS, dtype=jnp.int32)[None, :, None] - jnp.arange(Q_TILE, dtype=jnp.int32)[None, None, :]
    hit = _t5_bucket(rel)[:, None, :, :, None] == jnp.arange(NUM_BUCKETS, dtype=jnp.int32)
    near = jnp.sum(jnp.where(hit, rel_bias.T[None, :, None, None, :], 0.0), axis=-1)
    far = rel_bias[_t5_bucket(jnp.int32(-MAX_DISTANCE))]
    return ((near - far[None, :, None, None]) * LOG2_E).astype(jnp.float32)


def _values_t(v):
    n, keys, _ = v.shape
    vt = jnp.transpose(v.reshape(n, keys, N_KV_HEADS, HEAD_DIM), (0, 2, 3, 1))
    ones = jnp.ones((n, N_KV_HEADS, 1, keys), v.dtype)
    zeros = jnp.zeros((n, N_KV_HEADS, V_ROWS - HEAD_DIM - 1, keys), v.dtype)
    return jnp.concatenate([vt, ones, zeros], axis=2)


def _pad_cols(w, block, width):
    lead = w.shape[:-1]
    n = w.shape[-1] // block
    w = w.reshape(*lead, n, block)
    w = jnp.pad(w, [(0, 0)] * len(lead) + [(0, 0), (0, width - block)])
    return w.reshape(*lead, n * width)


def kernel(x_prompt, x_sample, p_prompt, p_sample, cache_k, cache_v, cache_kidx, norm_g, w_ffn1_gu, w_ffn1_down,
           w_in, w_out, gmlp_ws, gmlp_b, gmlp_vnorm, w_ffn2_gu, w_ffn2_down, w_ple_gate, w_ple, rel_bias):
    depth = w_in.shape[0]
    n_b, seq, d_model = x_prompt.shape
    dec_b, dec_seq, _ = x_sample.shape
    past = cache_k.shape[2]
    rows_p, rows_s = n_b * seq, dec_b * dec_seq
    attn_w = N_HEADS * HEAD_DIM
    kv_w = N_KV_HEADS * HEAD_DIM
    idx_w = H_IDX * D_IDX
    gmlp_w = d_model - attn_w
    mx = _MXU_DTYPE
    assert seq % ATTN_TILE == 0 and dec_seq <= CHUNK and past % MAX_DISTANCE == 0 and dec_seq % SUBLANES == 0
    assert past >= MAX_DISTANCE
    assert rows_p % (2 * GMLP_CHUNK) == 0 and (2 * GMLP_CHUNK) % dec_seq == 0 and rows_s % (2 * GMLP_CHUNK) == 0

    o1 = attn_w
    o2 = o1 + kv_w
    o3 = o2 + kv_w
    o4 = o3 + idx_w
    o6 = o4 + D_IDX + H_IDX
    w_q = w_in[:, :, :o1].astype(mx)
    w_qi = _pad_cols(w_in[:, :, o3:o4], D_IDX, LANES).astype(mx)
    w_kvki = w_in[:, :, o1:o3]
    w_kvki = jnp.concatenate([w_kvki, w_in[:, :, o4:o6]], axis=-1)
    w_kvki = jnp.pad(w_kvki, ((0, 0), (0, 0), (0, 2 * kv_w + LANES - w_kvki.shape[-1]))).astype(mx)
    w_uv = w_in[:, :, o6:].astype(mx)
    w1_gu, w1_dn = w_ffn1_gu.astype(mx), w_ffn1_down.astype(mx)
    w2_gu, w2_dn = w_ffn2_gu.astype(mx), w_ffn2_down.astype(mx)
    w_out2 = w_out.astype(mx).reshape(depth, 2, attn_w, d_model)
    w_pg, w_pe = w_ple_gate.astype(mx), w_ple.astype(mx)

    causal = jnp.tril(jnp.ones((GMLP_CHUNK, GMLP_CHUNK), bool))
    ws_full = jnp.where(causal, gmlp_ws, 0.0)
    reps = GMLP_CHUNK // dec_seq
    small = jnp.where(causal[:dec_seq, :dec_seq], gmlp_ws[:, :, :dec_seq, :dec_seq], 0.0)
    eye = jnp.eye(reps, dtype=gmlp_ws.dtype)
    ws_small = jnp.einsum('ab,lgij->lgaibj', eye, small).reshape(depth, GMLP_GROUPS, GMLP_CHUNK, GMLP_CHUNK)
    ws2 = jnp.stack([ws_full, ws_small], axis=1).astype(mx)
    b_small = jnp.tile(gmlp_b[:, :, :dec_seq], (1, 1, reps))
    b2 = jnp.stack([gmlp_b, b_small], axis=1)
    b2 = jnp.repeat(jnp.swapaxes(b2, 2, 3), gmlp_w // GMLP_GROUPS, axis=-1)

    near_bias = _near_bias(rel_bias)

    h = jnp.concatenate([x_prompt.reshape(rows_p, d_model), x_sample.reshape(rows_s, d_model)], axis=0)
    p_all = jnp.concatenate([p_prompt.reshape(depth, rows_p, -1), p_sample.reshape(depth, rows_s, -1)], axis=1)

    keys_s = past + dec_seq
    w0_s = past - MAX_DISTANCE
    keys_s_pad = max(-(-(w0_s + NEAR_KEYS) // COUNT_TILE) * COUNT_TILE, -(-w0_s // ATTN_TILE) * ATTN_TILE)
    assert keys_s_pad >= keys_s
    topk_p = min(TOPK_MAX, seq // 4)
    topk_s = min(TOPK_MAX, keys_s // 4)

    def sample_queries(x_hm):
        x = x_hm[:, rows_p:].reshape(x_hm.shape[0], dec_b, dec_seq, x_hm.shape[-1])
        x = jnp.pad(x, ((0, 0), (0, 0), (0, Q_TILE - dec_seq), (0, 0)))
        return x.reshape(x_hm.shape[0], dec_b * Q_TILE, x_hm.shape[-1])

    outs = [[] for _ in range(7)]
    for l in range(depth):
        g = norm_g[l][:, None, :]
        h = _ffn(h, g[0], g[1], w1_gu, w1_dn, l)

        q_hm = _proj(h, g[2], w_q[l], mx, True, "proj_q", out_scale=HEAD_DIM ** -0.5 * LOG2_E)
        qi_hm = _proj(h, g[2], w_qi[l], mx, True, "proj_qi")
        kvki, k_mx, ki_mx, vt_mx, w_t = _proj_kv(h, g[2], w_kvki[l])
        uv = _proj(h, g[2], w_uv[l], jnp.float32, False, "proj_uv")

        k_new, v_new, ki_new = kvki[:, :kv_w], kvki[:, kv_w:2 * kv_w], kvki[:, 2 * kv_w:2 * kv_w + D_IDX]

        attn_p = _attention(
            q_hm, qi_hm, w_t, k_mx, vt_mx, ki_mx, near_bias,
            n_batch=n_b, n_qblocks=seq // Q_TILE, n_keys=seq, t0_base=0, t0_step=Q_TILE, topk=topk_p)

        def keys_s_all(cache, new, width):
            full = jnp.concatenate([cache.reshape(dec_b, past, -1), new[rows_p:].reshape(dec_b, dec_seq, -1)], axis=1)
            full = jnp.pad(full, ((0, 0), (0, keys_s_pad - keys_s), (0, width - full.shape[-1])))
            return full.astype(mx)

        w_t_s = jnp.pad(w_t[:, rows_p:].reshape(H_IDX, dec_b, dec_seq), ((0, 0), (0, 0), (0, Q_TILE - dec_seq)))
        attn_s = _attention(
            sample_queries(q_hm), sample_queries(qi_hm), w_t_s.reshape(H_IDX, dec_b * Q_TILE),
            keys_s_all(cache_k[l], k_new, kv_w), _values_t(keys_s_all(cache_v[l], v_new, kv_w)),
            keys_s_all(cache_kidx[l], ki_new, LANES),
            near_bias, n_batch=dec_b, n_qblocks=1, n_keys=keys_s_pad, t0_base=past, t0_step=0, topk=topk_s)
        attn_s = attn_s.reshape(dec_b, Q_TILE, attn_w)[:, :dec_seq].reshape(rows_s, attn_w)
        attn = jnp.concatenate([attn_p, attn_s], axis=0)

        gm, vn = _gmlp(uv, gmlp_vnorm[l][None, :], ws2[l], b2[l], rows_p)
        h = _mixout(h, attn, gm, w_out2, g[3], l)
        h = _ffn(h, g[4], g[5], w2_gu, w2_dn, l)
        h = _ple(h, p_all[l], g[6], g[7], w_pg, w_pe, l)

        outs[0].append(k_new[:rows_p].reshape(n_b, seq, N_KV_HEADS, HEAD_DIM))
        outs[1].append(v_new[:rows_p].reshape(n_b, seq, N_KV_HEADS, HEAD_DIM))
        outs[2].append(ki_new[:rows_p].reshape(n_b, seq, D_IDX))
        outs[3].append(k_new[rows_p:].reshape(dec_b, dec_seq, N_KV_HEADS, HEAD_DIM))
        outs[4].append(v_new[rows_p:].reshape(dec_b, dec_seq, N_KV_HEADS, HEAD_DIM))
        outs[5].append(ki_new[rows_p:].reshape(dec_b, dec_seq, D_IDX))
        outs[6].append(vn[rows_p:].reshape(dec_b, dec_seq, gmlp_w))

    y_prompt = h[:rows_p].reshape(n_b, seq, d_model)
    y_sample = h[rows_p:].reshape(dec_b, dec_seq, d_model)
    return (y_prompt, y_sample) + tuple(jnp.stack(o) for o in outs)
```

```python
import functools
import math

import jax
import jax.numpy as jnp
from jax import lax
from jax.experimental import pallas as pl
from jax.experimental.pallas import tpu as pltpu

CHUNK = 64
N_HEADS = 8
N_KV_HEADS = 2
HEAD_DIM = 128
H_IDX = 16
D_IDX = 64
TOPK_MAX = 256
GMLP_CHUNK = 128
GMLP_GROUPS = 8
NUM_BUCKETS = 32
MAX_DISTANCE = 128
EPS = 1e-6
LOG2_E = 1.4426950408889634

LANES = 128
SUBLANES = 8
VMEM_LIMIT_BYTES = 56 * 1024 * 1024

ROW_TILE = 768
GMLP_TILE = 512
FF_TILE = 512
Q_TILE = 128
SCORE_TILE = 256
COUNT_TILE = 512
ATTN_TILE = 1024
SCORE_STEP = 4
NEAR_KEYS = 2 * MAX_DISTANCE
V_ROWS = HEAD_DIM + 16

_MXU_DTYPE = jnp.bfloat16
_INT_MIN = -(2 ** 31)


def _params(*semantics):
    return pltpu.CompilerParams(dimension_semantics=semantics, vmem_limit_bytes=VMEM_LIMIT_BYTES)


def _rms(x, g):
    return x * lax.rsqrt(jnp.mean(x * x, axis=-1, keepdims=True) + EPS) * g


def _ffn_kernel(h_ref, gpre_ref, gpost_ref, wg_ref, wu_ref, wd_ref, o_ref, xn_ref, acc_ref):
    j = pl.program_id(1)

    @pl.when(j == 0)
    def _():
        xn_ref[...] = _rms(h_ref[...], gpre_ref[...]).astype(xn_ref.dtype)
        acc_ref[...] = jnp.zeros_like(acc_ref)

    xn = xn_ref[...]
    gate = jnp.dot(xn, wg_ref[...], preferred_element_type=jnp.float32)
    up = jnp.dot(xn, wu_ref[...], preferred_element_type=jnp.float32)
    act = (jax.nn.silu(gate) * up).astype(wd_ref.dtype)
    acc_ref[...] += jnp.dot(act, wd_ref[...], preferred_element_type=jnp.float32)

    @pl.when(j == pl.num_programs(1) - 1)
    def _():
        o_ref[...] = h_ref[...] + 0.5 * _rms(acc_ref[...], gpost_ref[...])


def _ffn(h, g_pre, g_post, w_gu, w_down, layer):
    rows, d = h.shape
    d_ff = w_down.shape[1]
    nf = d_ff // FF_TILE
    assert rows % ROW_TILE == 0 and d_ff % FF_TILE == 0
    return pl.pallas_call(
        _ffn_kernel,
        out_shape=jax.ShapeDtypeStruct((rows, d), jnp.float32),
        grid=(rows // ROW_TILE, nf),
        in_specs=[
            pl.BlockSpec((ROW_TILE, d), lambda i, j: (i, 0)),
            pl.BlockSpec((1, d), lambda i, j: (0, 0)),
            pl.BlockSpec((1, d), lambda i, j: (0, 0)),
            pl.BlockSpec((None, d, FF_TILE), lambda i, j: (layer, 0, j)),
            pl.BlockSpec((None, d, FF_TILE), lambda i, j: (layer, 0, j + nf)),
            pl.BlockSpec((None, FF_TILE, d), lambda i, j: (layer, j, 0)),
        ],
        out_specs=pl.BlockSpec((ROW_TILE, d), lambda i, j: (i, 0)),
        scratch_shapes=[pltpu.VMEM((ROW_TILE, d), _MXU_DTYPE), pltpu.VMEM((ROW_TILE, d), jnp.float32)],
        compiler_params=_params("parallel", "arbitrary"),
        name="ffn",
    )(h, g_pre, g_post, w_gu, w_gu, w_down)


def _proj_kernel(h_ref, g_ref, w_ref, o_ref, *, head_major, out_scale):
    xn = _rms(h_ref[...], g_ref[...]).astype(w_ref.dtype)
    z = jnp.dot(xn, w_ref[...], preferred_element_type=jnp.float32)
    if out_scale != 1.0:
        z = z * out_scale
    if head_major:
        for hh in range(o_ref.shape[0]):
            o_ref[hh] = z[:, hh * LANES:(hh + 1) * LANES].astype(o_ref.dtype)
    else:
        o_ref[...] = z.astype(o_ref.dtype)


def _proj(h, g, w, out_dtype, head_major, name, out_scale=1.0):
    rows, d = h.shape
    n = w.shape[1]
    if head_major:
        out_shape = jax.ShapeDtypeStruct((n // LANES, rows, LANES), out_dtype)
        out_spec = pl.BlockSpec((n // LANES, ROW_TILE, LANES), lambda i: (0, i, 0))
    else:
        out_shape = jax.ShapeDtypeStruct((rows, n), out_dtype)
        out_spec = pl.BlockSpec((ROW_TILE, n), lambda i: (i, 0))
    return pl.pallas_call(
        functools.partial(_proj_kernel, head_major=head_major, out_scale=out_scale),
        out_shape=out_shape,
        grid=(rows // ROW_TILE,),
        in_specs=[
            pl.BlockSpec((ROW_TILE, d), lambda i: (i, 0)),
            pl.BlockSpec((1, d), lambda i: (0, 0)),
            pl.BlockSpec((d, n), lambda i: (0, 0)),
        ],
        out_specs=out_spec,
        compiler_params=_params("parallel"),
        name=name,
    )(h, g, w)


def _proj_kv_kernel(h_ref, g_ref, w_ref, kvki_ref, k_ref, ki_ref, vt_ref, wt_ref):
    kv_w = N_KV_HEADS * HEAD_DIM
    xn = _rms(h_ref[...], g_ref[...]).astype(w_ref.dtype)
    z = jnp.dot(xn, w_ref[...], preferred_element_type=jnp.float32)
    kvki_ref[...] = z
    k_ref[...] = z[:, :kv_w].astype(k_ref.dtype)
    tail = z[:, 2 * kv_w:]
    lane = lax.broadcasted_iota(jnp.int32, tail.shape, 1)
    ki_ref[...] = jnp.where(lane < D_IDX, tail, 0.0).astype(ki_ref.dtype)
    wt_ref[...] = tail.T[D_IDX:D_IDX + H_IDX, :]
    row = lax.broadcasted_iota(jnp.int32, (V_ROWS - HEAD_DIM, z.shape[0]), 0)
    for g in range(N_KV_HEADS):
        v_g = z[:, kv_w + g * HEAD_DIM:kv_w + (g + 1) * HEAD_DIM]
        vt_ref[g, 0:HEAD_DIM, :] = v_g.T.astype(vt_ref.dtype)
        vt_ref[g, HEAD_DIM:V_ROWS, :] = jnp.where(row == 0, 1.0, 0.0).astype(vt_ref.dtype)


def _proj_kv(h, g, w):
    rows, d = h.shape
    n = w.shape[1]
    kv_w = N_KV_HEADS * HEAD_DIM
    assert n == 2 * kv_w + LANES
    return pl.pallas_call(
        _proj_kv_kernel,
        out_shape=(jax.ShapeDtypeStruct((rows, n), jnp.float32),
                   jax.ShapeDtypeStruct((rows, kv_w), _MXU_DTYPE),
                   jax.ShapeDtypeStruct((rows, LANES), _MXU_DTYPE),
                   jax.ShapeDtypeStruct((N_KV_HEADS, V_ROWS, rows), _MXU_DTYPE),
                   jax.ShapeDtypeStruct((H_IDX, rows), jnp.float32)),
        grid=(rows // ROW_TILE,),
        in_specs=[
            pl.BlockSpec((ROW_TILE, d), lambda i: (i, 0)),
            pl.BlockSpec((1, d), lambda i: (0, 0)),
            pl.BlockSpec((d, n), lambda i: (0, 0)),
        ],
        out_specs=(pl.BlockSpec((ROW_TILE, n), lambda i: (i, 0)),
                   pl.BlockSpec((ROW_TILE, kv_w), lambda i: (i, 0)),
                   pl.BlockSpec((ROW_TILE, LANES), lambda i: (i, 0)),
                   pl.BlockSpec((N_KV_HEADS, V_ROWS, ROW_TILE), lambda i: (0, 0, i)),
                   pl.BlockSpec((H_IDX, ROW_TILE), lambda i: (0, i))),
        compiler_params=_params("parallel"),
        name="proj_kv",
    )(h, g, w)


def _gmlp_kernel(uv_ref, gain_ref, ws_ref, b_ref, o_ref, vn_ref):
    width = o_ref.shape[1]
    gdim = width // GMLP_GROUPS
    for c in range(uv_ref.shape[0] // GMLP_CHUNK):
        rs = slice(c * GMLP_CHUNK, (c + 1) * GMLP_CHUNK)
        u = jax.nn.gelu(uv_ref[rs, :width])
        vn = _rms(jax.nn.gelu(uv_ref[rs, width:]), gain_ref[...])
        vn_ref[rs, :] = vn
        vnb = vn.astype(ws_ref.dtype)
        for g in range(GMLP_GROUPS):
            cs = slice(g * gdim, (g + 1) * gdim)
            s = jnp.dot(ws_ref[g], vnb[:, cs], preferred_element_type=jnp.float32) + b_ref[:, cs]
            o_ref[rs, cs] = (u[:, cs] * s).astype(o_ref.dtype)


def _gmlp(uv, gain, ws2, b2, n_prompt_rows):
    rows = uv.shape[0]
    width = uv.shape[1] // 2
    tile = GMLP_TILE
    n_prompt_tiles = n_prompt_rows // tile

    def sel(i):
        return jnp.where(i >= n_prompt_tiles, 1, 0)

    return pl.pallas_call(
        _gmlp_kernel,
        out_shape=(jax.ShapeDtypeStruct((rows, width), _MXU_DTYPE),
                   jax.ShapeDtypeStruct((rows, width), jnp.float32)),
        grid=(rows // tile,),
        in_specs=[
            pl.BlockSpec((tile, 2 * width), lambda i: (i, 0)),
            pl.BlockSpec((1, width), lambda i: (0, 0)),
            pl.BlockSpec((None, GMLP_GROUPS, GMLP_CHUNK, GMLP_CHUNK), lambda i: (sel(i), 0, 0, 0)),
            pl.BlockSpec((None, GMLP_CHUNK, width), lambda i: (sel(i), 0, 0)),
        ],
        out_specs=(pl.BlockSpec((tile, width), lambda i: (i, 0)),
                   pl.BlockSpec((tile, width), lambda i: (i, 0))),
        compiler_params=_params("parallel"),
        name="gmlp",
    )(uv, gain, ws2, b2)


def _mixout_kernel(h_ref, a_ref, m_ref, w_ref, g_ref, o_ref):
    y = jnp.dot(a_ref[...], w_ref[0], preferred_element_type=jnp.float32)
    y = y + jnp.dot(m_ref[...], w_ref[1], preferred_element_type=jnp.float32)
    o_ref[...] = h_ref[...] + _rms(y, g_ref[...])


def _mixout(h, attn, gm, w_out2, g, layer):
    rows, d = h.shape
    half = attn.shape[1]
    return pl.pallas_call(
        _mixout_kernel,
        out_shape=jax.ShapeDtypeStruct((rows, d), jnp.float32),
        grid=(rows // ROW_TILE,),
        in_specs=[
            pl.BlockSpec((ROW_TILE, d), lambda i: (i, 0)),
            pl.BlockSpec((ROW_TILE, half), lambda i: (i, 0)),
            pl.BlockSpec((ROW_TILE, half), lambda i: (i, 0)),
            pl.BlockSpec((None, 2, half, d), lambda i: (layer, 0, 0, 0)),
            pl.BlockSpec((1, d), lambda i: (0, 0)),
        ],
        out_specs=pl.BlockSpec((ROW_TILE, d), lambda i: (i, 0)),
        compiler_params=_params("parallel"),
        name="mixout",
    )(h, attn, gm, w_out2, g)


def _ple_kernel(h_ref, p_ref, gpre_ref, gpost_ref, wpg_ref, wpe_ref, o_ref):
    h = h_ref[...]
    xn = _rms(h, gpre_ref[...]).astype(wpg_ref.dtype)
    gate = jax.nn.sigmoid(jnp.dot(xn, wpg_ref[...], preferred_element_type=jnp.float32))
    emb = jnp.dot(p_ref[...].astype(wpe_ref.dtype), wpe_ref[...], preferred_element_type=jnp.float32)
    o_ref[...] = h + _rms(gate * emb, gpost_ref[...])


def _ple(h, p, g_pre, g_post, w_pg, w_pe, layer):
    rows, d = h.shape
    d_ple = p.shape[1]
    return pl.pallas_call(
        _ple_kernel,
        out_shape=jax.ShapeDtypeStruct((rows, d), jnp.float32),
        grid=(rows // ROW_TILE,),
        in_specs=[
            pl.BlockSpec((ROW_TILE, d), lambda i: (i, 0)),
            pl.BlockSpec((ROW_TILE, d_ple), lambda i: (i, 0)),
            pl.BlockSpec((1, d), lambda i: (0, 0)),
            pl.BlockSpec((1, d), lambda i: (0, 0)),
            pl.BlockSpec((None, d, d), lambda i: (layer, 0, 0)),
            pl.BlockSpec((None, d_ple, d), lambda i: (layer, 0, 0)),
        ],
        out_specs=pl.BlockSpec((ROW_TILE, d), lambda i: (i, 0)),
        compiler_params=_params("parallel"),
        name="ple",
    )(h, p, g_pre, g_post, w_pg, w_pe)


def _attn_kernel(q_ref, qi_ref, w_ref, k_ref, vt_ref, ki_ref, bias_ref, o_ref,
                 keys_ref, m_ref, l_ref, acc_ref, s_ref, p_ref, *, t0_base, t0_step, topk, idx_scale, pos_bits):
    f32 = jnp.float32
    tq = Q_TILE
    heads_per_kv = N_HEADS // N_KV_HEADS
    nt = (((1,), (1,)), ((), ()))
    t0 = t0_base + pl.program_id(1) * t0_step
    w0 = jnp.maximum(t0 - MAX_DISTANCE, 0)

    def loop(n, fn):
        lax.fori_loop(0, n, lambda j, c: (fn(j), c)[1], 0)

    def reduce_keys(x, op):
        x = op(x.reshape(-1, SUBLANES * SUBLANES, tq), axis=0)
        x = op(x.reshape(SUBLANES, SUBLANES, tq), axis=0)
        return op(x, axis=0, keepdims=True)

    def score_rows(r0, near):
        r0 = pl.multiple_of(r0, MAX_DISTANCE)
        qs = qi_ref[...].reshape(H_IDX * tq, LANES)
        d = lax.dot_general(ki_ref[pl.ds(r0, SCORE_TILE), :], qs, nt, preferred_element_type=f32)
        score = jnp.zeros((SCORE_TILE, tq), f32)
        for h in range(H_IDX):
            score = score + (w_ref[h:h + 1, :] * idx_scale) * jnp.maximum(d[:, h * tq:(h + 1) * tq], 0.0)
        bits = lax.bitcast_convert_type(score, jnp.int32)
        key = bits ^ ((bits >> 31) & 0x7FFFFFFF)
        if near:
            s_pos = r0 + lax.broadcasted_iota(jnp.int32, (SCORE_TILE, tq), 0)
            t_pos = t0 + lax.broadcasted_iota(jnp.int32, (SCORE_TILE, tq), 1)
            key = jnp.where((s_pos // CHUNK) <= (t_pos // CHUNK), key, _INT_MIN)
        keys_ref[pl.ds(r0, SCORE_TILE), :] = key

    def score_step(j):
        for i in range(SCORE_STEP):
            score_rows((SCORE_STEP * j + i) * SCORE_TILE, False)

    loop((w0 + SCORE_STEP * SCORE_TILE - 1) // (SCORE_STEP * SCORE_TILE), score_step)
    score_rows(w0, True)
    pad_rows = COUNT_TILE - MAX_DISTANCE
    pad = pl.ds(pl.multiple_of(w0 + NEAR_KEYS, MAX_DISTANCE), pad_rows)
    keys_ref[pad, :] = jnp.full((pad_rows, tq), _INT_MIN, jnp.int32)

    n_count = (w0 + NEAR_KEYS + COUNT_TILE - 1) // COUNT_TILE
    wide = SUBLANES * SUBLANES

    def count_ge(cand):
        cand = jnp.concatenate([cand] * (wide // SUBLANES), axis=0)

        def body(j, cnt):
            k = keys_ref[pl.ds(pl.multiple_of(j * COUNT_TILE, COUNT_TILE), COUNT_TILE), :]
            k = k.reshape(COUNT_TILE // wide, wide, tq)
            return cnt + jnp.sum(jnp.where(k >= cand[None], 1.0, 0.0), axis=0)

        cnt = lax.fori_loop(0, n_count, body, jnp.zeros((wide, tq), f32))
        return jnp.broadcast_to(reduce_keys(cnt, jnp.sum), (SUBLANES, tq))

    def search_step(step, state):
        thr_u, at_thr = state
        cand_u = thr_u | jnp.left_shift(jnp.int32(1), 31 - step)
        cnt = count_ge(cand_u ^ _INT_MIN)
        take = cnt >= topk
        return jnp.where(take, cand_u, thr_u), jnp.where(take, cnt, at_thr)

    n_admissible = count_ge(jnp.full((SUBLANES, tq), _INT_MIN + 1, jnp.int32))
    blind_steps = 20
    state = lax.fori_loop(0, blind_steps, search_step, (jnp.zeros((SUBLANES, tq), jnp.int32), n_admissible))

    steps_per_test = 4

    def unsettled(state):
        step, _, at_thr = state
        settled = jnp.where(jnp.logical_or(at_thr == topk, n_admissible <= topk), 1.0, 0.0)
        return jnp.logical_and(step < 32, jnp.sum(settled) < float(SUBLANES * tq))

    def tested_steps(state):
        step, thr_u, at_thr = state
        for i in range(steps_per_test):
            thr_u, at_thr = search_step(step + i, (thr_u, at_thr))
        return step + steps_per_test, thr_u, at_thr

    _, thr_u, at_thr = lax.while_loop(unsettled, tested_steps, (jnp.int32(blind_steps),) + state)
    thr_s = thr_u ^ _INT_MIN
    thr = jnp.maximum(thr_s, _INT_MIN + 1)[0:1]

    tied = jnp.logical_and(at_thr > topk, n_admissible > topk)

    @pl.when(jnp.sum(jnp.where(tied, 1.0, 0.0)) > 0.0)
    def _break_ties():
        int_max = -(_INT_MIN + 1)
        is_max = thr_s == int_max
        above = jnp.where(is_max, 0.0, count_ge(jnp.where(is_max, thr_s, thr_s + 1)))
        need = topk - above
        thr_w = jnp.concatenate([thr_s] * (wide // SUBLANES), axis=0)

        def tile_hits(j, cut):
            rows = pl.ds(pl.multiple_of(j * COUNT_TILE, COUNT_TILE), COUNT_TILE)
            k = keys_ref[rows, :].reshape(COUNT_TILE // wide, wide, tq)
            pos = j * COUNT_TILE + lax.broadcasted_iota(jnp.int32, (COUNT_TILE, tq), 0)
            pos = pos.reshape(COUNT_TILE // wide, wide, tq)
            return rows, k, jnp.logical_and(k == thr_w[None], pos < cut[None])

        def tied_before(cut):
            cut = jnp.concatenate([cut] * (wide // SUBLANES), axis=0)

            def body(j, cnt):
                _, _, hit = tile_hits(j, cut)
                return cnt + jnp.sum(jnp.where(hit, 1.0, 0.0), axis=0)

            cnt = lax.fori_loop(0, n_count, body, jnp.zeros((wide, tq), f32))
            return jnp.broadcast_to(reduce_keys(cnt, jnp.sum), (SUBLANES, tq))

        def cut_step(step, last):
            cand = last | jnp.left_shift(jnp.int32(1), pos_bits - 1 - step)
            return jnp.where(tied_before(cand) < need, cand, last)

        last = lax.fori_loop(0, pos_bits, cut_step, jnp.zeros((SUBLANES, tq), jnp.int32))
        keep_before = jnp.where(tied, last + 1, int_max)
        keep_before = jnp.concatenate([keep_before] * (wide // SUBLANES), axis=0)

        def drop(j):
            rows, k, hit = tile_hits(j, keep_before)
            surplus = jnp.logical_and(k == thr_w[None], jnp.logical_not(hit))
            keys_ref[rows, :] = jnp.where(surplus, _INT_MIN, k).reshape(COUNT_TILE, tq)

        loop(n_count, drop)

    m_ref[...] = jnp.full(m_ref.shape, -1e30, f32)
    l_ref[...] = jnp.zeros(l_ref.shape, f32)
    acc_ref[...] = jnp.zeros(acc_ref.shape, f32)

    def attend(r0, nk, near):
        r0 = pl.multiple_of(r0, MAX_DISTANCE)
        rows = pl.ds(r0, nk)
        selected = keys_ref[rows, :] >= thr
        mask_bias = jnp.where(selected, 0.0, -jnp.inf)
        if not near:
            pos = r0 + lax.broadcasted_iota(jnp.int32, (nk, tq), 0)
            mask_bias = jnp.where(pos < w0, mask_bias, -jnp.inf)
        for g in range(N_KV_HEADS):
            cs = slice(g * HEAD_DIM, (g + 1) * HEAD_DIM)
            qg = q_ref[g * heads_per_kv:(g + 1) * heads_per_kv].reshape(heads_per_kv * tq, HEAD_DIM)
            s_all = lax.dot_general(k_ref[rows, cs], qg, nt, preferred_element_type=f32)
            for r in range(heads_per_kv):
                h = g * heads_per_kv + r
                s = s_all[:, r * tq:(r + 1) * tq] + mask_bias
                if near:
                    s = s + bias_ref[jnp.where(t0 == w0, 1, 0), h]
                s_ref[h, 0:nk, :] = s
        for g in range(N_KV_HEADS):
            alphas = []
            for r in range(heads_per_kv):
                h = g * heads_per_kv + r
                m_old = m_ref[h]
                m_new = jnp.maximum(m_old, reduce_keys(s_ref[h, 0:nk, :], jnp.max))
                alpha = jnp.exp2(m_old - m_new)
                p = jnp.exp2(s_ref[h, 0:nk, :] - m_new[0:1])
                m_ref[h] = m_new
                p_ref[g, 0:nk, r * tq:(r + 1) * tq] = p.astype(p_ref.dtype)
                alphas.append(alpha)
            pv = jnp.dot(vt_ref[g, :, rows], p_ref[g, 0:nk, :], preferred_element_type=f32)
            for r in range(heads_per_kv):
                h = g * heads_per_kv + r
                qs_ = slice(r * tq, (r + 1) * tq)
                acc_ref[h] = alphas[r][0:1] * acc_ref[h] + pv[0:HEAD_DIM, qs_]
                l_ref[h] = alphas[r] * l_ref[h] + pv[HEAD_DIM:HEAD_DIM + 1, qs_]

    loop((w0 + ATTN_TILE - 1) // ATTN_TILE, lambda j: attend(j * ATTN_TILE, ATTN_TILE, False))
    attend(w0, NEAR_KEYS, True)

    for h in range(N_HEADS):
        out_t = acc_ref[h] / l_ref[h][0:1]
        o_ref[:, h * HEAD_DIM:(h + 1) * HEAD_DIM] = out_t.T.astype(o_ref.dtype)


def _attention(q_hm, qi_hm, w_t, k_all, vt_all, ki_all, bias_tiles, *, n_batch, n_qblocks, n_keys, t0_base, t0_step,
               topk):
    tq = Q_TILE
    kv_w = N_KV_HEADS * HEAD_DIM
    once = pl.Buffered(1)
    if k_all.ndim == 3:
        key_specs = [
            pl.BlockSpec((None, n_keys, kv_w), lambda b, i: (b, 0, 0), pipeline_mode=once),
            pl.BlockSpec((None, N_KV_HEADS, V_ROWS, n_keys), lambda b, i: (b, 0, 0, 0), pipeline_mode=once),
            pl.BlockSpec((None, n_keys, LANES), lambda b, i: (b, 0, 0), pipeline_mode=once),
        ]
    else:
        key_specs = [
            pl.BlockSpec((n_keys, kv_w), lambda b, i: (b, 0), pipeline_mode=once),
            pl.BlockSpec((N_KV_HEADS, V_ROWS, n_keys), lambda b, i: (0, 0, b), pipeline_mode=once),
            pl.BlockSpec((n_keys, LANES), lambda b, i: (b, 0), pipeline_mode=once),
        ]
    assert n_keys % COUNT_TILE == 0 and tq == MAX_DISTANCE and NEAR_KEYS == SCORE_TILE
    assert ATTN_TILE % COUNT_TILE == 0 and ATTN_TILE == SCORE_STEP * SCORE_TILE
    assert t0_base % MAX_DISTANCE == 0 and t0_step % MAX_DISTANCE == 0
    last_w0 = max(t0_base + (n_qblocks - 1) * t0_step - MAX_DISTANCE, 0)
    assert -(-last_w0 // ATTN_TILE) * ATTN_TILE <= n_keys and last_w0 + NEAR_KEYS <= n_keys
    kernel = functools.partial(_attn_kernel, t0_base=t0_base, t0_step=t0_step, topk=float(topk),
                               idx_scale=(H_IDX * D_IDX) ** -0.5, pos_bits=(n_keys + COUNT_TILE).bit_length())
    return pl.pallas_call(
        kernel,
        out_shape=jax.ShapeDtypeStruct((n_batch * n_qblocks * tq, N_HEADS * HEAD_DIM), _MXU_DTYPE),
        grid=(n_batch, n_qblocks),
        in_specs=[
            pl.BlockSpec((N_HEADS, tq, HEAD_DIM), lambda b, i: (0, b * n_qblocks + i, 0)),
            pl.BlockSpec((H_IDX, tq, LANES), lambda b, i: (0, b * n_qblocks + i, 0)),
            pl.BlockSpec((H_IDX, tq), lambda b, i: (0, b * n_qblocks + i)),
            *key_specs,
            pl.BlockSpec((2, N_HEADS, NEAR_KEYS, tq), lambda b, i: (0, 0, 0, 0), pipeline_mode=once),
        ],
        out_specs=pl.BlockSpec((tq, N_HEADS * HEAD_DIM), lambda b, i: (b * n_qblocks + i, 0)),
        scratch_shapes=[
            pltpu.VMEM((n_keys + COUNT_TILE, tq), jnp.int32),
            pltpu.VMEM((N_HEADS, SUBLANES, tq), jnp.float32),
            pltpu.VMEM((N_HEADS, SUBLANES, tq), jnp.float32),
            pltpu.VMEM((N_HEADS, HEAD_DIM, tq), jnp.float32),
            pltpu.VMEM((N_HEADS, ATTN_TILE, tq), jnp.float32),
            pltpu.VMEM((N_KV_HEADS, ATTN_TILE, (N_HEADS // N_KV_HEADS) * tq), _MXU_DTYPE),
        ],
        compiler_params=_params("parallel", "arbitrary"),
        name="dsa_attention",
    )(q_hm, qi_hm, w_t, k_all, vt_all, ki_all, bias_tiles)


def _t5_bucket(rel):
    nb = NUM_BUCKETS // 2
    max_exact = nb // 2
    n = jnp.abs(rel)
    nf = jnp.maximum(n, max_exact).astype(jnp.float32)
    large = max_exact + (jnp.log(nf / max_exact) / math.log(MAX_DISTANCE / max_exact) * (nb - max_exact)).astype(jnp.int32)
    large = jnp.minimum(large, nb - 1)
    return jnp.where(rel > 0, nb, 0) + jnp.where(n < max_exact, n, large)


def _near_bias(rel_bias):
    d = jnp.array([-MAX_DISTANCE, 0], jnp.int32)[:, None, None]
    rel = d + jnp.arange(NEAR_KEYS, dtype=jnp.int32)[None, :, None] - jnp.arange(Q_TILE, dtype=jnp.int32)[None, None, :]
    hit = _t5_bucket(rel)[:, None, :, :, None] == jnp.arange(NUM_BUCKETS, dtype=jnp.int32)
    near = jnp.sum(jnp.where(hit, rel_bias.T[None, :, None, None, :], 0.0), axis=-1)
    far = rel_bias[_t5_bucket(jnp.int32(-MAX_DISTANCE))]
    return ((near - far[None, :, None, None]) * LOG2_E).astype(jnp.float32)


def _values_t(v):
    n, keys, _ = v.shape
    vt = jnp.transpose(v.reshape(n, keys, N_KV_HEADS, HEAD_DIM), (0, 2, 3, 1))
    ones = jnp.ones((n, N_KV_HEADS, 1, keys), v.dtype)
    zeros = jnp.zeros((n, N_KV_HEADS, V_ROWS - HEAD_DIM - 1, keys), v.dtype)
    return jnp.concatenate([vt, ones, zeros], axis=2)


def _pad_cols(w, block, width):
    lead = w.shape[:-1]
    n = w.shape[-1] // block
    w = w.reshape(*lead, n, block)
    w = jnp.pad(w, [(0, 0)] * len(lead) + [(0, 0), (0, width - block)])
    return w.reshape(*lead, n * width)


def kernel(x_prompt, x_sample, p_prompt, p_sample, cache_k, cache_v, cache_kidx, norm_g, w_ffn1_gu, w_ffn1_down,
           w_in, w_out, gmlp_ws, gmlp_b, gmlp_vnorm, w_ffn2_gu, w_ffn2_down, w_ple_gate, w_ple, rel_bias):
    depth = w_in.shape[0]
    n_b, seq, d_model = x_prompt.shape
    dec_b, dec_seq, _ = x_sample.shape
    past = cache_k.shape[2]
    rows_p, rows_s = n_b * seq, dec_b * dec_seq
    attn_w = N_HEADS * HEAD_DIM
    kv_w = N_KV_HEADS * HEAD_DIM
    idx_w = H_IDX * D_IDX
    gmlp_w = d_model - attn_w
    mx = _MXU_DTYPE
    assert seq % ATTN_TILE == 0 and dec_seq <= CHUNK and past % MAX_DISTANCE == 0 and dec_seq % SUBLANES == 0
    assert past >= MAX_DISTANCE
    assert rows_p % GMLP_TILE == 0 and rows_s % GMLP_TILE == 0 and GMLP_CHUNK % dec_seq == 0
    assert GMLP_TILE % GMLP_CHUNK == 0 and (rows_p + rows_s) % ROW_TILE == 0

    o1 = attn_w
    o2 = o1 + kv_w
    o3 = o2 + kv_w
    o4 = o3 + idx_w
    o6 = o4 + D_IDX + H_IDX
    w_q = w_in[:, :, :o1].astype(mx)
    w_qi = _pad_cols(w_in[:, :, o3:o4], D_IDX, LANES).astype(mx)
    w_kvki = w_in[:, :, o1:o3]
    w_kvki = jnp.concatenate([w_kvki, w_in[:, :, o4:o6]], axis=-1)
    w_kvki = jnp.pad(w_kvki, ((0, 0), (0, 0), (0, 2 * kv_w + LANES - w_kvki.shape[-1]))).astype(mx)
    w_uv = w_in[:, :, o6:].astype(mx)
    w1_gu, w1_dn = w_ffn1_gu.astype(mx), w_ffn1_down.astype(mx)
    w2_gu, w2_dn = w_ffn2_gu.astype(mx), w_ffn2_down.astype(mx)
    w_out2 = w_out.astype(mx).reshape(depth, 2, attn_w, d_model)
    w_pg, w_pe = w_ple_gate.astype(mx), w_ple.astype(mx)

    causal = jnp.tril(jnp.ones((GMLP_CHUNK, GMLP_CHUNK), bool))
    ws_full = jnp.where(causal, gmlp_ws, 0.0)
    reps = GMLP_CHUNK // dec_seq
    small = jnp.where(causal[:dec_seq, :dec_seq], gmlp_ws[:, :, :dec_seq, :dec_seq], 0.0)
    eye = jnp.eye(reps, dtype=gmlp_ws.dtype)
    ws_small = jnp.einsum('ab,lgij->lgaibj', eye, small).reshape(depth, GMLP_GROUPS, GMLP_CHUNK, GMLP_CHUNK)
    ws2 = jnp.stack([ws_full, ws_small], axis=1).astype(mx)
    b_small = jnp.tile(gmlp_b[:, :, :dec_seq], (1, 1, reps))
    b2 = jnp.stack([gmlp_b, b_small], axis=1)
    b2 = jnp.repeat(jnp.swapaxes(b2, 2, 3), gmlp_w // GMLP_GROUPS, axis=-1)

    near_bias = _near_bias(rel_bias)

    h = jnp.concatenate([x_prompt.reshape(rows_p, d_model), x_sample.reshape(rows_s, d_model)], axis=0)
    p_all = jnp.concatenate([p_prompt.reshape(depth, rows_p, -1), p_sample.reshape(depth, rows_s, -1)], axis=1)

    keys_s = past + dec_seq
    w0_s = past - MAX_DISTANCE
    keys_s_pad = max(-(-(w0_s + NEAR_KEYS) // COUNT_TILE) * COUNT_TILE, -(-w0_s // ATTN_TILE) * ATTN_TILE)
    assert keys_s_pad >= keys_s
    topk_p = min(TOPK_MAX, seq // 4)
    topk_s = min(TOPK_MAX, keys_s // 4)

    def sample_queries(x_hm):
        x = x_hm[:, rows_p:].reshape(x_hm.shape[0], dec_b, dec_seq, x_hm.shape[-1])
        x = jnp.pad(x, ((0, 0), (0, 0), (0, Q_TILE - dec_seq), (0, 0)))
        return x.reshape(x_hm.shape[0], dec_b * Q_TILE, x_hm.shape[-1])

    outs = [[] for _ in range(7)]
    for l in range(depth):
        g = norm_g[l][:, None, :]
        h = _ffn(h, g[0], g[1], w1_gu, w1_dn, l)

        q_hm = _proj(h, g[2], w_q[l], mx, True, "proj_q", out_scale=HEAD_DIM ** -0.5 * LOG2_E)
        qi_hm = _proj(h, g[2], w_qi[l], mx, True, "proj_qi")
        kvki, k_mx, ki_mx, vt_mx, w_t = _proj_kv(h, g[2], w_kvki[l])
        uv = _proj(h, g[2], w_uv[l], jnp.float32, False, "proj_uv")

        k_new, v_new, ki_new = kvki[:, :kv_w], kvki[:, kv_w:2 * kv_w], kvki[:, 2 * kv_w:2 * kv_w + D_IDX]

        attn_p = _attention(
            q_hm, qi_hm, w_t, k_mx, vt_mx, ki_mx, near_bias,
            n_batch=n_b, n_qblocks=seq // Q_TILE, n_keys=seq, t0_base=0, t0_step=Q_TILE, topk=topk_p)

        def keys_s_all(cache, new, width):
            full = jnp.concatenate([cache.reshape(dec_b, past, -1), new[rows_p:].reshape(dec_b, dec_seq, -1)], axis=1)
            full = jnp.pad(full, ((0, 0), (0, keys_s_pad - keys_s), (0, width - full.shape[-1])))
            return full.astype(mx)

        w_t_s = jnp.pad(w_t[:, rows_p:].reshape(H_IDX, dec_b, dec_seq), ((0, 0), (0, 0), (0, Q_TILE - dec_seq)))
        attn_s = _attention(
            sample_queries(q_hm), sample_queries(qi_hm), w_t_s.reshape(H_IDX, dec_b * Q_TILE),
            keys_s_all(cache_k[l], k_new, kv_w), _values_t(keys_s_all(cache_v[l], v_new, kv_w)),
            keys_s_all(cache_kidx[l], ki_new, LANES),
            near_bias, n_batch=dec_b, n_qblocks=1, n_keys=keys_s_pad, t0_base=past, t0_step=0, topk=topk_s)
        attn_s = attn_s.reshape(dec_b, Q_TILE, attn_w)[:, :dec_seq].reshape(rows_s, attn_w)
        attn = jnp.concatenate([attn_p, attn_s], axis=0)

        gm, vn = _gmlp(uv, gmlp_vnorm[l][None, :], ws2[l], b2[l], rows_p)
        h = _mixout(h, attn, gm, w_out2, g[3], l)
        h = _ffn(h, g[4], g[5], w2_gu, w2_dn, l)
        h = _ple(h, p_all[l], g[6], g[7], w_pg, w_pe, l)

        outs[0].append(k_new[:rows_p].reshape(n_b, seq, N_KV_HEADS, HEAD_DIM))
        outs[1].append(v_new[:rows_p].reshape(n_b, seq, N_KV_HEADS, HEAD_DIM))
        outs[2].append(ki_new[:rows_p].reshape(n_b, seq, D_IDX))
        outs[3].append(k_new[rows_p:].reshape(dec_b, dec_seq, N_KV_HEADS, HEAD_DIM))
        outs[4].append(v_new[rows_p:].reshape(dec_b, dec_seq, N_KV_HEADS, HEAD_DIM))
        outs[5].append(ki_new[rows_p:].reshape(dec_b, dec_seq, D_IDX))
        outs[6].append(vn[rows_p:].reshape(dec_b, dec_seq, gmlp_w))

    y_prompt = h[:rows_p].reshape(n_b, seq, d_model)
    y_sample = h[rows_p:].reshape(dec_b, dec_seq, d_model)
    return (y_prompt, y_sample) + tuple(jnp.stack(o) for o in outs)
```

```python
import functools
import math

import jax
import jax.numpy as jnp
from jax import lax
from jax.experimental import pallas as pl
from jax.experimental.pallas import tpu as pltpu

CHUNK = 64
N_HEADS = 8
N_KV_HEADS = 2
HEAD_DIM = 128
H_IDX = 16
D_IDX = 64
TOPK_MAX = 256
GMLP_CHUNK = 128
GMLP_GROUPS = 8
NUM_BUCKETS = 32
MAX_DISTANCE = 128
EPS = 1e-6
LOG2_E = 1.4426950408889634

LANES = 128
SUBLANES = 8
VMEM_LIMIT_BYTES = 56 * 1024 * 1024

ROW_TILE = 768
GMLP_TILE = 512
FF_TILE = 512
Q_TILE = 128
SCORE_TILE = 256
COUNT_TILE = 512
ATTN_TILE = 1024
SCORE_STEP = 4
NEAR_KEYS = 2 * MAX_DISTANCE
V_ROWS = HEAD_DIM + 16

_MXU_DTYPE = jnp.bfloat16
_INT_MIN = -(2 ** 31)


def _params(*semantics):
    return pltpu.CompilerParams(dimension_semantics=semantics, vmem_limit_bytes=VMEM_LIMIT_BYTES)


def _rms(x, g):
    return x * lax.rsqrt(jnp.mean(x * x, axis=-1, keepdims=True) + EPS) * g


def _ffn_kernel(h_ref, gpre_ref, gpost_ref, wg_ref, wu_ref, wd_ref, o_ref, xn_ref, acc_ref):
    j = pl.program_id(1)

    @pl.when(j == 0)
    def _():
        xn_ref[...] = _rms(h_ref[...], gpre_ref[...]).astype(xn_ref.dtype)
        acc_ref[...] = jnp.zeros_like(acc_ref)

    xn = xn_ref[...]
    gate = jnp.dot(xn, wg_ref[...], preferred_element_type=jnp.float32)
    up = jnp.dot(xn, wu_ref[...], preferred_element_type=jnp.float32)
    act = (jax.nn.silu(gate) * up).astype(wd_ref.dtype)
    acc_ref[...] += jnp.dot(act, wd_ref[...], preferred_element_type=jnp.float32)

    @pl.when(j == pl.num_programs(1) - 1)
    def _():
        o_ref[...] = h_ref[...] + 0.5 * _rms(acc_ref[...], gpost_ref[...])


def _ffn(h, g_pre, g_post, w_gu, w_down, layer):
    rows, d = h.shape
    d_ff = w_down.shape[1]
    nf = d_ff // FF_TILE
    assert rows % ROW_TILE == 0 and d_ff % FF_TILE == 0
    return pl.pallas_call(
        _ffn_kernel,
        out_shape=jax.ShapeDtypeStruct((rows, d), jnp.float32),
        grid=(rows // ROW_TILE, nf),
        in_specs=[
            pl.BlockSpec((ROW_TILE, d), lambda i, j: (i, 0)),
            pl.BlockSpec((1, d), lambda i, j: (0, 0)),
            pl.BlockSpec((1, d), lambda i, j: (0, 0)),
            pl.BlockSpec((None, d, FF_TILE), lambda i, j: (layer, 0, j)),
            pl.BlockSpec((None, d, FF_TILE), lambda i, j: (layer, 0, j + nf)),
            pl.BlockSpec((None, FF_TILE, d), lambda i, j: (layer, j, 0)),
        ],
        out_specs=pl.BlockSpec((ROW_TILE, d), lambda i, j: (i, 0)),
        scratch_shapes=[pltpu.VMEM((ROW_TILE, d), _MXU_DTYPE), pltpu.VMEM((ROW_TILE, d), jnp.float32)],
        compiler_params=_params("parallel", "arbitrary"),
        name="ffn",
    )(h, g_pre, g_post, w_gu, w_gu, w_down)


def _proj_kernel(h_ref, g_ref, w_ref, o_ref, *, head_major, out_scale):
    xn = _rms(h_ref[...], g_ref[...]).astype(w_ref.dtype)
    z = jnp.dot(xn, w_ref[...], preferred_element_type=jnp.float32)
    if out_scale != 1.0:
        z = z * out_scale
    if head_major:
        for hh in range(o_ref.shape[0]):
            o_ref[hh] = z[:, hh * LANES:(hh + 1) * LANES].astype(o_ref.dtype)
    else:
        o_ref[...] = z.astype(o_ref.dtype)


def _proj(h, g, w, out_dtype, head_major, name, out_scale=1.0):
    rows, d = h.shape
    n = w.shape[1]
    if head_major:
        out_shape = jax.ShapeDtypeStruct((n // LANES, rows, LANES), out_dtype)
        out_spec = pl.BlockSpec((n // LANES, ROW_TILE, LANES), lambda i: (0, i, 0))
    else:
        out_shape = jax.ShapeDtypeStruct((rows, n), out_dtype)
        out_spec = pl.BlockSpec((ROW_TILE, n), lambda i: (i, 0))
    return pl.pallas_call(
        functools.partial(_proj_kernel, head_major=head_major, out_scale=out_scale),
        out_shape=out_shape,
        grid=(rows // ROW_TILE,),
        in_specs=[
            pl.BlockSpec((ROW_TILE, d), lambda i: (i, 0)),
            pl.BlockSpec((1, d), lambda i: (0, 0)),
            pl.BlockSpec((d, n), lambda i: (0, 0)),
        ],
        out_specs=out_spec,
        compiler_params=_params("parallel"),
        name=name,
    )(h, g, w)


def _proj_kv_kernel(h_ref, g_ref, w_ref, kvki_ref, k_ref, ki_ref, vt_ref, wt_ref):
    kv_w = N_KV_HEADS * HEAD_DIM
    xn = _rms(h_ref[...], g_ref[...]).astype(w_ref.dtype)
    z = jnp.dot(xn, w_ref[...], preferred_element_type=jnp.float32)
    kvki_ref[...] = z
    k_ref[...] = z[:, :kv_w].astype(k_ref.dtype)
    tail = z[:, 2 * kv_w:]
    lane = lax.broadcasted_iota(jnp.int32, tail.shape, 1)
    ki_ref[...] = jnp.where(lane < D_IDX, tail, 0.0).astype(ki_ref.dtype)
    wt_ref[...] = tail.T[D_IDX:D_IDX + H_IDX, :]
    row = lax.broadcasted_iota(jnp.int32, (V_ROWS - HEAD_DIM, z.shape[0]), 0)
    for g in range(N_KV_HEADS):
        v_g = z[:, kv_w + g * HEAD_DIM:kv_w + (g + 1) * HEAD_DIM]
        vt_ref[g, 0:HEAD_DIM, :] = v_g.T.astype(vt_ref.dtype)
        vt_ref[g, HEAD_DIM:V_ROWS, :] = jnp.where(row == 0, 1.0, 0.0).astype(vt_ref.dtype)


def _proj_kv(h, g, w):
    rows, d = h.shape
    n = w.shape[1]
    kv_w = N_KV_HEADS * HEAD_DIM
    assert n == 2 * kv_w + LANES
    return pl.pallas_call(
        _proj_kv_kernel,
        out_shape=(jax.ShapeDtypeStruct((rows, n), jnp.float32),
                   jax.ShapeDtypeStruct((rows, kv_w), _MXU_DTYPE),
                   jax.ShapeDtypeStruct((rows, LANES), _MXU_DTYPE),
                   jax.ShapeDtypeStruct((N_KV_HEADS, V_ROWS, rows), _MXU_DTYPE),
                   jax.ShapeDtypeStruct((H_IDX, rows), jnp.float32)),
        grid=(rows // ROW_TILE,),
        in_specs=[
            pl.BlockSpec((ROW_TILE, d), lambda i: (i, 0)),
            pl.BlockSpec((1, d), lambda i: (0, 0)),
            pl.BlockSpec((d, n), lambda i: (0, 0)),
        ],
        out_specs=(pl.BlockSpec((ROW_TILE, n), lambda i: (i, 0)),
                   pl.BlockSpec((ROW_TILE, kv_w), lambda i: (i, 0)),
                   pl.BlockSpec((ROW_TILE, LANES), lambda i: (i, 0)),
                   pl.BlockSpec((N_KV_HEADS, V_ROWS, ROW_TILE), lambda i: (0, 0, i)),
                   pl.BlockSpec((H_IDX, ROW_TILE), lambda i: (0, i))),
        compiler_params=_params("parallel"),
        name="proj_kv",
    )(h, g, w)


def _gmlp_kernel(uv_ref, gain_ref, ws_ref, b_ref, o_ref, vn_ref):
    width = o_ref.shape[1]
    gdim = width // GMLP_GROUPS
    for c in range(uv_ref.shape[0] // GMLP_CHUNK):
        rs = slice(c * GMLP_CHUNK, (c + 1) * GMLP_CHUNK)
        u = jax.nn.gelu(uv_ref[rs, :width])
        vn = _rms(jax.nn.gelu(uv_ref[rs, width:]), gain_ref[...])
        vn_ref[rs, :] = vn
        vnb = vn.astype(ws_ref.dtype)
        for g in range(GMLP_GROUPS):
            cs = slice(g * gdim, (g + 1) * gdim)
            s = jnp.dot(ws_ref[g], vnb[:, cs], preferred_element_type=jnp.float32) + b_ref[:, cs]
            o_ref[rs, cs] = (u[:, cs] * s).astype(o_ref.dtype)


def _gmlp(uv, gain, ws2, b2, n_prompt_rows):
    rows = uv.shape[0]
    width = uv.shape[1] // 2
    tile = GMLP_TILE
    n_prompt_tiles = n_prompt_rows // tile

    def sel(i):
        return jnp.where(i >= n_prompt_tiles, 1, 0)

    return pl.pallas_call(
        _gmlp_kernel,
        out_shape=(jax.ShapeDtypeStruct((rows, width), _MXU_DTYPE),
                   jax.ShapeDtypeStruct((rows, width), jnp.float32)),
        grid=(rows // tile,),
        in_specs=[
            pl.BlockSpec((tile, 2 * width), lambda i: (i, 0)),
            pl.BlockSpec((1, width), lambda i: (0, 0)),
            pl.BlockSpec((None, GMLP_GROUPS, GMLP_CHUNK, GMLP_CHUNK), lambda i: (sel(i), 0, 0, 0)),
            pl.BlockSpec((None, GMLP_CHUNK, width), lambda i: (sel(i), 0, 0)),
        ],
        out_specs=(pl.BlockSpec((tile, width), lambda i: (i, 0)),
                   pl.BlockSpec((tile, width), lambda i: (i, 0))),
        compiler_params=_params("parallel"),
        name="gmlp",
    )(uv, gain, ws2, b2)


def _mixout_kernel(h_ref, a_ref, m_ref, w_ref, g_ref, o_ref):
    y = jnp.dot(a_ref[...], w_ref[0], preferred_element_type=jnp.float32)
    y = y + jnp.dot(m_ref[...], w_ref[1], preferred_element_type=jnp.float32)
    o_ref[...] = h_ref[...] + _rms(y, g_ref[...])


def _mixout(h, attn, gm, w_out2, g, layer):
    rows, d = h.shape
    half = attn.shape[1]
    return pl.pallas_call(
        _mixout_kernel,
        out_shape=jax.ShapeDtypeStruct((rows, d), jnp.float32),
        grid=(rows // ROW_TILE,),
        in_specs=[
            pl.BlockSpec((ROW_TILE, d), lambda i: (i, 0)),
            pl.BlockSpec((ROW_TILE, half), lambda i: (i, 0)),
            pl.BlockSpec((ROW_TILE, half), lambda i: (i, 0)),
            pl.BlockSpec((None, 2, half, d), lambda i: (layer, 0, 0, 0)),
            pl.BlockSpec((1, d), lambda i: (0, 0)),
        ],
        out_specs=pl.BlockSpec((ROW_TILE, d), lambda i: (i, 0)),
        compiler_params=_params("parallel"),
        name="mixout",
    )(h, attn, gm, w_out2, g)


def _ple_kernel(h_ref, p_ref, gpre_ref, gpost_ref, wpg_ref, wpe_ref, o_ref):
    h = h_ref[...]
    xn = _rms(h, gpre_ref[...]).astype(wpg_ref.dtype)
    gate = jax.nn.sigmoid(jnp.dot(xn, wpg_ref[...], preferred_element_type=jnp.float32))
    emb = jnp.dot(p_ref[...].astype(wpe_ref.dtype), wpe_ref[...], preferred_element_type=jnp.float32)
    o_ref[...] = h + _rms(gate * emb, gpost_ref[...])


def _ple(h, p, g_pre, g_post, w_pg, w_pe, layer):
    rows, d = h.shape
    d_ple = p.shape[1]
    return pl.pallas_call(
        _ple_kernel,
        out_shape=jax.ShapeDtypeStruct((rows, d), jnp.float32),
        grid=(rows // ROW_TILE,),
        in_specs=[
            pl.BlockSpec((ROW_TILE, d), lambda i: (i, 0)),
            pl.BlockSpec((ROW_TILE, d_ple), lambda i: (i, 0)),
            pl.BlockSpec((1, d), lambda i: (0, 0)),
            pl.BlockSpec((1, d), lambda i: (0, 0)),
            pl.BlockSpec((None, d, d), lambda i: (layer, 0, 0)),
            pl.BlockSpec((None, d_ple, d), lambda i: (layer, 0, 0)),
        ],
        out_specs=pl.BlockSpec((ROW_TILE, d), lambda i: (i, 0)),
        compiler_params=_params("parallel"),
        name="ple",
    )(h, p, g_pre, g_post, w_pg, w_pe)


def _attn_kernel(q_ref, qi_ref, w_ref, k_ref, vt_ref, ki_ref, bias_ref, o_ref,
                 keys_ref, m_ref, l_ref, acc_ref, s_ref, p_ref, *, t0_base, t0_step, topk, idx_scale, pos_bits):
    f32 = jnp.float32
    tq = Q_TILE
    heads_per_kv = N_HEADS // N_KV_HEADS
    nt = (((1,), (1,)), ((), ()))
    t0 = t0_base + pl.program_id(1) * t0_step
    w0 = jnp.maximum(t0 - MAX_DISTANCE, 0)

    def loop(n, fn):
        lax.fori_loop(0, n, lambda j, c: (fn(j), c)[1], 0)

    def reduce_keys(x, op):
        x = op(x.reshape(-1, SUBLANES * SUBLANES, tq), axis=0)
        x = op(x.reshape(SUBLANES, SUBLANES, tq), axis=0)
        return op(x, axis=0, keepdims=True)

    def score_rows(r0, near):
        r0 = pl.multiple_of(r0, MAX_DISTANCE)
        qs = qi_ref[...].reshape(H_IDX * tq, LANES)
        d = lax.dot_general(ki_ref[pl.ds(r0, SCORE_TILE), :], qs, nt, preferred_element_type=f32)
        score = jnp.zeros((SCORE_TILE, tq), f32)
        for h in range(H_IDX):
            score = score + (w_ref[h:h + 1, :] * idx_scale) * jnp.maximum(d[:, h * tq:(h + 1) * tq], 0.0)
        bits = lax.bitcast_convert_type(score, jnp.int32)
        key = bits ^ ((bits >> 31) & 0x7FFFFFFF)
        if near:
            s_pos = r0 + lax.broadcasted_iota(jnp.int32, (SCORE_TILE, tq), 0)
            t_pos = t0 + lax.broadcasted_iota(jnp.int32, (SCORE_TILE, tq), 1)
            key = jnp.where((s_pos // CHUNK) <= (t_pos // CHUNK), key, _INT_MIN)
        keys_ref[pl.ds(r0, SCORE_TILE), :] = key

    def score_step(j):
        for i in range(SCORE_STEP):
            score_rows((SCORE_STEP * j + i) * SCORE_TILE, False)

    loop((w0 + SCORE_STEP * SCORE_TILE - 1) // (SCORE_STEP * SCORE_TILE), score_step)
    score_rows(w0, True)
    pad_rows = COUNT_TILE - MAX_DISTANCE
    pad = pl.ds(pl.multiple_of(w0 + NEAR_KEYS, MAX_DISTANCE), pad_rows)
    keys_ref[pad, :] = jnp.full((pad_rows, tq), _INT_MIN, jnp.int32)

    n_count = (w0 + NEAR_KEYS + COUNT_TILE - 1) // COUNT_TILE
    wide = SUBLANES * SUBLANES

    def count_ge(cand):
        cand = jnp.concatenate([cand] * (wide // SUBLANES), axis=0)

        def body(j, cnt):
            k = keys_ref[pl.ds(pl.multiple_of(j * COUNT_TILE, COUNT_TILE), COUNT_TILE), :]
            k = k.reshape(COUNT_TILE // wide, wide, tq)
            for i in range(COUNT_TILE // wide):
                cnt = jnp.where(k[i] >= cand, cnt + 1.0, cnt)
            return cnt

        cnt = lax.fori_loop(0, n_count, body, jnp.zeros((wide, tq), f32))
        return jnp.broadcast_to(reduce_keys(cnt, jnp.sum), (SUBLANES, tq))

    def search_step(step, state):
        thr_u, at_thr = state
        cand_u = thr_u | jnp.left_shift(jnp.int32(1), 31 - step)
        cnt = count_ge(cand_u ^ _INT_MIN)
        take = cnt >= topk
        return jnp.where(take, cand_u, thr_u), jnp.where(take, cnt, at_thr)

    n_admissible = count_ge(jnp.full((SUBLANES, tq), _INT_MIN + 1, jnp.int32))
    blind_steps = 20
    state = lax.fori_loop(0, blind_steps, search_step, (jnp.zeros((SUBLANES, tq), jnp.int32), n_admissible))

    steps_per_test = 4

    def unsettled(state):
        step, _, at_thr = state
        settled = jnp.where(jnp.logical_or(at_thr == topk, n_admissible <= topk), 1.0, 0.0)
        return jnp.logical_and(step < 32, jnp.sum(settled) < float(SUBLANES * tq))

    def tested_steps(state):
        step, thr_u, at_thr = state
        for i in range(steps_per_test):
            thr_u, at_thr = search_step(step + i, (thr_u, at_thr))
        return step + steps_per_test, thr_u, at_thr

    _, thr_u, at_thr = lax.while_loop(unsettled, tested_steps, (jnp.int32(blind_steps),) + state)
    thr_s = thr_u ^ _INT_MIN
    thr = jnp.maximum(thr_s, _INT_MIN + 1)[0:1]

    tied = jnp.logical_and(at_thr > topk, n_admissible > topk)

    @pl.when(jnp.sum(jnp.where(tied, 1.0, 0.0)) > 0.0)
    def _break_ties():
        int_max = -(_INT_MIN + 1)
        is_max = thr_s == int_max
        above = jnp.where(is_max, 0.0, count_ge(jnp.where(is_max, thr_s, thr_s + 1)))
        need = topk - above
        thr_w = jnp.concatenate([thr_s] * (wide // SUBLANES), axis=0)

        def tile_hits(j, cut):
            rows = pl.ds(pl.multiple_of(j * COUNT_TILE, COUNT_TILE), COUNT_TILE)
            k = keys_ref[rows, :].reshape(COUNT_TILE // wide, wide, tq)
            pos = j * COUNT_TILE + lax.broadcasted_iota(jnp.int32, (COUNT_TILE, tq), 0)
            pos = pos.reshape(COUNT_TILE // wide, wide, tq)
            return rows, k, jnp.logical_and(k == thr_w[None], pos < cut[None])

        def tied_before(cut):
            cut = jnp.concatenate([cut] * (wide // SUBLANES), axis=0)

            def body(j, cnt):
                _, _, hit = tile_hits(j, cut)
                return cnt + jnp.sum(jnp.where(hit, 1.0, 0.0), axis=0)

            cnt = lax.fori_loop(0, n_count, body, jnp.zeros((wide, tq), f32))
            return jnp.broadcast_to(reduce_keys(cnt, jnp.sum), (SUBLANES, tq))

        def cut_step(step, last):
            cand = last | jnp.left_shift(jnp.int32(1), pos_bits - 1 - step)
            return jnp.where(tied_before(cand) < need, cand, last)

        last = lax.fori_loop(0, pos_bits, cut_step, jnp.zeros((SUBLANES, tq), jnp.int32))
        keep_before = jnp.where(tied, last + 1, int_max)
        keep_before = jnp.concatenate([keep_before] * (wide // SUBLANES), axis=0)

        def drop(j):
            rows, k, hit = tile_hits(j, keep_before)
            surplus = jnp.logical_and(k == thr_w[None], jnp.logical_not(hit))
            keys_ref[rows, :] = jnp.where(surplus, _INT_MIN, k).reshape(COUNT_TILE, tq)

        loop(n_count, drop)

    m_ref[...] = jnp.full(m_ref.shape, -1e30, f32)
    l_ref[...] = jnp.zeros(l_ref.shape, f32)
    acc_ref[...] = jnp.zeros(acc_ref.shape, f32)

    def attend(r0, nk, near):
        r0 = pl.multiple_of(r0, MAX_DISTANCE)
        rows = pl.ds(r0, nk)
        selected = keys_ref[rows, :] >= thr
        mask_bias = jnp.where(selected, 0.0, -jnp.inf)
        if not near:
            pos = r0 + lax.broadcasted_iota(jnp.int32, (nk, tq), 0)
            mask_bias = jnp.where(pos < w0, mask_bias, -jnp.inf)
        for g in range(N_KV_HEADS):
            cs = slice(g * HEAD_DIM, (g + 1) * HEAD_DIM)
            qg = q_ref[g * heads_per_kv:(g + 1) * heads_per_kv].reshape(heads_per_kv * tq, HEAD_DIM)
            s_all = lax.dot_general(k_ref[rows, cs], qg, nt, preferred_element_type=f32)
            for r in range(heads_per_kv):
                h = g * heads_per_kv + r
                s = s_all[:, r * tq:(r + 1) * tq] + mask_bias
                if near:
                    s = s + bias_ref[jnp.where(t0 == w0, 1, 0), h]
                s_ref[h, 0:nk, :] = s
        for g in range(N_KV_HEADS):
            alphas = []
            for r in range(heads_per_kv):
                h = g * heads_per_kv + r
                m_old = m_ref[h]
                m_new = jnp.maximum(m_old, reduce_keys(s_ref[h, 0:nk, :], jnp.max))
                alpha = jnp.exp2(m_old - m_new)
                p = jnp.exp2(s_ref[h, 0:nk, :] - m_new[0:1])
                m_ref[h] = m_new
                p_ref[g, 0:nk, r * tq:(r + 1) * tq] = p.astype(p_ref.dtype)
                alphas.append(alpha)
            pv = jnp.dot(vt_ref[g, :, rows], p_ref[g, 0:nk, :], preferred_element_type=f32)
            for r in range(heads_per_kv):
                h = g * heads_per_kv + r
                qs_ = slice(r * tq, (r + 1) * tq)
                acc_ref[h] = alphas[r][0:1] * acc_ref[h] + pv[0:HEAD_DIM, qs_]
                l_ref[h] = alphas[r] * l_ref[h] + pv[HEAD_DIM:HEAD_DIM + 1, qs_]

    loop((w0 + ATTN_TILE - 1) // ATTN_TILE, lambda j: attend(j * ATTN_TILE, ATTN_TILE, False))
    attend(w0, NEAR_KEYS, True)

    for h in range(N_HEADS):
        out_t = acc_ref[h] / l_ref[h][0:1]
        o_ref[:, h * HEAD_DIM:(h + 1) * HEAD_DIM] = out_t.T.astype(o_ref.dtype)


def _attention(q_hm, qi_hm, w_t, k_all, vt_all, ki_all, bias_tiles, *, n_batch, n_qblocks, n_keys, t0_base, t0_step,
               topk):
    tq = Q_TILE
    kv_w = N_KV_HEADS * HEAD_DIM
    once = pl.Buffered(1)
    if k_all.ndim == 3:
        key_specs = [
            pl.BlockSpec((None, n_keys, kv_w), lambda b, i: (b, 0, 0), pipeline_mode=once),
            pl.BlockSpec((None, N_KV_HEADS, V_ROWS, n_keys), lambda b, i: (b, 0, 0, 0), pipeline_mode=once),
            pl.BlockSpec((None, n_keys, LANES), lambda b, i: (b, 0, 0), pipeline_mode=once),
        ]
    else:
        key_specs = [
            pl.BlockSpec((n_keys, kv_w), lambda b, i: (b, 0), pipeline_mode=once),
            pl.BlockSpec((N_KV_HEADS, V_ROWS, n_keys), lambda b, i: (0, 0, b), pipeline_mode=once),
            pl.BlockSpec((n_keys, LANES), lambda b, i: (b, 0), pipeline_mode=once),
        ]
    assert n_keys % COUNT_TILE == 0 and tq == MAX_DISTANCE and NEAR_KEYS == SCORE_TILE
    assert ATTN_TILE % COUNT_TILE == 0 and ATTN_TILE == SCORE_STEP * SCORE_TILE
    assert t0_base % MAX_DISTANCE == 0 and t0_step % MAX_DISTANCE == 0
    last_w0 = max(t0_base + (n_qblocks - 1) * t0_step - MAX_DISTANCE, 0)
    assert -(-last_w0 // ATTN_TILE) * ATTN_TILE <= n_keys and last_w0 + NEAR_KEYS <= n_keys
    kernel = functools.partial(_attn_kernel, t0_base=t0_base, t0_step=t0_step, topk=float(topk),
                               idx_scale=(H_IDX * D_IDX) ** -0.5, pos_bits=(n_keys + COUNT_TILE).bit_length())
    return pl.pallas_call(
        kernel,
        out_shape=jax.ShapeDtypeStruct((n_batch * n_qblocks * tq, N_HEADS * HEAD_DIM), _MXU_DTYPE),
        grid=(n_batch, n_qblocks),
        in_specs=[
            pl.BlockSpec((N_HEADS, tq, HEAD_DIM), lambda b, i: (0, b * n_qblocks + i, 0)),
            pl.BlockSpec((H_IDX, tq, LANES), lambda b, i: (0, b * n_qblocks + i, 0)),
            pl.BlockSpec((H_IDX, tq), lambda b, i: (0, b * n_qblocks + i)),
            *key_specs,
            pl.BlockSpec((2, N_HEADS, NEAR_KEYS, tq), lambda b, i: (0, 0, 0, 0), pipeline_mode=once),
        ],
        out_specs=pl.BlockSpec((tq, N_HEADS * HEAD_DIM), lambda b, i: (b * n_qblocks + i, 0)),
        scratch_shapes=[
            pltpu.VMEM((n_keys + COUNT_TILE, tq), jnp.int32),
            pltpu.VMEM((N_HEADS, SUBLANES, tq), jnp.float32),
            pltpu.VMEM((N_HEADS, SUBLANES, tq), jnp.float32),
            pltpu.VMEM((N_HEADS, HEAD_DIM, tq), jnp.float32),
            pltpu.VMEM((N_HEADS, ATTN_TILE, tq), jnp.float32),
            pltpu.VMEM((N_KV_HEADS, ATTN_TILE, (N_HEADS // N_KV_HEADS) * tq), _MXU_DTYPE),
        ],
        compiler_params=_params("parallel", "arbitrary"),
        name="dsa_attention",
    )(q_hm, qi_hm, w_t, k_all, vt_all, ki_all, bias_tiles)


def _t5_bucket(rel):
    nb = NUM_BUCKETS // 2
    max_exact = nb // 2
    n = jnp.abs(rel)
    nf = jnp.maximum(n, max_exact).astype(jnp.float32)
    large = max_exact + (jnp.log(nf / max_exact) / math.log(MAX_DISTANCE / max_exact) * (nb - max_exact)).astype(jnp.int32)
    large = jnp.minimum(large, nb - 1)
    return jnp.where(rel > 0, nb, 0) + jnp.where(n < max_exact, n, large)


def _near_bias(rel_bias):
    d = jnp.array([-MAX_DISTANCE, 0], jnp.int32)[:, None, None]
    rel = d + jnp.arange(NEAR_KEYS, dtype=jnp.int32)[None, :, None] - jnp.arange(Q_TILE, dtype=jnp.int32)[None, None, :]
    hit = _t5_bucket(rel)[:, None, :, :, None] == jnp.arange(NUM_BUCKETS, dtype=jnp.int32)
    near = jnp.sum(jnp.where(hit, rel_bias.T[None, :, None, None, :], 0.0), axis=-1)
    far = rel_bias[_t5_bucket(jnp.int32(-MAX_DISTANCE))]
    return ((near - far[None, :, None, None]) * LOG2_E).astype(jnp.float32)


def _values_t(v):
    n, keys, _ = v.shape
    vt = jnp.transpose(v.reshape(n, keys, N_KV_HEADS, HEAD_DIM), (0, 2, 3, 1))
    ones = jnp.ones((n, N_KV_HEADS, 1, keys), v.dtype)
    zeros = jnp.zeros((n, N_KV_HEADS, V_ROWS - HEAD_DIM - 1, keys), v.dtype)
    return jnp.concatenate([vt, ones, zeros], axis=2)


def _pad_cols(w, block, width):
    lead = w.shape[:-1]
    n = w.shape[-1] // block
    w = w.reshape(*lead, n, block)
    w = jnp.pad(w, [(0, 0)] * len(lead) + [(0, 0), (0, width - block)])
    return w.reshape(*lead, n * width)


def kernel(x_prompt, x_sample, p_prompt, p_sample, cache_k, cache_v, cache_kidx, norm_g, w_ffn1_gu, w_ffn1_down,
           w_in, w_out, gmlp_ws, gmlp_b, gmlp_vnorm, w_ffn2_gu, w_ffn2_down, w_ple_gate, w_ple, rel_bias):
    depth = w_in.shape[0]
    n_b, seq, d_model = x_prompt.shape
    dec_b, dec_seq, _ = x_sample.shape
    past = cache_k.shape[2]
    rows_p, rows_s = n_b * seq, dec_b * dec_seq
    attn_w = N_HEADS * HEAD_DIM
    kv_w = N_KV_HEADS * HEAD_DIM
    idx_w = H_IDX * D_IDX
    gmlp_w = d_model - attn_w
    mx = _MXU_DTYPE
    assert seq % ATTN_TILE == 0 and dec_seq <= CHUNK and past % MAX_DISTANCE == 0 and dec_seq % SUBLANES == 0
    assert past >= MAX_DISTANCE
    assert rows_p % GMLP_TILE == 0 and rows_s % GMLP_TILE == 0 and GMLP_CHUNK % dec_seq == 0
    assert GMLP_TILE % GMLP_CHUNK == 0 and (rows_p + rows_s) % ROW_TILE == 0

    o1 = attn_w
    o2 = o1 + kv_w
    o3 = o2 + kv_w
    o4 = o3 + idx_w
    o6 = o4 + D_IDX + H_IDX
    w_q = w_in[:, :, :o1].astype(mx)
    w_qi = _pad_cols(w_in[:, :, o3:o4], D_IDX, LANES).astype(mx)
    w_kvki = w_in[:, :, o1:o3]
    w_kvki = jnp.concatenate([w_kvki, w_in[:, :, o4:o6]], axis=-1)
    w_kvki = jnp.pad(w_kvki, ((0, 0), (0, 0), (0, 2 * kv_w + LANES - w_kvki.shape[-1]))).astype(mx)
    w_uv = w_in[:, :, o6:].astype(mx)
    w1_gu, w1_dn = w_ffn1_gu.astype(mx), w_ffn1_down.astype(mx)
    w2_gu, w2_dn = w_ffn2_gu.astype(mx), w_ffn2_down.astype(mx)
    w_out2 = w_out.astype(mx).reshape(depth, 2, attn_w, d_model)
    w_pg, w_pe = w_ple_gate.astype(mx), w_ple.astype(mx)

    causal = jnp.tril(jnp.ones((GMLP_CHUNK, GMLP_CHUNK), bool))
    ws_full = jnp.where(causal, gmlp_ws, 0.0)
    reps = GMLP_CHUNK // dec_seq
    small = jnp.where(causal[:dec_seq, :dec_seq], gmlp_ws[:, :, :dec_seq, :dec_seq], 0.0)
    eye = jnp.eye(reps, dtype=gmlp_ws.dtype)
    ws_small = jnp.einsum('ab,lgij->lgaibj', eye, small).reshape(depth, GMLP_GROUPS, GMLP_CHUNK, GMLP_CHUNK)
    ws2 = jnp.stack([ws_full, ws_small], axis=1).astype(mx)
    b_small = jnp.tile(gmlp_b[:, :, :dec_seq], (1, 1, reps))
    b2 = jnp.stack([gmlp_b, b_small], axis=1)
    b2 = jnp.repeat(jnp.swapaxes(b2, 2, 3), gmlp_w // GMLP_GROUPS, axis=-1)

    near_bias = _near_bias(rel_bias)

    h = jnp.concatenate([x_prompt.reshape(rows_p, d_model), x_sample.reshape(rows_s, d_model)], axis=0)
    p_all = jnp.concatenate([p_prompt.reshape(depth, rows_p, -1), p_sample.reshape(depth, rows_s, -1)], axis=1)

    keys_s = past + dec_seq
    w0_s = past - MAX_DISTANCE
    keys_s_pad = max(-(-(w0_s + NEAR_KEYS) // COUNT_TILE) * COUNT_TILE, -(-w0_s // ATTN_TILE) * ATTN_TILE)
    assert keys_s_pad >= keys_s
    topk_p = min(TOPK_MAX, seq // 4)
    topk_s = min(TOPK_MAX, keys_s // 4)

    def sample_queries(x_hm):
        x = x_hm[:, rows_p:].reshape(x_hm.shape[0], dec_b, dec_seq, x_hm.shape[-1])
        x = jnp.pad(x, ((0, 0), (0, 0), (0, Q_TILE - dec_seq), (0, 0)))
        return x.reshape(x_hm.shape[0], dec_b * Q_TILE, x_hm.shape[-1])

    outs = [[] for _ in range(7)]
    for l in range(depth):
        g = norm_g[l][:, None, :]
        h = _ffn(h, g[0], g[1], w1_gu, w1_dn, l)

        q_hm = _proj(h, g[2], w_q[l], mx, True, "proj_q", out_scale=HEAD_DIM ** -0.5 * LOG2_E)
        qi_hm = _proj(h, g[2], w_qi[l], mx, True, "proj_qi")
        kvki, k_mx, ki_mx, vt_mx, w_t = _proj_kv(h, g[2], w_kvki[l])
        uv = _proj(h, g[2], w_uv[l], jnp.float32, False, "proj_uv")

        k_new, v_new, ki_new = kvki[:, :kv_w], kvki[:, kv_w:2 * kv_w], kvki[:, 2 * kv_w:2 * kv_w + D_IDX]

        attn_p = _attention(
            q_hm, qi_hm, w_t, k_mx, vt_mx, ki_mx, near_bias,
            n_batch=n_b, n_qblocks=seq // Q_TILE, n_keys=seq, t0_base=0, t0_step=Q_TILE, topk=topk_p)

        def keys_s_all(cache, new, width):
            full = jnp.concatenate([cache.reshape(dec_b, past, -1), new[rows_p:].reshape(dec_b, dec_seq, -1)], axis=1)
            full = jnp.pad(full, ((0, 0), (0, keys_s_pad - keys_s), (0, width - full.shape[-1])))
            return full.astype(mx)

        w_t_s = jnp.pad(w_t[:, rows_p:].reshape(H_IDX, dec_b, dec_seq), ((0, 0), (0, 0), (0, Q_TILE - dec_seq)))
        attn_s = _attention(
            sample_queries(q_hm), sample_queries(qi_hm), w_t_s.reshape(H_IDX, dec_b * Q_TILE),
            keys_s_all(cache_k[l], k_new, kv_w), _values_t(keys_s_all(cache_v[l], v_new, kv_w)),
            keys_s_all(cache_kidx[l], ki_new, LANES),
            near_bias, n_batch=dec_b, n_qblocks=1, n_keys=keys_s_pad, t0_base=past, t0_step=0, topk=topk_s)
        attn_s = attn_s.reshape(dec_b, Q_TILE, attn_w)[:, :dec_seq].reshape(rows_s, attn_w)
        attn = jnp.concatenate([attn_p, attn_s], axis=0)

        gm, vn = _gmlp(uv, gmlp_vnorm[l][None, :], ws2[l], b2[l], rows_p)
        h = _mixout(h, attn, gm, w_out2, g[3], l)
        h = _ffn(h, g[4], g[5], w2_gu, w2_dn, l)
        h = _ple(h, p_all[l], g[6], g[7], w_pg, w_pe, l)

        outs[0].append(k_new[:rows_p].reshape(n_b, seq, N_KV_HEADS, HEAD_DIM))
        outs[1].append(v_new[:rows_p].reshape(n_b, seq, N_KV_HEADS, HEAD_DIM))
        outs[2].append(ki_new[:rows_p].reshape(n_b, seq, D_IDX))
        outs[3].append(k_new[rows_p:].reshape(dec_b, dec_seq, N_KV_HEADS, HEAD_DIM))
        outs[4].append(v_new[rows_p:].reshape(dec_b, dec_seq, N_KV_HEADS, HEAD_DIM))
        outs[5].append(ki_new[rows_p:].reshape(dec_b, dec_seq, D_IDX))
        outs[6].append(vn[rows_p:].reshape(dec_b, dec_seq, gmlp_w))

    y_prompt = h[:rows_p].reshape(n_b, seq, d_model)
    y_sample = h[rows_p:].reshape(dec_b, dec_seq, d_model)
    return (y_prompt, y_sample) + tuple(jnp.stack(o) for o in outs)
```

```python
import functools
import math

import jax
import jax.numpy as jnp
from jax import lax
from jax.experimental import pallas as pl
from jax.experimental.pallas import tpu as pltpu

CHUNK = 64
N_HEADS = 8
N_KV_HEADS = 2
HEAD_DIM = 128
H_IDX = 16
D_IDX = 64
TOPK_MAX = 256
GMLP_CHUNK = 128
GMLP_GROUPS = 8
NUM_BUCKETS = 32
MAX_DISTANCE = 128
EPS = 1e-6
LOG2_E = 1.4426950408889634

LANES = 128
SUBLANES = 8
VMEM_LIMIT_BYTES = 56 * 1024 * 1024

ROW_TILE = 768
GMLP_TILE = 512
FF_TILE = 512
Q_TILE = 128
SCORE_TILE = 256
COUNT_TILE = 512
ATTN_TILE = 1024
SCORE_STEP = 4
NEAR_KEYS = 2 * MAX_DISTANCE
V_ROWS = HEAD_DIM + 16

_MXU_DTYPE = jnp.bfloat16
_INT_MIN = -(2 ** 31)


def _params(*semantics):
    return pltpu.CompilerParams(dimension_semantics=semantics, vmem_limit_bytes=VMEM_LIMIT_BYTES)


def _rms(x, g):
    return x * lax.rsqrt(jnp.mean(x * x, axis=-1, keepdims=True) + EPS) * g


def _ffn_kernel(h_ref, gpre_ref, gpost_ref, wg_ref, wu_ref, wd_ref, o_ref, xn_ref, acc_ref):
    j = pl.program_id(1)

    @pl.when(j == 0)
    def _():
        xn_ref[...] = _rms(h_ref[...], gpre_ref[...]).astype(xn_ref.dtype)
        acc_ref[...] = jnp.zeros_like(acc_ref)

    xn = xn_ref[...]
    gate = jnp.dot(xn, wg_ref[...], preferred_element_type=jnp.float32)
    up = jnp.dot(xn, wu_ref[...], preferred_element_type=jnp.float32)
    act = (jax.nn.silu(gate) * up).astype(wd_ref.dtype)
    acc_ref[...] += jnp.dot(act, wd_ref[...], preferred_element_type=jnp.float32)

    @pl.when(j == pl.num_programs(1) - 1)
    def _():
        o_ref[...] = h_ref[...] + 0.5 * _rms(acc_ref[...], gpost_ref[...])


def _ffn(h, g_pre, g_post, w_gu, w_down, layer):
    rows, d = h.shape
    d_ff = w_down.shape[1]
    nf = d_ff // FF_TILE
    assert rows % ROW_TILE == 0 and d_ff % FF_TILE == 0
    return pl.pallas_call(
        _ffn_kernel,
        out_shape=jax.ShapeDtypeStruct((rows, d), jnp.float32),
        grid=(rows // ROW_TILE, nf),
        in_specs=[
            pl.BlockSpec((ROW_TILE, d), lambda i, j: (i, 0)),
            pl.BlockSpec((1, d), lambda i, j: (0, 0)),
            pl.BlockSpec((1, d), lambda i, j: (0, 0)),
            pl.BlockSpec((None, d, FF_TILE), lambda i, j: (layer, 0, j)),
            pl.BlockSpec((None, d, FF_TILE), lambda i, j: (layer, 0, j + nf)),
            pl.BlockSpec((None, FF_TILE, d), lambda i, j: (layer, j, 0)),
        ],
        out_specs=pl.BlockSpec((ROW_TILE, d), lambda i, j: (i, 0)),
        scratch_shapes=[pltpu.VMEM((ROW_TILE, d), _MXU_DTYPE), pltpu.VMEM((ROW_TILE, d), jnp.float32)],
        compiler_params=_params("parallel", "arbitrary"),
        name="ffn",
    )(h, g_pre, g_post, w_gu, w_gu, w_down)


def _proj_kernel(h_ref, g_ref, w_ref, o_ref, *, head_major, out_scale):
    xn = _rms(h_ref[...], g_ref[...]).astype(w_ref.dtype)
    z = jnp.dot(xn, w_ref[...], preferred_element_type=jnp.float32)
    if out_scale != 1.0:
        z = z * out_scale
    if head_major:
        for hh in range(o_ref.shape[0]):
            o_ref[hh] = z[:, hh * LANES:(hh + 1) * LANES].astype(o_ref.dtype)
    else:
        o_ref[...] = z.astype(o_ref.dtype)


def _proj(h, g, w, out_dtype, head_major, name, out_scale=1.0):
    rows, d = h.shape
    n = w.shape[1]
    if head_major:
        out_shape = jax.ShapeDtypeStruct((n // LANES, rows, LANES), out_dtype)
        out_spec = pl.BlockSpec((n // LANES, ROW_TILE, LANES), lambda i: (0, i, 0))
    else:
        out_shape = jax.ShapeDtypeStruct((rows, n), out_dtype)
        out_spec = pl.BlockSpec((ROW_TILE, n), lambda i: (i, 0))
    return pl.pallas_call(
        functools.partial(_proj_kernel, head_major=head_major, out_scale=out_scale),
        out_shape=out_shape,
        grid=(rows // ROW_TILE,),
        in_specs=[
            pl.BlockSpec((ROW_TILE, d), lambda i: (i, 0)),
            pl.BlockSpec((1, d), lambda i: (0, 0)),
            pl.BlockSpec((d, n), lambda i: (0, 0)),
        ],
        out_specs=out_spec,
        compiler_params=_params("parallel"),
        name=name,
    )(h, g, w)


def _proj_kv_kernel(h_ref, g_ref, w_ref, kf_ref, vf_ref, kif_ref, k_ref, ki_ref, vt_ref, wt_ref):
    kv_w = N_KV_HEADS * HEAD_DIM
    xn = _rms(h_ref[...], g_ref[...]).astype(w_ref.dtype)
    z = jnp.dot(xn, w_ref[...], preferred_element_type=jnp.float32)
    kf_ref[...] = z[:, :kv_w]
    vf_ref[...] = z[:, kv_w:2 * kv_w]
    k_ref[...] = z[:, :kv_w].astype(k_ref.dtype)
    tail = z[:, 2 * kv_w:]
    kif_ref[...] = tail[:, :D_IDX]
    lane = lax.broadcasted_iota(jnp.int32, tail.shape, 1)
    ki_ref[...] = jnp.where(lane < D_IDX, tail, 0.0).astype(ki_ref.dtype)
    wt_ref[...] = tail.T[D_IDX:D_IDX + H_IDX, :]
    row = lax.broadcasted_iota(jnp.int32, (V_ROWS - HEAD_DIM, z.shape[0]), 0)
    for g in range(N_KV_HEADS):
        v_g = z[:, kv_w + g * HEAD_DIM:kv_w + (g + 1) * HEAD_DIM]
        vt_ref[g, 0:HEAD_DIM, :] = v_g.T.astype(vt_ref.dtype)
        vt_ref[g, HEAD_DIM:V_ROWS, :] = jnp.where(row == 0, 1.0, 0.0).astype(vt_ref.dtype)


def _proj_kv(h, g, w):
    rows, d = h.shape
    n = w.shape[1]
    kv_w = N_KV_HEADS * HEAD_DIM
    assert n == 2 * kv_w + LANES
    return pl.pallas_call(
        _proj_kv_kernel,
        out_shape=(jax.ShapeDtypeStruct((rows, kv_w), jnp.float32),
                   jax.ShapeDtypeStruct((rows, kv_w), jnp.float32),
                   jax.ShapeDtypeStruct((rows, D_IDX), jnp.float32),
                   jax.ShapeDtypeStruct((rows, kv_w), _MXU_DTYPE),
                   jax.ShapeDtypeStruct((rows, LANES), _MXU_DTYPE),
                   jax.ShapeDtypeStruct((N_KV_HEADS, V_ROWS, rows), _MXU_DTYPE),
                   jax.ShapeDtypeStruct((H_IDX, rows), jnp.float32)),
        grid=(rows // ROW_TILE,),
        in_specs=[
            pl.BlockSpec((ROW_TILE, d), lambda i: (i, 0)),
            pl.BlockSpec((1, d), lambda i: (0, 0)),
            pl.BlockSpec((d, n), lambda i: (0, 0)),
        ],
        out_specs=(pl.BlockSpec((ROW_TILE, kv_w), lambda i: (i, 0)),
                   pl.BlockSpec((ROW_TILE, kv_w), lambda i: (i, 0)),
                   pl.BlockSpec((ROW_TILE, D_IDX), lambda i: (i, 0)),
                   pl.BlockSpec((ROW_TILE, kv_w), lambda i: (i, 0)),
                   pl.BlockSpec((ROW_TILE, LANES), lambda i: (i, 0)),
                   pl.BlockSpec((N_KV_HEADS, V_ROWS, ROW_TILE), lambda i: (0, 0, i)),
                   pl.BlockSpec((H_IDX, ROW_TILE), lambda i: (0, i))),
        compiler_params=_params("parallel"),
        name="proj_kv",
    )(h, g, w)


def _gmlp_kernel(uv_ref, gain_ref, ws_ref, b_ref, o_ref, vn_ref):
    width = o_ref.shape[1]
    gdim = width // GMLP_GROUPS
    for c in range(uv_ref.shape[0] // GMLP_CHUNK):
        rs = slice(c * GMLP_CHUNK, (c + 1) * GMLP_CHUNK)
        u = jax.nn.gelu(uv_ref[rs, :width])
        vn = _rms(jax.nn.gelu(uv_ref[rs, width:]), gain_ref[...])
        vn_ref[rs, :] = vn
        vnb = vn.astype(ws_ref.dtype)
        for g in range(GMLP_GROUPS):
            cs = slice(g * gdim, (g + 1) * gdim)
            s = jnp.dot(ws_ref[g], vnb[:, cs], preferred_element_type=jnp.float32) + b_ref[:, cs]
            o_ref[rs, cs] = (u[:, cs] * s).astype(o_ref.dtype)


def _gmlp(uv, gain, ws2, b2, n_prompt_rows):
    rows = uv.shape[0]
    width = uv.shape[1] // 2
    tile = GMLP_TILE
    n_prompt_tiles = n_prompt_rows // tile

    def sel(i):
        return jnp.where(i >= n_prompt_tiles, 1, 0)

    return pl.pallas_call(
        _gmlp_kernel,
        out_shape=(jax.ShapeDtypeStruct((rows, width), _MXU_DTYPE),
                   jax.ShapeDtypeStruct((rows, width), jnp.float32)),
        grid=(rows // tile,),
        in_specs=[
            pl.BlockSpec((tile, 2 * width), lambda i: (i, 0)),
            pl.BlockSpec((1, width), lambda i: (0, 0)),
            pl.BlockSpec((None, GMLP_GROUPS, GMLP_CHUNK, GMLP_CHUNK), lambda i: (sel(i), 0, 0, 0)),
            pl.BlockSpec((None, GMLP_CHUNK, width), lambda i: (sel(i), 0, 0)),
        ],
        out_specs=(pl.BlockSpec((tile, width), lambda i: (i, 0)),
                   pl.BlockSpec((tile, width), lambda i: (i, 0))),
        compiler_params=_params("parallel"),
        name="gmlp",
    )(uv, gain, ws2, b2)


def _mixout_kernel(h_ref, a_ref, m_ref, w_ref, g_ref, o_ref):
    y = jnp.dot(a_ref[...], w_ref[0], preferred_element_type=jnp.float32)
    y = y + jnp.dot(m_ref[...], w_ref[1], preferred_element_type=jnp.float32)
    o_ref[...] = h_ref[...] + _rms(y, g_ref[...])


def _mixout(h, attn, gm, w_out2, g, layer):
    rows, d = h.shape
    half = attn.shape[1]
    return pl.pallas_call(
        _mixout_kernel,
        out_shape=jax.ShapeDtypeStruct((rows, d), jnp.float32),
        grid=(rows // ROW_TILE,),
        in_specs=[
            pl.BlockSpec((ROW_TILE, d), lambda i: (i, 0)),
            pl.BlockSpec((ROW_TILE, half), lambda i: (i, 0)),
            pl.BlockSpec((ROW_TILE, half), lambda i: (i, 0)),
            pl.BlockSpec((None, 2, half, d), lambda i: (layer, 0, 0, 0)),
            pl.BlockSpec((1, d), lambda i: (0, 0)),
        ],
        out_specs=pl.BlockSpec((ROW_TILE, d), lambda i: (i, 0)),
        compiler_params=_params("parallel"),
        name="mixout",
    )(h, attn, gm, w_out2, g)


def _ple_kernel(h_ref, p_ref, gpre_ref, gpost_ref, wpg_ref, wpe_ref, o_ref):
    h = h_ref[...]
    xn = _rms(h, gpre_ref[...]).astype(wpg_ref.dtype)
    gate = jax.nn.sigmoid(jnp.dot(xn, wpg_ref[...], preferred_element_type=jnp.float32))
    emb = jnp.dot(p_ref[...].astype(wpe_ref.dtype), wpe_ref[...], preferred_element_type=jnp.float32)
    o_ref[...] = h + _rms(gate * emb, gpost_ref[...])


def _ple(h, p, g_pre, g_post, w_pg, w_pe, layer):
    rows, d = h.shape
    d_ple = p.shape[1]
    return pl.pallas_call(
        _ple_kernel,
        out_shape=jax.ShapeDtypeStruct((rows, d), jnp.float32),
        grid=(rows // ROW_TILE,),
        in_specs=[
            pl.BlockSpec((ROW_TILE, d), lambda i: (i, 0)),
            pl.BlockSpec((ROW_TILE, d_ple), lambda i: (i, 0)),
            pl.BlockSpec((1, d), lambda i: (0, 0)),
            pl.BlockSpec((1, d), lambda i: (0, 0)),
            pl.BlockSpec((None, d, d), lambda i: (layer, 0, 0)),
            pl.BlockSpec((None, d_ple, d), lambda i: (layer, 0, 0)),
        ],
        out_specs=pl.BlockSpec((ROW_TILE, d), lambda i: (i, 0)),
        compiler_params=_params("parallel"),
        name="ple",
    )(h, p, g_pre, g_post, w_pg, w_pe)


def _attn_kernel(q_ref, qi_ref, w_ref, k_ref, vt_ref, ki_ref, bias_ref, o_ref,
                 keys_ref, m_ref, l_ref, acc_ref, s_ref, p_ref, *, t0_base, t0_step, topk, idx_scale, pos_bits,
                 n_queries):
    f32 = jnp.float32
    tq = Q_TILE
    heads_per_kv = N_HEADS // N_KV_HEADS
    nt = (((1,), (1,)), ((), ()))
    t0 = t0_base + pl.program_id(1) * t0_step
    w0 = jnp.maximum(t0 - MAX_DISTANCE, 0)

    def loop(n, fn):
        lax.fori_loop(0, n, lambda j, c: (fn(j), c)[1], 0)

    def reduce_keys(x, op):
        x = op(x.reshape(-1, SUBLANES * SUBLANES, tq), axis=0)
        x = op(x.reshape(SUBLANES, SUBLANES, tq), axis=0)
        return op(x, axis=0, keepdims=True)

    def score_rows(r0, near):
        r0 = pl.multiple_of(r0, MAX_DISTANCE)
        qs = qi_ref[...].reshape(H_IDX * tq, LANES)
        d = lax.dot_general(ki_ref[pl.ds(r0, SCORE_TILE), :], qs, nt, preferred_element_type=f32)
        score = jnp.zeros((SCORE_TILE, tq), f32)
        for h in range(H_IDX):
            score = score + (w_ref[h:h + 1, :] * idx_scale) * jnp.maximum(d[:, h * tq:(h + 1) * tq], 0.0)
        bits = lax.bitcast_convert_type(score, jnp.int32)
        key = bits ^ ((bits >> 31) & 0x7FFFFFFF)
        if near:
            s_pos = r0 + lax.broadcasted_iota(jnp.int32, (SCORE_TILE, tq), 0)
            t_pos = t0 + lax.broadcasted_iota(jnp.int32, (SCORE_TILE, tq), 1)
            key = jnp.where((s_pos // CHUNK) <= (t_pos // CHUNK), key, _INT_MIN)
        keys_ref[pl.ds(r0, SCORE_TILE), :] = key

    def score_step(j):
        for i in range(SCORE_STEP):
            score_rows((SCORE_STEP * j + i) * SCORE_TILE, False)

    loop((w0 + SCORE_STEP * SCORE_TILE - 1) // (SCORE_STEP * SCORE_TILE), score_step)
    score_rows(w0, True)
    pad_rows = COUNT_TILE - MAX_DISTANCE
    pad = pl.ds(pl.multiple_of(w0 + NEAR_KEYS, MAX_DISTANCE), pad_rows)
    keys_ref[pad, :] = jnp.full((pad_rows, tq), _INT_MIN, jnp.int32)

    n_count = (w0 + NEAR_KEYS + COUNT_TILE - 1) // COUNT_TILE
    wide = SUBLANES * SUBLANES

    def count_ge(cand):
        cand = jnp.concatenate([cand] * (wide // SUBLANES), axis=0)

        def body(j, cnt):
            k = keys_ref[pl.ds(pl.multiple_of(j * COUNT_TILE, COUNT_TILE), COUNT_TILE), :]
            k = k.reshape(COUNT_TILE // wide, wide, tq)
            for i in range(COUNT_TILE // wide):
                cnt = jnp.where(k[i] >= cand, cnt + 1.0, cnt)
            return cnt

        cnt = lax.fori_loop(0, n_count, body, jnp.zeros((wide, tq), f32))
        return jnp.broadcast_to(reduce_keys(cnt, jnp.sum), (SUBLANES, tq))

    def search_step(step, state):
        thr_u, at_thr = state
        cand_u = thr_u | jnp.left_shift(jnp.int32(1), 31 - step)
        cnt = count_ge(cand_u ^ _INT_MIN)
        take = cnt >= topk
        return jnp.where(take, cand_u, thr_u), jnp.where(take, cnt, at_thr)

    n_admissible = count_ge(jnp.full((SUBLANES, tq), _INT_MIN + 1, jnp.int32))
    blind_steps = 20
    state = lax.fori_loop(0, blind_steps, search_step, (jnp.zeros((SUBLANES, tq), jnp.int32), n_admissible))

    steps_per_test = 4
    real_query = lax.broadcasted_iota(jnp.int32, (SUBLANES, tq), 1) < n_queries

    def unsettled(state):
        step, _, at_thr = state
        settled = jnp.logical_or(jnp.logical_or(at_thr == topk, n_admissible <= topk), jnp.logical_not(real_query))
        settled = jnp.where(settled, 1.0, 0.0)
        return jnp.logical_and(step < 32, jnp.sum(settled) < float(SUBLANES * tq))

    def tested_steps(state):
        step, thr_u, at_thr = state
        for i in range(steps_per_test):
            thr_u, at_thr = search_step(step + i, (thr_u, at_thr))
        return step + steps_per_test, thr_u, at_thr

    _, thr_u, at_thr = lax.while_loop(unsettled, tested_steps, (jnp.int32(blind_steps),) + state)
    thr_s = thr_u ^ _INT_MIN
    thr = jnp.maximum(thr_s, _INT_MIN + 1)[0:1]

    tied = jnp.logical_and(jnp.logical_and(at_thr > topk, n_admissible > topk), real_query)

    @pl.when(jnp.sum(jnp.where(tied, 1.0, 0.0)) > 0.0)
    def _break_ties():
        int_max = -(_INT_MIN + 1)
        is_max = thr_s == int_max
        above = jnp.where(is_max, 0.0, count_ge(jnp.where(is_max, thr_s, thr_s + 1)))
        need = topk - above
        thr_w = jnp.concatenate([thr_s] * (wide // SUBLANES), axis=0)

        def tile_hits(j, cut):
            rows = pl.ds(pl.multiple_of(j * COUNT_TILE, COUNT_TILE), COUNT_TILE)
            k = keys_ref[rows, :].reshape(COUNT_TILE // wide, wide, tq)
            pos = j * COUNT_TILE + lax.broadcasted_iota(jnp.int32, (COUNT_TILE, tq), 0)
            pos = pos.reshape(COUNT_TILE // wide, wide, tq)
            return rows, k, jnp.logical_and(k == thr_w[None], pos < cut[None])

        def tied_before(cut):
            cut = jnp.concatenate([cut] * (wide // SUBLANES), axis=0)

            def body(j, cnt):
                _, _, hit = tile_hits(j, cut)
                return cnt + jnp.sum(jnp.where(hit, 1.0, 0.0), axis=0)

            cnt = lax.fori_loop(0, n_count, body, jnp.zeros((wide, tq), f32))
            return jnp.broadcast_to(reduce_keys(cnt, jnp.sum), (SUBLANES, tq))

        def cut_step(step, last):
            cand = last | jnp.left_shift(jnp.int32(1), pos_bits - 1 - step)
            return jnp.where(tied_before(cand) < need, cand, last)

        last = lax.fori_loop(0, pos_bits, cut_step, jnp.zeros((SUBLANES, tq), jnp.int32))
        keep_before = jnp.where(tied, last + 1, int_max)
        keep_before = jnp.concatenate([keep_before] * (wide // SUBLANES), axis=0)

        def drop(j):
            rows, k, hit = tile_hits(j, keep_before)
            surplus = jnp.logical_and(k == thr_w[None], jnp.logical_not(hit))
            keys_ref[rows, :] = jnp.where(surplus, _INT_MIN, k).reshape(COUNT_TILE, tq)

        loop(n_count, drop)

    m_ref[...] = jnp.full(m_ref.shape, -1e30, f32)
    l_ref[...] = jnp.zeros(l_ref.shape, f32)
    acc_ref[...] = jnp.zeros(acc_ref.shape, f32)

    def attend(r0, nk, near):
        r0 = pl.multiple_of(r0, MAX_DISTANCE)
        rows = pl.ds(r0, nk)
        selected = keys_ref[rows, :] >= thr
        mask_bias = jnp.where(selected, 0.0, -jnp.inf)
        if not near:
            pos = r0 + lax.broadcasted_iota(jnp.int32, (nk, tq), 0)
            mask_bias = jnp.where(pos < w0, mask_bias, -jnp.inf)
        for g in range(N_KV_HEADS):
            cs = slice(g * HEAD_DIM, (g + 1) * HEAD_DIM)
            qg = q_ref[g * heads_per_kv:(g + 1) * heads_per_kv].reshape(heads_per_kv * tq, HEAD_DIM)
            s_all = lax.dot_general(k_ref[rows, cs], qg, nt, preferred_element_type=f32)
            for r in range(heads_per_kv):
                h = g * heads_per_kv + r
                s = s_all[:, r * tq:(r + 1) * tq] + mask_bias
                if near:
                    s = s + bias_ref[jnp.where(t0 == w0, 1, 0), h]
                s_ref[h, 0:nk, :] = s
        for g in range(N_KV_HEADS):
            alphas = []
            for r in range(heads_per_kv):
                h = g * heads_per_kv + r
                m_old = m_ref[h]
                m_new = jnp.maximum(m_old, reduce_keys(s_ref[h, 0:nk, :], jnp.max))
                alpha = jnp.exp2(m_old - m_new)
                p = jnp.exp2(s_ref[h, 0:nk, :] - m_new[0:1])
                m_ref[h] = m_new
                p_ref[g, 0:nk, r * tq:(r + 1) * tq] = p.astype(p_ref.dtype)
                alphas.append(alpha)
            pv = jnp.dot(vt_ref[g, :, rows], p_ref[g, 0:nk, :], preferred_element_type=f32)
            for r in range(heads_per_kv):
                h = g * heads_per_kv + r
                qs_ = slice(r * tq, (r + 1) * tq)
                acc_ref[h] = alphas[r][0:1] * acc_ref[h] + pv[0:HEAD_DIM, qs_]
                l_ref[h] = alphas[r] * l_ref[h] + pv[HEAD_DIM:HEAD_DIM + 1, qs_]

    loop((w0 + ATTN_TILE - 1) // ATTN_TILE, lambda j: attend(j * ATTN_TILE, ATTN_TILE, False))
    attend(w0, NEAR_KEYS, True)

    for h in range(N_HEADS):
        out_t = acc_ref[h] / l_ref[h][0:1]
        o_ref[:, h * HEAD_DIM:(h + 1) * HEAD_DIM] = out_t.T.astype(o_ref.dtype)


def _attention(q_hm, qi_hm, w_t, k_all, vt_all, ki_all, bias_tiles, *, n_batch, n_qblocks, n_keys, t0_base, t0_step,
               topk, n_queries=Q_TILE):
    tq = Q_TILE
    kv_w = N_KV_HEADS * HEAD_DIM
    once = pl.Buffered(1)
    if k_all.ndim == 3:
        key_specs = [
            pl.BlockSpec((None, n_keys, kv_w), lambda b, i: (b, 0, 0), pipeline_mode=once),
            pl.BlockSpec((None, N_KV_HEADS, V_ROWS, n_keys), lambda b, i: (b, 0, 0, 0), pipeline_mode=once),
            pl.BlockSpec((None, n_keys, LANES), lambda b, i: (b, 0, 0), pipeline_mode=once),
        ]
    else:
        key_specs = [
            pl.BlockSpec((n_keys, kv_w), lambda b, i: (b, 0), pipeline_mode=once),
            pl.BlockSpec((N_KV_HEADS, V_ROWS, n_keys), lambda b, i: (0, 0, b), pipeline_mode=once),
            pl.BlockSpec((n_keys, LANES), lambda b, i: (b, 0), pipeline_mode=once),
        ]
    assert n_keys % COUNT_TILE == 0 and tq == MAX_DISTANCE and NEAR_KEYS == SCORE_TILE
    assert ATTN_TILE % COUNT_TILE == 0 and ATTN_TILE == SCORE_STEP * SCORE_TILE
    assert t0_base % MAX_DISTANCE == 0 and t0_step % MAX_DISTANCE == 0
    last_w0 = max(t0_base + (n_qblocks - 1) * t0_step - MAX_DISTANCE, 0)
    assert -(-last_w0 // ATTN_TILE) * ATTN_TILE <= n_keys and last_w0 + NEAR_KEYS <= n_keys
    kernel = functools.partial(_attn_kernel, t0_base=t0_base, t0_step=t0_step, topk=float(topk),
                               idx_scale=(H_IDX * D_IDX) ** -0.5, pos_bits=(n_keys + COUNT_TILE).bit_length(),
                               n_queries=n_queries)
    return pl.pallas_call(
        kernel,
        out_shape=jax.ShapeDtypeStruct((n_batch * n_qblocks * tq, N_HEADS * HEAD_DIM), _MXU_DTYPE),
        grid=(n_batch, n_qblocks),
        in_specs=[
            pl.BlockSpec((N_HEADS, tq, HEAD_DIM), lambda b, i: (0, b * n_qblocks + i, 0)),
            pl.BlockSpec((H_IDX, tq, LANES), lambda b, i: (0, b * n_qblocks + i, 0)),
            pl.BlockSpec((H_IDX, tq), lambda b, i: (0, b * n_qblocks + i)),
            *key_specs,
            pl.BlockSpec((2, N_HEADS, NEAR_KEYS, tq), lambda b, i: (0, 0, 0, 0), pipeline_mode=once),
        ],
        out_specs=pl.BlockSpec((tq, N_HEADS * HEAD_DIM), lambda b, i: (b * n_qblocks + i, 0)),
        scratch_shapes=[
            pltpu.VMEM((n_keys + COUNT_TILE, tq), jnp.int32),
            pltpu.VMEM((N_HEADS, SUBLANES, tq), jnp.float32),
            pltpu.VMEM((N_HEADS, SUBLANES, tq), jnp.float32),
            pltpu.VMEM((N_HEADS, HEAD_DIM, tq), jnp.float32),
            pltpu.VMEM((N_HEADS, ATTN_TILE, tq), jnp.float32),
            pltpu.VMEM((N_KV_HEADS, ATTN_TILE, (N_HEADS // N_KV_HEADS) * tq), _MXU_DTYPE),
        ],
        compiler_params=_params("parallel", "arbitrary"),
        name="dsa_attention",
    )(q_hm, qi_hm, w_t, k_all, vt_all, ki_all, bias_tiles)


def _t5_bucket(rel):
    nb = NUM_BUCKETS // 2
    max_exact = nb // 2
    n = jnp.abs(rel)
    nf = jnp.maximum(n, max_exact).astype(jnp.float32)
    large = max_exact + (jnp.log(nf / max_exact) / math.log(MAX_DISTANCE / max_exact) * (nb - max_exact)).astype(jnp.int32)
    large = jnp.minimum(large, nb - 1)
    return jnp.where(rel > 0, nb, 0) + jnp.where(n < max_exact, n, large)


def _near_bias(rel_bias):
    d = jnp.array([-MAX_DISTANCE, 0], jnp.int32)[:, None, None]
    rel = d + jnp.arange(NEAR_KEYS, dtype=jnp.int32)[None, :, None] - jnp.arange(Q_TILE, dtype=jnp.int32)[None, None, :]
    hit = _t5_bucket(rel)[:, None, :, :, None] == jnp.arange(NUM_BUCKETS, dtype=jnp.int32)
    near = jnp.sum(jnp.where(hit, rel_bias.T[None, :, None, None, :], 0.0), axis=-1)
    far = rel_bias[_t5_bucket(jnp.int32(-MAX_DISTANCE))]
    return ((near - far[None, :, None, None]) * LOG2_E).astype(jnp.float32)


def _values_t(v):
    n, keys, _ = v.shape
    vt = jnp.transpose(v.reshape(n, keys, N_KV_HEADS, HEAD_DIM), (0, 2, 3, 1))
    ones = jnp.ones((n, N_KV_HEADS, 1, keys), v.dtype)
    zeros = jnp.zeros((n, N_KV_HEADS, V_ROWS - HEAD_DIM - 1, keys), v.dtype)
    return jnp.concatenate([vt, ones, zeros], axis=2)


def _pad_cols(w, block, width):
    lead = w.shape[:-1]
    n = w.shape[-1] // block
    w = w.reshape(*lead, n, block)
    w = jnp.pad(w, [(0, 0)] * len(lead) + [(0, 0), (0, width - block)])
    return w.reshape(*lead, n * width)


def kernel(x_prompt, x_sample, p_prompt, p_sample, cache_k, cache_v, cache_kidx, norm_g, w_ffn1_gu, w_ffn1_down,
           w_in, w_out, gmlp_ws, gmlp_b, gmlp_vnorm, w_ffn2_gu, w_ffn2_down, w_ple_gate, w_ple, rel_bias):
    depth = w_in.shape[0]
    n_b, seq, d_model = x_prompt.shape
    dec_b, dec_seq, _ = x_sample.shape
    past = cache_k.shape[2]
    rows_p, rows_s = n_b * seq, dec_b * dec_seq
    attn_w = N_HEADS * HEAD_DIM
    kv_w = N_KV_HEADS * HEAD_DIM
    idx_w = H_IDX * D_IDX
    gmlp_w = d_model - attn_w
    mx = _MXU_DTYPE
    assert seq % ATTN_TILE == 0 and dec_seq <= CHUNK and past % MAX_DISTANCE == 0 and dec_seq % SUBLANES == 0
    assert past >= MAX_DISTANCE
    assert rows_p % GMLP_TILE == 0 and rows_s % GMLP_TILE == 0 and GMLP_CHUNK % dec_seq == 0
    assert GMLP_TILE % GMLP_CHUNK == 0 and (rows_p + rows_s) % ROW_TILE == 0

    o1 = attn_w
    o2 = o1 + kv_w
    o3 = o2 + kv_w
    o4 = o3 + idx_w
    o6 = o4 + D_IDX + H_IDX
    w_q = w_in[:, :, :o1].astype(mx)
    w_qi = _pad_cols(w_in[:, :, o3:o4], D_IDX, LANES).astype(mx)
    w_kvki = w_in[:, :, o1:o3]
    w_kvki = jnp.concatenate([w_kvki, w_in[:, :, o4:o6]], axis=-1)
    w_kvki = jnp.pad(w_kvki, ((0, 0), (0, 0), (0, 2 * kv_w + LANES - w_kvki.shape[-1]))).astype(mx)
    w_uv = w_in[:, :, o6:].astype(mx)
    w1_gu, w1_dn = w_ffn1_gu.astype(mx), w_ffn1_down.astype(mx)
    w2_gu, w2_dn = w_ffn2_gu.astype(mx), w_ffn2_down.astype(mx)
    w_out2 = w_out.astype(mx).reshape(depth, 2, attn_w, d_model)
    w_pg, w_pe = w_ple_gate.astype(mx), w_ple.astype(mx)

    causal = jnp.tril(jnp.ones((GMLP_CHUNK, GMLP_CHUNK), bool))
    ws_full = jnp.where(causal, gmlp_ws, 0.0)
    reps = GMLP_CHUNK // dec_seq
    small = jnp.where(causal[:dec_seq, :dec_seq], gmlp_ws[:, :, :dec_seq, :dec_seq], 0.0)
    eye = jnp.eye(reps, dtype=gmlp_ws.dtype)
    ws_small = jnp.einsum('ab,lgij->lgaibj', eye, small).reshape(depth, GMLP_GROUPS, GMLP_CHUNK, GMLP_CHUNK)
    ws2 = jnp.stack([ws_full, ws_small], axis=1).astype(mx)
    b_small = jnp.tile(gmlp_b[:, :, :dec_seq], (1, 1, reps))
    b2 = jnp.stack([gmlp_b, b_small], axis=1)
    b2 = jnp.repeat(jnp.swapaxes(b2, 2, 3), gmlp_w // GMLP_GROUPS, axis=-1)

    near_bias = _near_bias(rel_bias)

    h = jnp.concatenate([x_prompt.reshape(rows_p, d_model), x_sample.reshape(rows_s, d_model)], axis=0)
    p_all = jnp.concatenate([p_prompt.reshape(depth, rows_p, -1), p_sample.reshape(depth, rows_s, -1)], axis=1)

    keys_s = past + dec_seq
    w0_s = past - MAX_DISTANCE
    keys_s_pad = max(-(-(w0_s + NEAR_KEYS) // COUNT_TILE) * COUNT_TILE, -(-w0_s // ATTN_TILE) * ATTN_TILE)
    assert keys_s_pad >= keys_s
    topk_p = min(TOPK_MAX, seq // 4)
    topk_s = min(TOPK_MAX, keys_s // 4)

    def sample_queries(x_hm):
        x = x_hm[:, rows_p:].reshape(x_hm.shape[0], dec_b, dec_seq, x_hm.shape[-1])
        x = jnp.pad(x, ((0, 0), (0, 0), (0, Q_TILE - dec_seq), (0, 0)))
        return x.reshape(x_hm.shape[0], dec_b * Q_TILE, x_hm.shape[-1])

    outs = [[] for _ in range(7)]
    for l in range(depth):
        g = norm_g[l][:, None, :]
        h = _ffn(h, g[0], g[1], w1_gu, w1_dn, l)

        q_hm = _proj(h, g[2], w_q[l], mx, True, "proj_q", out_scale=HEAD_DIM ** -0.5 * LOG2_E)
        qi_hm = _proj(h, g[2], w_qi[l], mx, True, "proj_qi")
        k_new, v_new, ki_new, k_mx, ki_mx, vt_mx, w_t = _proj_kv(h, g[2], w_kvki[l])
        uv = _proj(h, g[2], w_uv[l], jnp.float32, False, "proj_uv")

        attn_p = _attention(
            q_hm, qi_hm, w_t, k_mx, vt_mx, ki_mx, near_bias,
            n_batch=n_b, n_qblocks=seq // Q_TILE, n_keys=seq, t0_base=0, t0_step=Q_TILE, topk=topk_p)

        def keys_s_all(cache, new, width):
            full = jnp.concatenate([cache.reshape(dec_b, past, -1), new[rows_p:].reshape(dec_b, dec_seq, -1)], axis=1)
            full = jnp.pad(full, ((0, 0), (0, keys_s_pad - keys_s), (0, width - full.shape[-1])))
            return full.astype(mx)

        w_t_s = jnp.pad(w_t[:, rows_p:].reshape(H_IDX, dec_b, dec_seq), ((0, 0), (0, 0), (0, Q_TILE - dec_seq)))
        attn_s = _attention(
            sample_queries(q_hm), sample_queries(qi_hm), w_t_s.reshape(H_IDX, dec_b * Q_TILE),
            keys_s_all(cache_k[l], k_new, kv_w), _values_t(keys_s_all(cache_v[l], v_new, kv_w)),
            keys_s_all(cache_kidx[l], ki_new, LANES),
            near_bias, n_batch=dec_b, n_qblocks=1, n_keys=keys_s_pad, t0_base=past, t0_step=0, topk=topk_s,
            n_queries=dec_seq)
        attn_s = attn_s.reshape(dec_b, Q_TILE, attn_w)[:, :dec_seq].reshape(rows_s, attn_w)
        attn = jnp.concatenate([attn_p, attn_s], axis=0)

        gm, vn = _gmlp(uv, gmlp_vnorm[l][None, :], ws2[l], b2[l], rows_p)
        h = _mixout(h, attn, gm, w_out2, g[3], l)
        h = _ffn(h, g[4], g[5], w2_gu, w2_dn, l)
        h = _ple(h, p_all[l], g[6], g[7], w_pg, w_pe, l)

        outs[0].append(k_new[:rows_p].reshape(n_b, seq, N_KV_HEADS, HEAD_DIM))
        outs[1].append(v_new[:rows_p].reshape(n_b, seq, N_KV_HEADS, HEAD_DIM))
        outs[2].append(ki_new[:rows_p].reshape(n_b, seq, D_IDX))
        outs[3].append(k_new[rows_p:].reshape(dec_b, dec_seq, N_KV_HEADS, HEAD_DIM))
        outs[4].append(v_new[rows_p:].reshape(dec_b, dec_seq, N_KV_HEADS, HEAD_DIM))
        outs[5].append(ki_new[rows_p:].reshape(dec_b, dec_seq, D_IDX))
        outs[6].append(vn[rows_p:].reshape(dec_b, dec_seq, gmlp_w))

    y_prompt = h[:rows_p].reshape(n_b, seq, d_model)
    y_sample = h[rows_p:].reshape(dec_b, dec_seq, d_model)
    return (y_prompt, y_sample) + tuple(jnp.stack(o) for o in outs)
```

```python
import functools
import math

import jax
import jax.numpy as jnp
from jax import lax
from jax.experimental import pallas as pl
from jax.experimental.pallas import tpu as pltpu

CHUNK = 64
N_HEADS = 8
N_KV_HEADS = 2
HEAD_DIM = 128
H_IDX = 16
D_IDX = 64
TOPK_MAX = 256
GMLP_CHUNK = 128
GMLP_GROUPS = 8
NUM_BUCKETS = 32
MAX_DISTANCE = 128
EPS = 1e-6
LOG2_E = 1.4426950408889634

LANES = 128
SUBLANES = 8
VMEM_LIMIT_BYTES = 56 * 1024 * 1024

ROW_TILE = 768
GMLP_TILE = 512
FF_TILE = 512
Q_TILE = 128
SCORE_TILE = 256
COUNT_TILE = 512
ATTN_TILE = 1024
SCORE_STEP = 4
NEAR_KEYS = 2 * MAX_DISTANCE
V_ROWS = HEAD_DIM + 16

_MXU_DTYPE = jnp.bfloat16
_INT_MIN = -(2 ** 31)


def _params(*semantics):
    return pltpu.CompilerParams(dimension_semantics=semantics, vmem_limit_bytes=VMEM_LIMIT_BYTES)


def _rms(x, g):
    return x * lax.rsqrt(jnp.mean(x * x, axis=-1, keepdims=True) + EPS) * g


def _ffn_kernel(h_ref, gpre_ref, gpost_ref, wg_ref, wu_ref, wd_ref, o_ref, xn_ref, acc_ref):
    j = pl.program_id(1)

    @pl.when(j == 0)
    def _():
        xn_ref[...] = _rms(h_ref[...], gpre_ref[...]).astype(xn_ref.dtype)
        acc_ref[...] = jnp.zeros_like(acc_ref)

    xn = xn_ref[...]
    gate = jnp.dot(xn, wg_ref[...], preferred_element_type=jnp.float32)
    up = jnp.dot(xn, wu_ref[...], preferred_element_type=jnp.float32)
    act = (jax.nn.silu(gate) * up).astype(wd_ref.dtype)
    acc_ref[...] += jnp.dot(act, wd_ref[...], preferred_element_type=jnp.float32)

    @pl.when(j == pl.num_programs(1) - 1)
    def _():
        o_ref[...] = h_ref[...] + 0.5 * _rms(acc_ref[...], gpost_ref[...])


def _ffn(h, g_pre, g_post, w_gu, w_down, layer):
    rows, d = h.shape
    d_ff = w_down.shape[1]
    nf = d_ff // FF_TILE
    assert rows % ROW_TILE == 0 and d_ff % FF_TILE == 0
    return pl.pallas_call(
        _ffn_kernel,
        out_shape=jax.ShapeDtypeStruct((rows, d), jnp.float32),
        grid=(rows // ROW_TILE, nf),
        in_specs=[
            pl.BlockSpec((ROW_TILE, d), lambda i, j: (i, 0)),
            pl.BlockSpec((1, d), lambda i, j: (0, 0)),
            pl.BlockSpec((1, d), lambda i, j: (0, 0)),
            pl.BlockSpec((None, d, FF_TILE), lambda i, j: (layer, 0, j)),
            pl.BlockSpec((None, d, FF_TILE), lambda i, j: (layer, 0, j + nf)),
            pl.BlockSpec((None, FF_TILE, d), lambda i, j: (layer, j, 0)),
        ],
        out_specs=pl.BlockSpec((ROW_TILE, d), lambda i, j: (i, 0)),
        scratch_shapes=[pltpu.VMEM((ROW_TILE, d), _MXU_DTYPE), pltpu.VMEM((ROW_TILE, d), jnp.float32)],
        compiler_params=_params("parallel", "arbitrary"),
        name="ffn",
    )(h, g_pre, g_post, w_gu, w_gu, w_down)


def _proj_kernel(h_ref, g_ref, w_ref, o_ref, *, head_major, out_scale):
    xn = _rms(h_ref[...], g_ref[...]).astype(w_ref.dtype)
    z = jnp.dot(xn, w_ref[...], preferred_element_type=jnp.float32)
    if out_scale != 1.0:
        z = z * out_scale
    if head_major:
        for hh in range(o_ref.shape[0]):
            o_ref[hh] = z[:, hh * LANES:(hh + 1) * LANES].astype(o_ref.dtype)
    else:
        o_ref[...] = z.astype(o_ref.dtype)


def _proj(h, g, w, out_dtype, head_major, name, out_scale=1.0):
    rows, d = h.shape
    n = w.shape[1]
    if head_major:
        out_shape = jax.ShapeDtypeStruct((n // LANES, rows, LANES), out_dtype)
        out_spec = pl.BlockSpec((n // LANES, ROW_TILE, LANES), lambda i: (0, i, 0))
    else:
        out_shape = jax.ShapeDtypeStruct((rows, n), out_dtype)
        out_spec = pl.BlockSpec((ROW_TILE, n), lambda i: (i, 0))
    return pl.pallas_call(
        functools.partial(_proj_kernel, head_major=head_major, out_scale=out_scale),
        out_shape=out_shape,
        grid=(rows // ROW_TILE,),
        in_specs=[
            pl.BlockSpec((ROW_TILE, d), lambda i: (i, 0)),
            pl.BlockSpec((1, d), lambda i: (0, 0)),
            pl.BlockSpec((d, n), lambda i: (0, 0)),
        ],
        out_specs=out_spec,
        compiler_params=_params("parallel"),
        name=name,
    )(h, g, w)


def _proj_kv_kernel(h_ref, g_ref, w_ref, kf_ref, vf_ref, kif_ref, k_ref, ki_ref, vt_ref, wt_ref):
    kv_w = N_KV_HEADS * HEAD_DIM
    xn = _rms(h_ref[...], g_ref[...]).astype(w_ref.dtype)
    z = jnp.dot(xn, w_ref[...], preferred_element_type=jnp.float32)
    kf_ref[...] = z[:, :kv_w]
    vf_ref[...] = z[:, kv_w:2 * kv_w]
    k_ref[...] = z[:, :kv_w].astype(k_ref.dtype)
    tail = z[:, 2 * kv_w:]
    kif_ref[...] = tail[:, :D_IDX]
    lane = lax.broadcasted_iota(jnp.int32, tail.shape, 1)
    ki_ref[...] = jnp.where(lane < D_IDX, tail, 0.0).astype(ki_ref.dtype)
    wt_ref[...] = tail.T[D_IDX:D_IDX + H_IDX, :]
    row = lax.broadcasted_iota(jnp.int32, (V_ROWS - HEAD_DIM, z.shape[0]), 0)
    for g in range(N_KV_HEADS):
        v_g = z[:, kv_w + g * HEAD_DIM:kv_w + (g + 1) * HEAD_DIM]
        vt_ref[g, 0:HEAD_DIM, :] = v_g.T.astype(vt_ref.dtype)
        vt_ref[g, HEAD_DIM:V_ROWS, :] = jnp.where(row == 0, 1.0, 0.0).astype(vt_ref.dtype)


def _proj_kv(h, g, w):
    rows, d = h.shape
    n = w.shape[1]
    kv_w = N_KV_HEADS * HEAD_DIM
    assert n == 2 * kv_w + LANES
    return pl.pallas_call(
        _proj_kv_kernel,
        out_shape=(jax.ShapeDtypeStruct((rows, kv_w), jnp.float32),
                   jax.ShapeDtypeStruct((rows, kv_w), jnp.float32),
                   jax.ShapeDtypeStruct((rows, D_IDX), jnp.float32),
                   jax.ShapeDtypeStruct((rows, kv_w), _MXU_DTYPE),
                   jax.ShapeDtypeStruct((rows, LANES), _MXU_DTYPE),
                   jax.ShapeDtypeStruct((N_KV_HEADS, V_ROWS, rows), _MXU_DTYPE),
                   jax.ShapeDtypeStruct((H_IDX, rows), jnp.float32)),
        grid=(rows // ROW_TILE,),
        in_specs=[
            pl.BlockSpec((ROW_TILE, d), lambda i: (i, 0)),
            pl.BlockSpec((1, d), lambda i: (0, 0)),
            pl.BlockSpec((d, n), lambda i: (0, 0)),
        ],
        out_specs=(pl.BlockSpec((ROW_TILE, kv_w), lambda i: (i, 0)),
                   pl.BlockSpec((ROW_TILE, kv_w), lambda i: (i, 0)),
                   pl.BlockSpec((ROW_TILE, D_IDX), lambda i: (i, 0)),
                   pl.BlockSpec((ROW_TILE, kv_w), lambda i: (i, 0)),
                   pl.BlockSpec((ROW_TILE, LANES), lambda i: (i, 0)),
                   pl.BlockSpec((N_KV_HEADS, V_ROWS, ROW_TILE), lambda i: (0, 0, i)),
                   pl.BlockSpec((H_IDX, ROW_TILE), lambda i: (0, i))),
        compiler_params=_params("parallel"),
        name="proj_kv",
    )(h, g, w)


def _gmlp_kernel(uv_ref, gain_ref, ws_ref, b_ref, o_ref, vn_ref):
    width = o_ref.shape[1]
    gdim = width // GMLP_GROUPS
    for c in range(uv_ref.shape[0] // GMLP_CHUNK):
        rs = slice(c * GMLP_CHUNK, (c + 1) * GMLP_CHUNK)
        u = jax.nn.gelu(uv_ref[rs, :width])
        vn = _rms(jax.nn.gelu(uv_ref[rs, width:]), gain_ref[...])
        vn_ref[rs, :] = vn
        vnb = vn.astype(ws_ref.dtype)
        for g in range(GMLP_GROUPS):
            cs = slice(g * gdim, (g + 1) * gdim)
            s = jnp.dot(ws_ref[g], vnb[:, cs], preferred_element_type=jnp.float32) + b_ref[:, cs]
            o_ref[rs, cs] = (u[:, cs] * s).astype(o_ref.dtype)


def _gmlp(uv, gain, ws2, b2, n_prompt_rows):
    rows = uv.shape[0]
    width = uv.shape[1] // 2
    tile = GMLP_TILE
    n_prompt_tiles = n_prompt_rows // tile

    def sel(i):
        return jnp.where(i >= n_prompt_tiles, 1, 0)

    return pl.pallas_call(
        _gmlp_kernel,
        out_shape=(jax.ShapeDtypeStruct((rows, width), _MXU_DTYPE),
                   jax.ShapeDtypeStruct((rows, width), jnp.float32)),
        grid=(rows // tile,),
        in_specs=[
            pl.BlockSpec((tile, 2 * width), lambda i: (i, 0)),
            pl.BlockSpec((1, width), lambda i: (0, 0)),
            pl.BlockSpec((None, GMLP_GROUPS, GMLP_CHUNK, GMLP_CHUNK), lambda i: (sel(i), 0, 0, 0)),
            pl.BlockSpec((None, GMLP_CHUNK, width), lambda i: (sel(i), 0, 0)),
        ],
        out_specs=(pl.BlockSpec((tile, width), lambda i: (i, 0)),
                   pl.BlockSpec((tile, width), lambda i: (i, 0))),
        compiler_params=_params("parallel"),
        name="gmlp",
    )(uv, gain, ws2, b2)


def _mixout_kernel(h_ref, a_ref, m_ref, w_ref, g_ref, o_ref):
    y = jnp.dot(a_ref[...], w_ref[0], preferred_element_type=jnp.float32)
    y = y + jnp.dot(m_ref[...], w_ref[1], preferred_element_type=jnp.float32)
    o_ref[...] = h_ref[...] + _rms(y, g_ref[...])


def _mixout(h, attn, gm, w_out2, g, layer):
    rows, d = h.shape
    half = attn.shape[1]
    return pl.pallas_call(
        _mixout_kernel,
        out_shape=jax.ShapeDtypeStruct((rows, d), jnp.float32),
        grid=(rows // ROW_TILE,),
        in_specs=[
            pl.BlockSpec((ROW_TILE, d), lambda i: (i, 0)),
            pl.BlockSpec((ROW_TILE, half), lambda i: (i, 0)),
            pl.BlockSpec((ROW_TILE, half), lambda i: (i, 0)),
            pl.BlockSpec((None, 2, half, d), lambda i: (layer, 0, 0, 0)),
            pl.BlockSpec((1, d), lambda i: (0, 0)),
        ],
        out_specs=pl.BlockSpec((ROW_TILE, d), lambda i: (i, 0)),
        compiler_params=_params("parallel"),
        name="mixout",
    )(h, attn, gm, w_out2, g)


def _ple_kernel(h_ref, p_ref, gpre_ref, gpost_ref, wpg_ref, wpe_ref, o_ref):
    h = h_ref[...]
    xn = _rms(h, gpre_ref[...]).astype(wpg_ref.dtype)
    gate = jax.nn.sigmoid(jnp.dot(xn, wpg_ref[...], preferred_element_type=jnp.float32))
    emb = jnp.dot(p_ref[...].astype(wpe_ref.dtype), wpe_ref[...], preferred_element_type=jnp.float32)
    o_ref[...] = h + _rms(gate * emb, gpost_ref[...])


def _ple(h, p, g_pre, g_post, w_pg, w_pe, layer):
    rows, d = h.shape
    d_ple = p.shape[1]
    return pl.pallas_call(
        _ple_kernel,
        out_shape=jax.ShapeDtypeStruct((rows, d), jnp.float32),
        grid=(rows // ROW_TILE,),
        in_specs=[
            pl.BlockSpec((ROW_TILE, d), lambda i: (i, 0)),
            pl.BlockSpec((ROW_TILE, d_ple), lambda i: (i, 0)),
            pl.BlockSpec((1, d), lambda i: (0, 0)),
            pl.BlockSpec((1, d), lambda i: (0, 0)),
            pl.BlockSpec((None, d, d), lambda i: (layer, 0, 0)),
            pl.BlockSpec((None, d_ple, d), lambda i: (layer, 0, 0)),
        ],
        out_specs=pl.BlockSpec((ROW_TILE, d), lambda i: (i, 0)),
        compiler_params=_params("parallel"),
        name="ple",
    )(h, p, g_pre, g_post, w_pg, w_pe)


def _attn_kernel(q_ref, qi_ref, w_ref, k_ref, vt_ref, ki_ref, bias_ref, o_ref,
                 keys_ref, m_ref, l_ref, acc_ref, s_ref, p_ref, *, t0_base, t0_step, topk, idx_scale, pos_bits,
                 n_queries):
    f32 = jnp.float32
    tq = Q_TILE
    heads_per_kv = N_HEADS // N_KV_HEADS
    nt = (((1,), (1,)), ((), ()))
    t0 = t0_base + pl.program_id(1) * t0_step
    w0 = jnp.maximum(t0 - MAX_DISTANCE, 0)

    def loop(n, fn):
        lax.fori_loop(0, n, lambda j, c: (fn(j), c)[1], 0)

    def reduce_keys(x, op):
        x = op(x.reshape(-1, SUBLANES * SUBLANES, tq), axis=0)
        x = op(x.reshape(SUBLANES, SUBLANES, tq), axis=0)
        return op(x, axis=0, keepdims=True)

    def score_rows(r0, near):
        r0 = pl.multiple_of(r0, MAX_DISTANCE)
        qs = qi_ref[...].reshape(H_IDX * tq, LANES)
        d = lax.dot_general(ki_ref[pl.ds(r0, SCORE_TILE), :], qs, nt, preferred_element_type=f32)
        score = jnp.zeros((SCORE_TILE, tq), f32)
        for h in range(H_IDX):
            score = score + (w_ref[h:h + 1, :] * idx_scale) * jnp.maximum(d[:, h * tq:(h + 1) * tq], 0.0)
        bits = lax.bitcast_convert_type(score, jnp.int32)
        key = bits ^ ((bits >> 31) & 0x7FFFFFFF)
        if near:
            s_pos = r0 + lax.broadcasted_iota(jnp.int32, (SCORE_TILE, tq), 0)
            t_pos = t0 + lax.broadcasted_iota(jnp.int32, (SCORE_TILE, tq), 1)
            key = jnp.where((s_pos // CHUNK) <= (t_pos // CHUNK), key, _INT_MIN)
        keys_ref[pl.ds(r0, SCORE_TILE), :] = key

    def score_step(j):
        for i in range(SCORE_STEP):
            score_rows((SCORE_STEP * j + i) * SCORE_TILE, False)

    loop((w0 + SCORE_STEP * SCORE_TILE - 1) // (SCORE_STEP * SCORE_TILE), score_step)
    score_rows(w0, True)
    pad_rows = COUNT_TILE - MAX_DISTANCE
    pad = pl.ds(pl.multiple_of(w0 + NEAR_KEYS, MAX_DISTANCE), pad_rows)
    keys_ref[pad, :] = jnp.full((pad_rows, tq), _INT_MIN, jnp.int32)

    n_count = (w0 + NEAR_KEYS + COUNT_TILE - 1) // COUNT_TILE
    wide = SUBLANES * SUBLANES

    def count_ge(cand):
        cand = jnp.concatenate([cand] * (wide // SUBLANES), axis=0)

        def body(j, cnt):
            k = keys_ref[pl.ds(pl.multiple_of(j * COUNT_TILE, COUNT_TILE), COUNT_TILE), :]
            k = k.reshape(COUNT_TILE // wide, wide, tq)
            for i in range(COUNT_TILE // wide):
                cnt = jnp.where(k[i] >= cand, cnt + 1.0, cnt)
            return cnt

        cnt = lax.fori_loop(0, n_count, body, jnp.zeros((wide, tq), f32))
        return jnp.broadcast_to(reduce_keys(cnt, jnp.sum), (SUBLANES, tq))

    def search_step(step, state):
        thr_u, at_thr = state
        cand_u = thr_u | jnp.left_shift(jnp.int32(1), 31 - step)
        cnt = count_ge(cand_u ^ _INT_MIN)
        take = cnt >= topk
        return jnp.where(take, cand_u, thr_u), jnp.where(take, cnt, at_thr)

    q_chunk = (t0 + lax.broadcasted_iota(jnp.int32, (SUBLANES, tq), 1)) // CHUNK
    n_admissible = jnp.minimum((q_chunk + 1) * CHUNK, w0 + NEAR_KEYS).astype(f32)
    blind_steps = 22
    state = lax.fori_loop(0, blind_steps, search_step, (jnp.zeros((SUBLANES, tq), jnp.int32), n_admissible))

    steps_per_test = 2
    real_query = lax.broadcasted_iota(jnp.int32, (SUBLANES, tq), 1) < n_queries

    def unsettled(state):
        step, _, at_thr = state
        settled = jnp.logical_or(jnp.logical_or(at_thr == topk, n_admissible <= topk), jnp.logical_not(real_query))
        settled = jnp.where(settled, 1.0, 0.0)
        return jnp.logical_and(step < 32, jnp.sum(settled) < float(SUBLANES * tq))

    def tested_steps(state):
        step, thr_u, at_thr = state
        for i in range(steps_per_test):
            thr_u, at_thr = search_step(step + i, (thr_u, at_thr))
        return step + steps_per_test, thr_u, at_thr

    _, thr_u, at_thr = lax.while_loop(unsettled, tested_steps, (jnp.int32(blind_steps),) + state)
    thr_s = thr_u ^ _INT_MIN
    thr = jnp.maximum(thr_s, _INT_MIN + 1)[0:1]

    tied = jnp.logical_and(jnp.logical_and(at_thr > topk, n_admissible > topk), real_query)

    @pl.when(jnp.sum(jnp.where(tied, 1.0, 0.0)) > 0.0)
    def _break_ties():
        int_max = -(_INT_MIN + 1)
        is_max = thr_s == int_max
        above = jnp.where(is_max, 0.0, count_ge(jnp.where(is_max, thr_s, thr_s + 1)))
        need = topk - above
        thr_w = jnp.concatenate([thr_s] * (wide // SUBLANES), axis=0)

        def tile_hits(j, cut):
            rows = pl.ds(pl.multiple_of(j * COUNT_TILE, COUNT_TILE), COUNT_TILE)
            k = keys_ref[rows, :].reshape(COUNT_TILE // wide, wide, tq)
            pos = j * COUNT_TILE + lax.broadcasted_iota(jnp.int32, (COUNT_TILE, tq), 0)
            pos = pos.reshape(COUNT_TILE // wide, wide, tq)
            return rows, k, jnp.logical_and(k == thr_w[None], pos < cut[None])

        def tied_before(cut):
            cut = jnp.concatenate([cut] * (wide // SUBLANES), axis=0)

            def body(j, cnt):
                _, _, hit = tile_hits(j, cut)
                return cnt + jnp.sum(jnp.where(hit, 1.0, 0.0), axis=0)

            cnt = lax.fori_loop(0, n_count, body, jnp.zeros((wide, tq), f32))
            return jnp.broadcast_to(reduce_keys(cnt, jnp.sum), (SUBLANES, tq))

        def cut_step(step, last):
            cand = last | jnp.left_shift(jnp.int32(1), pos_bits - 1 - step)
            return jnp.where(tied_before(cand) < need, cand, last)

        last = lax.fori_loop(0, pos_bits, cut_step, jnp.zeros((SUBLANES, tq), jnp.int32))
        keep_before = jnp.where(tied, last + 1, int_max)
        keep_before = jnp.concatenate([keep_before] * (wide // SUBLANES), axis=0)

        def drop(j):
            rows, k, hit = tile_hits(j, keep_before)
            surplus = jnp.logical_and(k == thr_w[None], jnp.logical_not(hit))
            keys_ref[rows, :] = jnp.where(surplus, _INT_MIN, k).reshape(COUNT_TILE, tq)

        loop(n_count, drop)

    m_ref[...] = jnp.full(m_ref.shape, -1e30, f32)
    l_ref[...] = jnp.zeros(l_ref.shape, f32)
    acc_ref[...] = jnp.zeros(acc_ref.shape, f32)

    def attend(r0, nk, near):
        r0 = pl.multiple_of(r0, MAX_DISTANCE)
        rows = pl.ds(r0, nk)
        selected = keys_ref[rows, :] >= thr
        mask_bias = jnp.where(selected, 0.0, -jnp.inf)
        if not near:
            pos = r0 + lax.broadcasted_iota(jnp.int32, (nk, tq), 0)
            mask_bias = jnp.where(pos < w0, mask_bias, -jnp.inf)
        for g in range(N_KV_HEADS):
            cs = slice(g * HEAD_DIM, (g + 1) * HEAD_DIM)
            qg = q_ref[g * heads_per_kv:(g + 1) * heads_per_kv].reshape(heads_per_kv * tq, HEAD_DIM)
            s_all = lax.dot_general(k_ref[rows, cs], qg, nt, preferred_element_type=f32)
            for r in range(heads_per_kv):
                h = g * heads_per_kv + r
                s = s_all[:, r * tq:(r + 1) * tq] + mask_bias
                if near:
                    s = s + bias_ref[jnp.where(t0 == w0, 1, 0), h]
                s_ref[h, 0:nk, :] = s
        for g in range(N_KV_HEADS):
            alphas = []
            for r in range(heads_per_kv):
                h = g * heads_per_kv + r
                m_old = m_ref[h]
                m_new = jnp.maximum(m_old, reduce_keys(s_ref[h, 0:nk, :], jnp.max))
                alpha = jnp.exp2(m_old - m_new)
                p = jnp.exp2(s_ref[h, 0:nk, :] - m_new[0:1])
                m_ref[h] = m_new
                p_ref[g, 0:nk, r * tq:(r + 1) * tq] = p.astype(p_ref.dtype)
                alphas.append(alpha)
            pv = jnp.dot(vt_ref[g, :, rows], p_ref[g, 0:nk, :], preferred_element_type=f32)
            for r in range(heads_per_kv):
                h = g * heads_per_kv + r
                qs_ = slice(r * tq, (r + 1) * tq)
                acc_ref[h] = alphas[r][0:1] * acc_ref[h] + pv[0:HEAD_DIM, qs_]
                l_ref[h] = alphas[r] * l_ref[h] + pv[HEAD_DIM:HEAD_DIM + 1, qs_]

    loop((w0 + ATTN_TILE - 1) // ATTN_TILE, lambda j: attend(j * ATTN_TILE, ATTN_TILE, False))
    attend(w0, NEAR_KEYS, True)

    for h in range(N_HEADS):
        out_t = acc_ref[h] / l_ref[h][0:1]
        o_ref[:, h * HEAD_DIM:(h + 1) * HEAD_DIM] = out_t.T.astype(o_ref.dtype)


def _attention(q_hm, qi_hm, w_t, k_all, vt_all, ki_all, bias_tiles, *, n_batch, n_qblocks, n_keys, t0_base, t0_step,
               topk, n_queries=Q_TILE):
    tq = Q_TILE
    kv_w = N_KV_HEADS * HEAD_DIM
    once = pl.Buffered(1)
    if k_all.ndim == 3:
        key_specs = [
            pl.BlockSpec((None, n_keys, kv_w), lambda b, i: (b, 0, 0), pipeline_mode=once),
            pl.BlockSpec((None, N_KV_HEADS, V_ROWS, n_keys), lambda b, i: (b, 0, 0, 0), pipeline_mode=once),
            pl.BlockSpec((None, n_keys, LANES), lambda b, i: (b, 0, 0), pipeline_mode=once),
        ]
    else:
        key_specs = [
            pl.BlockSpec((n_keys, kv_w), lambda b, i: (b, 0), pipeline_mode=once),
            pl.BlockSpec((N_KV_HEADS, V_ROWS, n_keys), lambda b, i: (0, 0, b), pipeline_mode=once),
            pl.BlockSpec((n_keys, LANES), lambda b, i: (b, 0), pipeline_mode=once),
        ]
    assert n_keys % COUNT_TILE == 0 and tq == MAX_DISTANCE and NEAR_KEYS == SCORE_TILE
    assert ATTN_TILE % COUNT_TILE == 0 and ATTN_TILE == SCORE_STEP * SCORE_TILE
    assert t0_base % MAX_DISTANCE == 0 and t0_step % MAX_DISTANCE == 0
    last_w0 = max(t0_base + (n_qblocks - 1) * t0_step - MAX_DISTANCE, 0)
    assert -(-last_w0 // ATTN_TILE) * ATTN_TILE <= n_keys and last_w0 + NEAR_KEYS <= n_keys
    kernel = functools.partial(_attn_kernel, t0_base=t0_base, t0_step=t0_step, topk=float(topk),
                               idx_scale=(H_IDX * D_IDX) ** -0.5, pos_bits=(n_keys + COUNT_TILE).bit_length(),
                               n_queries=n_queries)
    return pl.pallas_call(
        kernel,
        out_shape=jax.ShapeDtypeStruct((n_batch * n_qblocks * tq, N_HEADS * HEAD_DIM), _MXU_DTYPE),
        grid=(n_batch, n_qblocks),
        in_specs=[
            pl.BlockSpec((N_HEADS, tq, HEAD_DIM), lambda b, i: (0, b * n_qblocks + i, 0)),
            pl.BlockSpec((H_IDX, tq, LANES), lambda b, i: (0, b * n_qblocks + i, 0)),
            pl.BlockSpec((H_IDX, tq), lambda b, i: (0, b * n_qblocks + i)),
            *key_specs,
            pl.BlockSpec((2, N_HEADS, NEAR_KEYS, tq), lambda b, i: (0, 0, 0, 0), pipeline_mode=once),
        ],
        out_specs=pl.BlockSpec((tq, N_HEADS * HEAD_DIM), lambda b, i: (b * n_qblocks + i, 0)),
        scratch_shapes=[
            pltpu.VMEM((n_keys + COUNT_TILE, tq), jnp.int32),
            pltpu.VMEM((N_HEADS, SUBLANES, tq), jnp.float32),
            pltpu.VMEM((N_HEADS, SUBLANES, tq), jnp.float32),
            pltpu.VMEM((N_HEADS, HEAD_DIM, tq), jnp.float32),
            pltpu.VMEM((N_HEADS, ATTN_TILE, tq), jnp.float32),
            pltpu.VMEM((N_KV_HEADS, ATTN_TILE, (N_HEADS // N_KV_HEADS) * tq), _MXU_DTYPE),
        ],
        compiler_params=_params("parallel", "arbitrary"),
        name="dsa_attention",
    )(q_hm, qi_hm, w_t, k_all, vt_all, ki_all, bias_tiles)


def _t5_bucket(rel):
    nb = NUM_BUCKETS // 2
    max_exact = nb // 2
    n = jnp.abs(rel)
    nf = jnp.maximum(n, max_exact).astype(jnp.float32)
    large = max_exact + (jnp.log(nf / max_exact) / math.log(MAX_DISTANCE / max_exact) * (nb - max_exact)).astype(jnp.int32)
    large = jnp.minimum(large, nb - 1)
    return jnp.where(rel > 0, nb, 0) + jnp.where(n < max_exact, n, large)


def _near_bias(rel_bias):
    d = jnp.array([-MAX_DISTANCE, 0], jnp.int32)[:, None, None]
    rel = d + jnp.arange(NEAR_KEYS, dtype=jnp.int32)[None, :, None] - jnp.arange(Q_TILE, dtype=jnp.int32)[None, None, :]
    hit = _t5_bucket(rel)[:, None, :, :, None] == jnp.arange(NUM_BUCKETS, dtype=jnp.int32)
    near = jnp.sum(jnp.where(hit, rel_bias.T[None, :, None, None, :], 0.0), axis=-1)
    far = rel_bias[_t5_bucket(jnp.int32(-MAX_DISTANCE))]
    return ((near - far[None, :, None, None]) * LOG2_E).astype(jnp.float32)


def _values_t(v):
    n, keys, _ = v.shape
    vt = jnp.transpose(v.reshape(n, keys, N_KV_HEADS, HEAD_DIM), (0, 2, 3, 1))
    ones = jnp.ones((n, N_KV_HEADS, 1, keys), v.dtype)
    zeros = jnp.zeros((n, N_KV_HEADS, V_ROWS - HEAD_DIM - 1, keys), v.dtype)
    return jnp.concatenate([vt, ones, zeros], axis=2)


def _pad_cols(w, block, width):
    lead = w.shape[:-1]
    n = w.shape[-1] // block
    w = w.reshape(*lead, n, block)
    w = jnp.pad(w, [(0, 0)] * len(lead) + [(0, 0), (0, width - block)])
    return w.reshape(*lead, n * width)


def kernel(x_prompt, x_sample, p_prompt, p_sample, cache_k, cache_v, cache_kidx, norm_g, w_ffn1_gu, w_ffn1_down,
           w_in, w_out, gmlp_ws, gmlp_b, gmlp_vnorm, w_ffn2_gu, w_ffn2_down, w_ple_gate, w_ple, rel_bias):
    depth = w_in.shape[0]
    n_b, seq, d_model = x_prompt.shape
    dec_b, dec_seq, _ = x_sample.shape
    past = cache_k.shape[2]
    rows_p, rows_s = n_b * seq, dec_b * dec_seq
    attn_w = N_HEADS * HEAD_DIM
    kv_w = N_KV_HEADS * HEAD_DIM
    idx_w = H_IDX * D_IDX
    gmlp_w = d_model - attn_w
    mx = _MXU_DTYPE
    assert seq % ATTN_TILE == 0 and dec_seq <= CHUNK and past % MAX_DISTANCE == 0 and dec_seq % SUBLANES == 0
    assert past >= MAX_DISTANCE
    assert rows_p % GMLP_TILE == 0 and rows_s % GMLP_TILE == 0 and GMLP_CHUNK % dec_seq == 0
    assert GMLP_TILE % GMLP_CHUNK == 0 and (rows_p + rows_s) % ROW_TILE == 0

    o1 = attn_w
    o2 = o1 + kv_w
    o3 = o2 + kv_w
    o4 = o3 + idx_w
    o6 = o4 + D_IDX + H_IDX
    w_q = w_in[:, :, :o1].astype(mx)
    w_qi = _pad_cols(w_in[:, :, o3:o4], D_IDX, LANES).astype(mx)
    w_kvki = w_in[:, :, o1:o3]
    w_kvki = jnp.concatenate([w_kvki, w_in[:, :, o4:o6]], axis=-1)
    w_kvki = jnp.pad(w_kvki, ((0, 0), (0, 0), (0, 2 * kv_w + LANES - w_kvki.shape[-1]))).astype(mx)
    w_uv = w_in[:, :, o6:].astype(mx)
    w1_gu, w1_dn = w_ffn1_gu.astype(mx), w_ffn1_down.astype(mx)
    w2_gu, w2_dn = w_ffn2_gu.astype(mx), w_ffn2_down.astype(mx)
    w_out2 = w_out.astype(mx).reshape(depth, 2, attn_w, d_model)
    w_pg, w_pe = w_ple_gate.astype(mx), w_ple.astype(mx)

    causal = jnp.tril(jnp.ones((GMLP_CHUNK, GMLP_CHUNK), bool))
    ws_full = jnp.where(causal, gmlp_ws, 0.0)
    reps = GMLP_CHUNK // dec_seq
    small = jnp.where(causal[:dec_seq, :dec_seq], gmlp_ws[:, :, :dec_seq, :dec_seq], 0.0)
    eye = jnp.eye(reps, dtype=gmlp_ws.dtype)
    ws_small = jnp.einsum('ab,lgij->lgaibj', eye, small).reshape(depth, GMLP_GROUPS, GMLP_CHUNK, GMLP_CHUNK)
    ws2 = jnp.stack([ws_full, ws_small], axis=1).astype(mx)
    b_small = jnp.tile(gmlp_b[:, :, :dec_seq], (1, 1, reps))
    b2 = jnp.stack([gmlp_b, b_small], axis=1)
    b2 = jnp.repeat(jnp.swapaxes(b2, 2, 3), gmlp_w // GMLP_GROUPS, axis=-1)

    near_bias = _near_bias(rel_bias)

    h = jnp.concatenate([x_prompt.reshape(rows_p, d_model), x_sample.reshape(rows_s, d_model)], axis=0)
    p_all = jnp.concatenate([p_prompt.reshape(depth, rows_p, -1), p_sample.reshape(depth, rows_s, -1)], axis=1)

    keys_s = past + dec_seq
    w0_s = past - MAX_DISTANCE
    keys_s_pad = max(-(-(w0_s + NEAR_KEYS) // COUNT_TILE) * COUNT_TILE, -(-w0_s // ATTN_TILE) * ATTN_TILE)
    assert keys_s_pad >= keys_s
    topk_p = min(TOPK_MAX, seq // 4)
    topk_s = min(TOPK_MAX, keys_s // 4)

    def sample_queries(x_hm):
        x = x_hm[:, rows_p:].reshape(x_hm.shape[0], dec_b, dec_seq, x_hm.shape[-1])
        x = jnp.pad(x, ((0, 0), (0, 0), (0, Q_TILE - dec_seq), (0, 0)))
        return x.reshape(x_hm.shape[0], dec_b * Q_TILE, x_hm.shape[-1])

    outs = [[] for _ in range(7)]
    for l in range(depth):
        g = norm_g[l][:, None, :]
        h = _ffn(h, g[0], g[1], w1_gu, w1_dn, l)

        q_hm = _proj(h, g[2], w_q[l], mx, True, "proj_q", out_scale=HEAD_DIM ** -0.5 * LOG2_E)
        qi_hm = _proj(h, g[2], w_qi[l], mx, True, "proj_qi")
        k_new, v_new, ki_new, k_mx, ki_mx, vt_mx, w_t = _proj_kv(h, g[2], w_kvki[l])
        uv = _proj(h, g[2], w_uv[l], jnp.float32, False, "proj_uv")

        attn_p = _attention(
            q_hm, qi_hm, w_t, k_mx, vt_mx, ki_mx, near_bias,
            n_batch=n_b, n_qblocks=seq // Q_TILE, n_keys=seq, t0_base=0, t0_step=Q_TILE, topk=topk_p)

        def keys_s_all(cache, new, width):
            full = jnp.concatenate([cache.reshape(dec_b, past, -1), new[rows_p:].reshape(dec_b, dec_seq, -1)], axis=1)
            full = jnp.pad(full, ((0, 0), (0, keys_s_pad - keys_s), (0, width - full.shape[-1])))
            return full.astype(mx)

        w_t_s = jnp.pad(w_t[:, rows_p:].reshape(H_IDX, dec_b, dec_seq), ((0, 0), (0, 0), (0, Q_TILE - dec_seq)))
        attn_s = _attention(
            sample_queries(q_hm), sample_queries(qi_hm), w_t_s.reshape(H_IDX, dec_b * Q_TILE),
            keys_s_all(cache_k[l], k_new, kv_w), _values_t(keys_s_all(cache_v[l], v_new, kv_w)),
            keys_s_all(cache_kidx[l], ki_new, LANES),
            near_bias, n_batch=dec_b, n_qblocks=1, n_keys=keys_s_pad, t0_base=past, t0_step=0, topk=topk_s,
            n_queries=dec_seq)
        attn_s = attn_s.reshape(dec_b, Q_TILE, attn_w)[:, :dec_seq].reshape(rows_s, attn_w)
        attn = jnp.concatenate([attn_p, attn_s], axis=0)

        gm, vn = _gmlp(uv, gmlp_vnorm[l][None, :], ws2[l], b2[l], rows_p)
        h = _mixout(h, attn, gm, w_out2, g[3], l)
        h = _ffn(h, g[4], g[5], w2_gu, w2_dn, l)
        h = _ple(h, p_all[l], g[6], g[7], w_pg, w_pe, l)

        outs[0].append(k_new[:rows_p].reshape(n_b, seq, N_KV_HEADS, HEAD_DIM))
        outs[1].append(v_new[:rows_p].reshape(n_b, seq, N_KV_HEADS, HEAD_DIM))
        outs[2].append(ki_new[:rows_p].reshape(n_b, seq, D_IDX))
        outs[3].append(k_new[rows_p:].reshape(dec_b, dec_seq, N_KV_HEADS, HEAD_DIM))
        outs[4].append(v_new[rows_p:].reshape(dec_b, dec_seq, N_KV_HEADS, HEAD_DIM))
        outs[5].append(ki_new[rows_p:].reshape(dec_b, dec_seq, D_IDX))
        outs[6].append(vn[rows_p:].reshape(dec_b, dec_seq, gmlp_w))

    y_prompt = h[:rows_p].reshape(n_b, seq, d_model)
    y_sample = h[rows_p:].reshape(dec_b, dec_seq, d_model)
    return (y_prompt, y_sample) + tuple(jnp.stack(o) for o in outs)
```

```python
import functools
import math

import jax
import jax.numpy as jnp
from jax import lax
from jax.experimental import pallas as pl
from jax.experimental.pallas import tpu as pltpu

CHUNK = 64
N_HEADS = 8
N_KV_HEADS = 2
HEAD_DIM = 128
H_IDX = 16
D_IDX = 64
TOPK_MAX = 256
GMLP_CHUNK = 128
GMLP_GROUPS = 8
NUM_BUCKETS = 32
MAX_DISTANCE = 128
EPS = 1e-6
LOG2_E = 1.4426950408889634

LANES = 128
SUBLANES = 8
VMEM_LIMIT_BYTES = 56 * 1024 * 1024

ROW_TILE = 768
GMLP_TILE = 512
FF_TILE = 512
Q_TILE = 128
SCORE_TILE = 256
COUNT_TILE = 512
ATTN_TILE = 1024
BIG_STEP = 2
SCORE_STEP = 4
NEAR_KEYS = 2 * MAX_DISTANCE
V_ROWS = HEAD_DIM + 16

_MXU_DTYPE = jnp.bfloat16
_INT_MIN = -(2 ** 31)


def _params(*semantics):
    return pltpu.CompilerParams(dimension_semantics=semantics, vmem_limit_bytes=VMEM_LIMIT_BYTES)


def _rms(x, g):
    return x * lax.rsqrt(jnp.mean(x * x, axis=-1, keepdims=True) + EPS) * g


def _ffn_kernel(h_ref, gpre_ref, gpost_ref, wg_ref, wu_ref, wd_ref, o_ref, xn_ref, acc_ref):
    j = pl.program_id(1)

    @pl.when(j == 0)
    def _():
        xn_ref[...] = _rms(h_ref[...], gpre_ref[...]).astype(xn_ref.dtype)
        acc_ref[...] = jnp.zeros_like(acc_ref)

    xn = xn_ref[...]
    gate = jnp.dot(xn, wg_ref[...], preferred_element_type=jnp.float32)
    up = jnp.dot(xn, wu_ref[...], preferred_element_type=jnp.float32)
    act = (jax.nn.silu(gate) * up).astype(wd_ref.dtype)
    acc_ref[...] += jnp.dot(act, wd_ref[...], preferred_element_type=jnp.float32)

    @pl.when(j == pl.num_programs(1) - 1)
    def _():
        o_ref[...] = h_ref[...] + 0.5 * _rms(acc_ref[...], gpost_ref[...])


def _ffn(h, g_pre, g_post, w_gu, w_down, layer):
    rows, d = h.shape
    d_ff = w_down.shape[1]
    nf = d_ff // FF_TILE
    assert rows % ROW_TILE == 0 and d_ff % FF_TILE == 0
    return pl.pallas_call(
        _ffn_kernel,
        out_shape=jax.ShapeDtypeStruct((rows, d), jnp.float32),
        grid=(rows // ROW_TILE, nf),
        in_specs=[
            pl.BlockSpec((ROW_TILE, d), lambda i, j: (i, 0)),
            pl.BlockSpec((1, d), lambda i, j: (0, 0)),
            pl.BlockSpec((1, d), lambda i, j: (0, 0)),
            pl.BlockSpec((None, d, FF_TILE), lambda i, j: (layer, 0, j)),
            pl.BlockSpec((None, d, FF_TILE), lambda i, j: (layer, 0, j + nf)),
            pl.BlockSpec((None, FF_TILE, d), lambda i, j: (layer, j, 0)),
        ],
        out_specs=pl.BlockSpec((ROW_TILE, d), lambda i, j: (i, 0)),
        scratch_shapes=[pltpu.VMEM((ROW_TILE, d), _MXU_DTYPE), pltpu.VMEM((ROW_TILE, d), jnp.float32)],
        compiler_params=_params("parallel", "arbitrary"),
        name="ffn",
    )(h, g_pre, g_post, w_gu, w_gu, w_down)


def _proj_kernel(h_ref, g_ref, w_ref, o_ref, *, head_major, out_scale):
    xn = _rms(h_ref[...], g_ref[...]).astype(w_ref.dtype)
    z = jnp.dot(xn, w_ref[...], preferred_element_type=jnp.float32)
    if out_scale != 1.0:
        z = z * out_scale
    if head_major:
        for hh in range(o_ref.shape[0]):
            o_ref[hh] = z[:, hh * LANES:(hh + 1) * LANES].astype(o_ref.dtype)
    else:
        o_ref[...] = z.astype(o_ref.dtype)


def _proj(h, g, w, out_dtype, head_major, name, out_scale=1.0):
    rows, d = h.shape
    n = w.shape[1]
    if head_major:
        out_shape = jax.ShapeDtypeStruct((n // LANES, rows, LANES), out_dtype)
        out_spec = pl.BlockSpec((n // LANES, ROW_TILE, LANES), lambda i: (0, i, 0))
    else:
        out_shape = jax.ShapeDtypeStruct((rows, n), out_dtype)
        out_spec = pl.BlockSpec((ROW_TILE, n), lambda i: (i, 0))
    return pl.pallas_call(
        functools.partial(_proj_kernel, head_major=head_major, out_scale=out_scale),
        out_shape=out_shape,
        grid=(rows // ROW_TILE,),
        in_specs=[
            pl.BlockSpec((ROW_TILE, d), lambda i: (i, 0)),
            pl.BlockSpec((1, d), lambda i: (0, 0)),
            pl.BlockSpec((d, n), lambda i: (0, 0)),
        ],
        out_specs=out_spec,
        compiler_params=_params("parallel"),
        name=name,
    )(h, g, w)


def _proj_kv_kernel(h_ref, g_ref, w_ref, kf_ref, vf_ref, kif_ref, k_ref, ki_ref, vt_ref, wt_ref):
    kv_w = N_KV_HEADS * HEAD_DIM
    xn = _rms(h_ref[...], g_ref[...]).astype(w_ref.dtype)
    z = jnp.dot(xn, w_ref[...], preferred_element_type=jnp.float32)
    kf_ref[...] = z[:, :kv_w]
    vf_ref[...] = z[:, kv_w:2 * kv_w]
    k_ref[...] = z[:, :kv_w].astype(k_ref.dtype)
    tail = z[:, 2 * kv_w:]
    kif_ref[...] = tail[:, :D_IDX]
    lane = lax.broadcasted_iota(jnp.int32, tail.shape, 1)
    ki_ref[...] = jnp.where(lane < D_IDX, tail, 0.0).astype(ki_ref.dtype)
    wt_ref[...] = tail.T[D_IDX:D_IDX + H_IDX, :]
    row = lax.broadcasted_iota(jnp.int32, (V_ROWS - HEAD_DIM, z.shape[0]), 0)
    for g in range(N_KV_HEADS):
        v_g = z[:, kv_w + g * HEAD_DIM:kv_w + (g + 1) * HEAD_DIM]
        vt_ref[g, 0:HEAD_DIM, :] = v_g.T.astype(vt_ref.dtype)
        vt_ref[g, HEAD_DIM:V_ROWS, :] = jnp.where(row == 0, 1.0, 0.0).astype(vt_ref.dtype)


def _proj_kv(h, g, w):
    rows, d = h.shape
    n = w.shape[1]
    kv_w = N_KV_HEADS * HEAD_DIM
    assert n == 2 * kv_w + LANES
    return pl.pallas_call(
        _proj_kv_kernel,
        out_shape=(jax.ShapeDtypeStruct((rows, kv_w), jnp.float32),
                   jax.ShapeDtypeStruct((rows, kv_w), jnp.float32),
                   jax.ShapeDtypeStruct((rows, D_IDX), jnp.float32),
                   jax.ShapeDtypeStruct((rows, kv_w), _MXU_DTYPE),
                   jax.ShapeDtypeStruct((rows, LANES), _MXU_DTYPE),
                   jax.ShapeDtypeStruct((N_KV_HEADS, V_ROWS, rows), _MXU_DTYPE),
                   jax.ShapeDtypeStruct((H_IDX, rows), jnp.float32)),
        grid=(rows // ROW_TILE,),
        in_specs=[
            pl.BlockSpec((ROW_TILE, d), lambda i: (i, 0)),
            pl.BlockSpec((1, d), lambda i: (0, 0)),
            pl.BlockSpec((d, n), lambda i: (0, 0)),
        ],
        out_specs=(pl.BlockSpec((ROW_TILE, kv_w), lambda i: (i, 0)),
                   pl.BlockSpec((ROW_TILE, kv_w), lambda i: (i, 0)),
                   pl.BlockSpec((ROW_TILE, D_IDX), lambda i: (i, 0)),
                   pl.BlockSpec((ROW_TILE, kv_w), lambda i: (i, 0)),
                   pl.BlockSpec((ROW_TILE, LANES), lambda i: (i, 0)),
                   pl.BlockSpec((N_KV_HEADS, V_ROWS, ROW_TILE), lambda i: (0, 0, i)),
                   pl.BlockSpec((H_IDX, ROW_TILE), lambda i: (0, i))),
        compiler_params=_params("parallel"),
        name="proj_kv",
    )(h, g, w)


def _gmlp_kernel(uv_ref, gain_ref, ws_ref, b_ref, o_ref, vn_ref):
    width = o_ref.shape[1]
    gdim = width // GMLP_GROUPS
    for c in range(uv_ref.shape[0] // GMLP_CHUNK):
        rs = slice(c * GMLP_CHUNK, (c + 1) * GMLP_CHUNK)
        u = jax.nn.gelu(uv_ref[rs, :width])
        vn = _rms(jax.nn.gelu(uv_ref[rs, width:]), gain_ref[...])
        vn_ref[rs, :] = vn
        vnb = vn.astype(ws_ref.dtype)
        for g in range(GMLP_GROUPS):
            cs = slice(g * gdim, (g + 1) * gdim)
            s = jnp.dot(ws_ref[g], vnb[:, cs], preferred_element_type=jnp.float32) + b_ref[:, cs]
            o_ref[rs, cs] = (u[:, cs] * s).astype(o_ref.dtype)


def _gmlp(uv, gain, ws2, b2, n_prompt_rows):
    rows = uv.shape[0]
    width = uv.shape[1] // 2
    tile = GMLP_TILE
    n_prompt_tiles = n_prompt_rows // tile

    def sel(i):
        return jnp.where(i >= n_prompt_tiles, 1, 0)

    return pl.pallas_call(
        _gmlp_kernel,
        out_shape=(jax.ShapeDtypeStruct((rows, width), _MXU_DTYPE),
                   jax.ShapeDtypeStruct((rows, width), jnp.float32)),
        grid=(rows // tile,),
        in_specs=[
            pl.BlockSpec((tile, 2 * width), lambda i: (i, 0)),
            pl.BlockSpec((1, width), lambda i: (0, 0)),
            pl.BlockSpec((None, GMLP_GROUPS, GMLP_CHUNK, GMLP_CHUNK), lambda i: (sel(i), 0, 0, 0)),
            pl.BlockSpec((None, GMLP_CHUNK, width), lambda i: (sel(i), 0, 0)),
        ],
        out_specs=(pl.BlockSpec((tile, width), lambda i: (i, 0)),
                   pl.BlockSpec((tile, width), lambda i: (i, 0))),
        compiler_params=_params("parallel"),
        name="gmlp",
    )(uv, gain, ws2, b2)


def _mixout_kernel(h_ref, a_ref, m_ref, w_ref, g_ref, o_ref):
    y = jnp.dot(a_ref[...], w_ref[0], preferred_element_type=jnp.float32)
    y = y + jnp.dot(m_ref[...], w_ref[1], preferred_element_type=jnp.float32)
    o_ref[...] = h_ref[...] + _rms(y, g_ref[...])


def _mixout(h, attn, gm, w_out2, g, layer):
    rows, d = h.shape
    half = attn.shape[1]
    return pl.pallas_call(
        _mixout_kernel,
        out_shape=jax.ShapeDtypeStruct((rows, d), jnp.float32),
        grid=(rows // ROW_TILE,),
        in_specs=[
            pl.BlockSpec((ROW_TILE, d), lambda i: (i, 0)),
            pl.BlockSpec((ROW_TILE, half), lambda i: (i, 0)),
            pl.BlockSpec((ROW_TILE, half), lambda i: (i, 0)),
            pl.BlockSpec((None, 2, half, d), lambda i: (layer, 0, 0, 0)),
            pl.BlockSpec((1, d), lambda i: (0, 0)),
        ],
        out_specs=pl.BlockSpec((ROW_TILE, d), lambda i: (i, 0)),
        compiler_params=_params("parallel"),
        name="mixout",
    )(h, attn, gm, w_out2, g)


def _ple_kernel(h_ref, p_ref, gpre_ref, gpost_ref, wpg_ref, wpe_ref, o_ref):
    h = h_ref[...]
    xn = _rms(h, gpre_ref[...]).astype(wpg_ref.dtype)
    gate = jax.nn.sigmoid(jnp.dot(xn, wpg_ref[...], preferred_element_type=jnp.float32))
    emb = jnp.dot(p_ref[...].astype(wpe_ref.dtype), wpe_ref[...], preferred_element_type=jnp.float32)
    o_ref[...] = h + _rms(gate * emb, gpost_ref[...])


def _ple(h, p, g_pre, g_post, w_pg, w_pe, layer):
    rows, d = h.shape
    d_ple = p.shape[1]
    return pl.pallas_call(
        _ple_kernel,
        out_shape=jax.ShapeDtypeStruct((rows, d), jnp.float32),
        grid=(rows // ROW_TILE,),
        in_specs=[
            pl.BlockSpec((ROW_TILE, d), lambda i: (i, 0)),
            pl.BlockSpec((ROW_TILE, d_ple), lambda i: (i, 0)),
            pl.BlockSpec((1, d), lambda i: (0, 0)),
            pl.BlockSpec((1, d), lambda i: (0, 0)),
            pl.BlockSpec((None, d, d), lambda i: (layer, 0, 0)),
            pl.BlockSpec((None, d_ple, d), lambda i: (layer, 0, 0)),
        ],
        out_specs=pl.BlockSpec((ROW_TILE, d), lambda i: (i, 0)),
        compiler_params=_params("parallel"),
        name="ple",
    )(h, p, g_pre, g_post, w_pg, w_pe)


def _attn_kernel(q_ref, qi_ref, w_ref, k_ref, vt_ref, ki_ref, bias_ref, o_ref,
                 keys_ref, m_ref, l_ref, acc_ref, s_ref, p_ref, *, t0_base, t0_step, topk, idx_scale, pos_bits,
                 n_queries):
    f32 = jnp.float32
    tq = Q_TILE
    heads_per_kv = N_HEADS // N_KV_HEADS
    nt = (((1,), (1,)), ((), ()))
    t0 = t0_base + pl.program_id(1) * t0_step
    w0 = jnp.maximum(t0 - MAX_DISTANCE, 0)

    def loop(n, fn):
        lax.fori_loop(0, n, lambda j, c: (fn(j), c)[1], 0)

    def reduce_keys(x, op):
        x = op(x.reshape(-1, SUBLANES * SUBLANES, tq), axis=0)
        x = op(x.reshape(SUBLANES, SUBLANES, tq), axis=0)
        return op(x, axis=0, keepdims=True)

    def score_rows(r0, near):
        r0 = pl.multiple_of(r0, MAX_DISTANCE)
        qs = qi_ref[...].reshape(H_IDX * tq, LANES)
        d = lax.dot_general(ki_ref[pl.ds(r0, SCORE_TILE), :], qs, nt, preferred_element_type=f32)
        score = jnp.zeros((SCORE_TILE, tq), f32)
        for h in range(H_IDX):
            score = score + (w_ref[h:h + 1, :] * idx_scale) * jnp.maximum(d[:, h * tq:(h + 1) * tq], 0.0)
        bits = lax.bitcast_convert_type(score, jnp.int32)
        key = bits ^ ((bits >> 31) & 0x7FFFFFFF)
        if near:
            s_pos = r0 + lax.broadcasted_iota(jnp.int32, (SCORE_TILE, tq), 0)
            t_pos = t0 + lax.broadcasted_iota(jnp.int32, (SCORE_TILE, tq), 1)
            key = jnp.where((s_pos // CHUNK) <= (t_pos // CHUNK), key, _INT_MIN)
        keys_ref[pl.ds(r0, SCORE_TILE), :] = key

    def score_step(r0, n_tiles):
        for i in range(n_tiles):
            score_rows(r0 + i * SCORE_TILE, False)

    n_big = w0 // (BIG_STEP * ATTN_TILE)
    far_rest = n_big * (BIG_STEP * ATTN_TILE)
    n_small = (w0 - far_rest + ATTN_TILE - 1) // ATTN_TILE
    loop(n_big, lambda j: score_step(j * (BIG_STEP * ATTN_TILE), BIG_STEP * SCORE_STEP))
    loop(n_small, lambda j: score_step(far_rest + j * ATTN_TILE, SCORE_STEP))
    score_rows(w0, True)
    pad_rows = COUNT_TILE - MAX_DISTANCE
    pad = pl.ds(pl.multiple_of(w0 + NEAR_KEYS, MAX_DISTANCE), pad_rows)
    keys_ref[pad, :] = jnp.full((pad_rows, tq), _INT_MIN, jnp.int32)

    n_count = (w0 + NEAR_KEYS + COUNT_TILE - 1) // COUNT_TILE
    wide = SUBLANES * SUBLANES

    def count_ge(cand):
        cand = jnp.concatenate([cand] * (wide // SUBLANES), axis=0)

        def body(j, cnt):
            k = keys_ref[pl.ds(pl.multiple_of(j * COUNT_TILE, COUNT_TILE), COUNT_TILE), :]
            k = k.reshape(COUNT_TILE // wide, wide, tq)
            for i in range(COUNT_TILE // wide):
                cnt = jnp.where(k[i] >= cand, cnt + 1.0, cnt)
            return cnt

        cnt = lax.fori_loop(0, n_count, body, jnp.zeros((wide, tq), f32))
        return jnp.broadcast_to(reduce_keys(cnt, jnp.sum), (SUBLANES, tq))

    def search_step(step, state):
        thr_u, at_thr = state
        cand_u = thr_u | jnp.left_shift(jnp.int32(1), 31 - step)
        cnt = count_ge(cand_u ^ _INT_MIN)
        take = cnt >= topk
        return jnp.where(take, cand_u, thr_u), jnp.where(take, cnt, at_thr)

    q_chunk = (t0 + lax.broadcasted_iota(jnp.int32, (SUBLANES, tq), 1)) // CHUNK
    n_admissible = jnp.minimum((q_chunk + 1) * CHUNK, w0 + NEAR_KEYS).astype(f32)
    blind_steps = 22
    state = lax.fori_loop(0, blind_steps, search_step, (jnp.zeros((SUBLANES, tq), jnp.int32), n_admissible))

    steps_per_test = 2
    real_query = lax.broadcasted_iota(jnp.int32, (SUBLANES, tq), 1) < n_queries

    def unsettled(state):
        step, _, at_thr = state
        settled = jnp.logical_or(jnp.logical_or(at_thr == topk, n_admissible <= topk), jnp.logical_not(real_query))
        settled = jnp.where(settled, 1.0, 0.0)
        return jnp.logical_and(step < 32, jnp.sum(settled) < float(SUBLANES * tq))

    def tested_steps(state):
        step, thr_u, at_thr = state
        for i in range(steps_per_test):
            thr_u, at_thr = search_step(step + i, (thr_u, at_thr))
        return step + steps_per_test, thr_u, at_thr

    _, thr_u, at_thr = lax.while_loop(unsettled, tested_steps, (jnp.int32(blind_steps),) + state)
    thr_s = thr_u ^ _INT_MIN
    thr = jnp.maximum(thr_s, _INT_MIN + 1)[0:1]

    tied = jnp.logical_and(jnp.logical_and(at_thr > topk, n_admissible > topk), real_query)

    @pl.when(jnp.sum(jnp.where(tied, 1.0, 0.0)) > 0.0)
    def _break_ties():
        int_max = -(_INT_MIN + 1)
        is_max = thr_s == int_max
        above = jnp.where(is_max, 0.0, count_ge(jnp.where(is_max, thr_s, thr_s + 1)))
        need = topk - above
        thr_w = jnp.concatenate([thr_s] * (wide // SUBLANES), axis=0)

        def tile_hits(j, cut):
            rows = pl.ds(pl.multiple_of(j * COUNT_TILE, COUNT_TILE), COUNT_TILE)
            k = keys_ref[rows, :].reshape(COUNT_TILE // wide, wide, tq)
            pos = j * COUNT_TILE + lax.broadcasted_iota(jnp.int32, (COUNT_TILE, tq), 0)
            pos = pos.reshape(COUNT_TILE // wide, wide, tq)
            return rows, k, jnp.logical_and(k == thr_w[None], pos < cut[None])

        def tied_before(cut):
            cut = jnp.concatenate([cut] * (wide // SUBLANES), axis=0)

            def body(j, cnt):
                _, _, hit = tile_hits(j, cut)
                return cnt + jnp.sum(jnp.where(hit, 1.0, 0.0), axis=0)

            cnt = lax.fori_loop(0, n_count, body, jnp.zeros((wide, tq), f32))
            return jnp.broadcast_to(reduce_keys(cnt, jnp.sum), (SUBLANES, tq))

        def cut_step(step, last):
            cand = last | jnp.left_shift(jnp.int32(1), pos_bits - 1 - step)
            return jnp.where(tied_before(cand) < need, cand, last)

        last = lax.fori_loop(0, pos_bits, cut_step, jnp.zeros((SUBLANES, tq), jnp.int32))
        keep_before = jnp.where(tied, last + 1, int_max)
        keep_before = jnp.concatenate([keep_before] * (wide // SUBLANES), axis=0)

        def drop(j):
            rows, k, hit = tile_hits(j, keep_before)
            surplus = jnp.logical_and(k == thr_w[None], jnp.logical_not(hit))
            keys_ref[rows, :] = jnp.where(surplus, _INT_MIN, k).reshape(COUNT_TILE, tq)

        loop(n_count, drop)

    m_ref[...] = jnp.full(m_ref.shape, -1e30, f32)
    l_ref[...] = jnp.zeros(l_ref.shape, f32)
    acc_ref[...] = jnp.zeros(acc_ref.shape, f32)

    def attend(r0, nk, near):
        r0 = pl.multiple_of(r0, MAX_DISTANCE)
        rows = pl.ds(r0, nk)
        selected = keys_ref[rows, :] >= thr
        mask_bias = jnp.where(selected, 0.0, -jnp.inf)
        if not near:
            pos = r0 + lax.broadcasted_iota(jnp.int32, (nk, tq), 0)
            mask_bias = jnp.where(pos < w0, mask_bias, -jnp.inf)
        def logits(g):
            cs = slice(g * HEAD_DIM, (g + 1) * HEAD_DIM)
            qg = q_ref[g * heads_per_kv:(g + 1) * heads_per_kv].reshape(heads_per_kv * tq, HEAD_DIM)
            s_all = lax.dot_general(k_ref[rows, cs], qg, nt, preferred_element_type=f32)
            for r in range(heads_per_kv):
                h = g * heads_per_kv + r
                s = s_all[:, r * tq:(r + 1) * tq] + mask_bias
                if near:
                    s = s + bias_ref[jnp.where(t0 == w0, 1, 0), h]
                s_ref[h, 0:nk, :] = s

        def probabilities(g):
            alphas = []
            for r in range(heads_per_kv):
                h = g * heads_per_kv + r
                m_old = m_ref[h]
                m_new = jnp.maximum(m_old, reduce_keys(s_ref[h, 0:nk, :], jnp.max))
                alphas.append(jnp.exp2(m_old - m_new))
                p = jnp.exp2(s_ref[h, 0:nk, :] - m_new[0:1])
                m_ref[h] = m_new
                p_ref[g, 0:nk, r * tq:(r + 1) * tq] = p.astype(p_ref.dtype)
            return alphas

        def accumulate(g, alphas):
            pv = jnp.dot(vt_ref[g, :, rows], p_ref[g, 0:nk, :], preferred_element_type=f32)
            for r in range(heads_per_kv):
                h = g * heads_per_kv + r
                qs_ = slice(r * tq, (r + 1) * tq)
                acc_ref[h] = alphas[r][0:1] * acc_ref[h] + pv[0:HEAD_DIM, qs_]
                l_ref[h] = alphas[r] * l_ref[h] + pv[HEAD_DIM:HEAD_DIM + 1, qs_]

        logits(0)
        alphas = probabilities(0)
        for g in range(1, N_KV_HEADS):
            logits(g)
            accumulate(g - 1, alphas)
            alphas = probabilities(g)
        accumulate(N_KV_HEADS - 1, alphas)

    loop(n_big, lambda j: attend(j * (BIG_STEP * ATTN_TILE), BIG_STEP * ATTN_TILE, False))
    loop(n_small, lambda j: attend(far_rest + j * ATTN_TILE, ATTN_TILE, False))
    attend(w0, NEAR_KEYS, True)

    for h in range(N_HEADS):
        out_t = acc_ref[h] / l_ref[h][0:1]
        o_ref[:, h * HEAD_DIM:(h + 1) * HEAD_DIM] = out_t.T.astype(o_ref.dtype)


def _attention(q_hm, qi_hm, w_t, k_all, vt_all, ki_all, bias_tiles, *, n_batch, n_qblocks, n_keys, t0_base, t0_step,
               topk, n_queries=Q_TILE):
    tq = Q_TILE
    kv_w = N_KV_HEADS * HEAD_DIM
    once = pl.Buffered(1)
    if k_all.ndim == 3:
        key_specs = [
            pl.BlockSpec((None, n_keys, kv_w), lambda b, i: (b, 0, 0), pipeline_mode=once),
            pl.BlockSpec((None, N_KV_HEADS, V_ROWS, n_keys), lambda b, i: (b, 0, 0, 0), pipeline_mode=once),
            pl.BlockSpec((None, n_keys, LANES), lambda b, i: (b, 0, 0), pipeline_mode=once),
        ]
    else:
        key_specs = [
            pl.BlockSpec((n_keys, kv_w), lambda b, i: (b, 0), pipeline_mode=once),
            pl.BlockSpec((N_KV_HEADS, V_ROWS, n_keys), lambda b, i: (0, 0, b), pipeline_mode=once),
            pl.BlockSpec((n_keys, LANES), lambda b, i: (b, 0), pipeline_mode=once),
        ]
    assert n_keys % COUNT_TILE == 0 and tq == MAX_DISTANCE and NEAR_KEYS == SCORE_TILE
    assert ATTN_TILE % COUNT_TILE == 0 and ATTN_TILE == SCORE_STEP * SCORE_TILE
    assert t0_base % MAX_DISTANCE == 0 and t0_step % MAX_DISTANCE == 0
    last_w0 = max(t0_base + (n_qblocks - 1) * t0_step - MAX_DISTANCE, 0)
    assert -(-last_w0 // ATTN_TILE) * ATTN_TILE <= n_keys and last_w0 + NEAR_KEYS <= n_keys
    kernel = functools.partial(_attn_kernel, t0_base=t0_base, t0_step=t0_step, topk=float(topk),
                               idx_scale=(H_IDX * D_IDX) ** -0.5, pos_bits=(n_keys + COUNT_TILE).bit_length(),
                               n_queries=n_queries)
    return pl.pallas_call(
        kernel,
        out_shape=jax.ShapeDtypeStruct((n_batch * n_qblocks * tq, N_HEADS * HEAD_DIM), _MXU_DTYPE),
        grid=(n_batch, n_qblocks),
        in_specs=[
            pl.BlockSpec((N_HEADS, tq, HEAD_DIM), lambda b, i: (0, b * n_qblocks + i, 0)),
            pl.BlockSpec((H_IDX, tq, LANES), lambda b, i: (0, b * n_qblocks + i, 0)),
            pl.BlockSpec((H_IDX, tq), lambda b, i: (0, b * n_qblocks + i)),
            *key_specs,
            pl.BlockSpec((2, N_HEADS, NEAR_KEYS, tq), lambda b, i: (0, 0, 0, 0), pipeline_mode=once),
        ],
        out_specs=pl.BlockSpec((tq, N_HEADS * HEAD_DIM), lambda b, i: (b * n_qblocks + i, 0)),
        scratch_shapes=[
            pltpu.VMEM((n_keys + COUNT_TILE, tq), jnp.int32),
            pltpu.VMEM((N_HEADS, SUBLANES, tq), jnp.float32),
            pltpu.VMEM((N_HEADS, SUBLANES, tq), jnp.float32),
            pltpu.VMEM((N_HEADS, HEAD_DIM, tq), jnp.float32),
            pltpu.VMEM((N_HEADS, BIG_STEP * ATTN_TILE, tq), jnp.float32),
            pltpu.VMEM((N_KV_HEADS, BIG_STEP * ATTN_TILE, (N_HEADS // N_KV_HEADS) * tq), _MXU_DTYPE),
        ],
        compiler_params=_params("parallel", "arbitrary"),
        name="dsa_attention",
    )(q_hm, qi_hm, w_t, k_all, vt_all, ki_all, bias_tiles)


def _t5_bucket(rel):
    nb = NUM_BUCKETS // 2
    max_exact = nb // 2
    n = jnp.abs(rel)
    nf = jnp.maximum(n, max_exact).astype(jnp.float32)
    large = max_exact + (jnp.log(nf / max_exact) / math.log(MAX_DISTANCE / max_exact) * (nb - max_exact)).astype(jnp.int32)
    large = jnp.minimum(large, nb - 1)
    return jnp.where(rel > 0, nb, 0) + jnp.where(n < max_exact, n, large)


def _near_bias(rel_bias):
    d = jnp.array([-MAX_DISTANCE, 0], jnp.int32)[:, None, None]
    rel = d + jnp.arange(NEAR_KEYS, dtype=jnp.int32)[None, :, None] - jnp.arange(Q_TILE, dtype=jnp.int32)[None, None, :]
    hit = _t5_bucket(rel)[:, None, :, :, None] == jnp.arange(NUM_BUCKETS, dtype=jnp.int32)
    near = jnp.sum(jnp.where(hit, rel_bias.T[None, :, None, None, :], 0.0), axis=-1)
    far = rel_bias[_t5_bucket(jnp.int32(-MAX_DISTANCE))]
    return ((near - far[None, :, None, None]) * LOG2_E).astype(jnp.float32)


def _values_t(v):
    n, keys, _ = v.shape
    vt = jnp.transpose(v.reshape(n, keys, N_KV_HEADS, HEAD_DIM), (0, 2, 3, 1))
    ones = jnp.ones((n, N_KV_HEADS, 1, keys), v.dtype)
    zeros = jnp.zeros((n, N_KV_HEADS, V_ROWS - HEAD_DIM - 1, keys), v.dtype)
    return jnp.concatenate([vt, ones, zeros], axis=2)


def _pad_cols(w, block, width):
    lead = w.shape[:-1]
    n = w.shape[-1] // block
    w = w.reshape(*lead, n, block)
    w = jnp.pad(w, [(0, 0)] * len(lead) + [(0, 0), (0, width - block)])
    return w.reshape(*lead, n * width)


def kernel(x_prompt, x_sample, p_prompt, p_sample, cache_k, cache_v, cache_kidx, norm_g, w_ffn1_gu, w_ffn1_down,
           w_in, w_out, gmlp_ws, gmlp_b, gmlp_vnorm, w_ffn2_gu, w_ffn2_down, w_ple_gate, w_ple, rel_bias):
    depth = w_in.shape[0]
    n_b, seq, d_model = x_prompt.shape
    dec_b, dec_seq, _ = x_sample.shape
    past = cache_k.shape[2]
    rows_p, rows_s = n_b * seq, dec_b * dec_seq
    attn_w = N_HEADS * HEAD_DIM
    kv_w = N_KV_HEADS * HEAD_DIM
    idx_w = H_IDX * D_IDX
    gmlp_w = d_model - attn_w
    mx = _MXU_DTYPE
    assert seq % ATTN_TILE == 0 and dec_seq <= CHUNK and past % MAX_DISTANCE == 0 and dec_seq % SUBLANES == 0
    assert past >= MAX_DISTANCE
    assert rows_p % GMLP_TILE == 0 and rows_s % GMLP_TILE == 0 and GMLP_CHUNK % dec_seq == 0
    assert GMLP_TILE % GMLP_CHUNK == 0 and (rows_p + rows_s) % ROW_TILE == 0

    o1 = attn_w
    o2 = o1 + kv_w
    o3 = o2 + kv_w
    o4 = o3 + idx_w
    o6 = o4 + D_IDX + H_IDX
    w_q = w_in[:, :, :o1].astype(mx)
    w_qi = _pad_cols(w_in[:, :, o3:o4], D_IDX, LANES).astype(mx)
    w_kvki = w_in[:, :, o1:o3]
    w_kvki = jnp.concatenate([w_kvki, w_in[:, :, o4:o6]], axis=-1)
    w_kvki = jnp.pad(w_kvki, ((0, 0), (0, 0), (0, 2 * kv_w + LANES - w_kvki.shape[-1]))).astype(mx)
    w_uv = w_in[:, :, o6:].astype(mx)
    w1_gu, w1_dn = w_ffn1_gu.astype(mx), w_ffn1_down.astype(mx)
    w2_gu, w2_dn = w_ffn2_gu.astype(mx), w_ffn2_down.astype(mx)
    w_out2 = w_out.astype(mx).reshape(depth, 2, attn_w, d_model)
    w_pg, w_pe = w_ple_gate.astype(mx), w_ple.astype(mx)

    causal = jnp.tril(jnp.ones((GMLP_CHUNK, GMLP_CHUNK), bool))
    ws_full = jnp.where(causal, gmlp_ws, 0.0)
    reps = GMLP_CHUNK // dec_seq
    small = jnp.where(causal[:dec_seq, :dec_seq], gmlp_ws[:, :, :dec_seq, :dec_seq], 0.0)
    eye = jnp.eye(reps, dtype=gmlp_ws.dtype)
    ws_small = jnp.einsum('ab,lgij->lgaibj', eye, small).reshape(depth, GMLP_GROUPS, GMLP_CHUNK, GMLP_CHUNK)
    ws2 = jnp.stack([ws_full, ws_small], axis=1).astype(mx)
    b_small = jnp.tile(gmlp_b[:, :, :dec_seq], (1, 1, reps))
    b2 = jnp.stack([gmlp_b, b_small], axis=1)
    b2 = jnp.repeat(jnp.swapaxes(b2, 2, 3), gmlp_w // GMLP_GROUPS, axis=-1)

    near_bias = _near_bias(rel_bias)

    h = jnp.concatenate([x_prompt.reshape(rows_p, d_model), x_sample.reshape(rows_s, d_model)], axis=0)
    p_all = jnp.concatenate([p_prompt.reshape(depth, rows_p, -1), p_sample.reshape(depth, rows_s, -1)], axis=1)

    keys_s = past + dec_seq
    w0_s = past - MAX_DISTANCE
    keys_s_pad = max(-(-(w0_s + NEAR_KEYS) // COUNT_TILE) * COUNT_TILE, -(-w0_s // ATTN_TILE) * ATTN_TILE)
    assert keys_s_pad >= keys_s
    topk_p = min(TOPK_MAX, seq // 4)
    topk_s = min(TOPK_MAX, keys_s // 4)

    def sample_queries(x_hm):
        x = x_hm[:, rows_p:].reshape(x_hm.shape[0], dec_b, dec_seq, x_hm.shape[-1])
        x = jnp.pad(x, ((0, 0), (0, 0), (0, Q_TILE - dec_seq), (0, 0)))
        return x.reshape(x_hm.shape[0], dec_b * Q_TILE, x_hm.shape[-1])

    outs = [[] for _ in range(7)]
    for l in range(depth):
        g = norm_g[l][:, None, :]
        h = _ffn(h, g[0], g[1], w1_gu, w1_dn, l)

        q_hm = _proj(h, g[2], w_q[l], mx, True, "proj_q", out_scale=HEAD_DIM ** -0.5 * LOG2_E)
        qi_hm = _proj(h, g[2], w_qi[l], mx, True, "proj_qi")
        k_new, v_new, ki_new, k_mx, ki_mx, vt_mx, w_t = _proj_kv(h, g[2], w_kvki[l])
        uv = _proj(h, g[2], w_uv[l], jnp.float32, False, "proj_uv")

        attn_p = _attention(
            q_hm, qi_hm, w_t, k_mx, vt_mx, ki_mx, near_bias,
            n_batch=n_b, n_qblocks=seq // Q_TILE, n_keys=seq, t0_base=0, t0_step=Q_TILE, topk=topk_p)

        def keys_s_all(cache, new, width):
            full = jnp.concatenate([cache.reshape(dec_b, past, -1), new[rows_p:].reshape(dec_b, dec_seq, -1)], axis=1)
            full = jnp.pad(full, ((0, 0), (0, keys_s_pad - keys_s), (0, width - full.shape[-1])))
            return full.astype(mx)

        w_t_s = jnp.pad(w_t[:, rows_p:].reshape(H_IDX, dec_b, dec_seq), ((0, 0), (0, 0), (0, Q_TILE - dec_seq)))
        attn_s = _attention(
            sample_queries(q_hm), sample_queries(qi_hm), w_t_s.reshape(H_IDX, dec_b * Q_TILE),
            keys_s_all(cache_k[l], k_new, kv_w), _values_t(keys_s_all(cache_v[l], v_new, kv_w)),
            keys_s_all(cache_kidx[l], ki_new, LANES),
            near_bias, n_batch=dec_b, n_qblocks=1, n_keys=keys_s_pad, t0_base=past, t0_step=0, topk=topk_s,
            n_queries=dec_seq)
        attn_s = attn_s.reshape(dec_b, Q_TILE, attn_w)[:, :dec_seq].reshape(rows_s, attn_w)
        attn = jnp.concatenate([attn_p, attn_s], axis=0)

        gm, vn = _gmlp(uv, gmlp_vnorm[l][None, :], ws2[l], b2[l], rows_p)
        h = _mixout(h, attn, gm, w_out2, g[3], l)
        h = _ffn(h, g[4], g[5], w2_gu, w2_dn, l)
        h = _ple(h, p_all[l], g[6], g[7], w_pg, w_pe, l)

        outs[0].append(k_new[:rows_p].reshape(n_b, seq, N_KV_HEADS, HEAD_DIM))
        outs[1].append(v_new[:rows_p].reshape(n_b, seq, N_KV_HEADS, HEAD_DIM))
        outs[2].append(ki_new[:rows_p].reshape(n_b, seq, D_IDX))
        outs[3].append(k_new[rows_p:].reshape(dec_b, dec_seq, N_KV_HEADS, HEAD_DIM))
        outs[4].append(v_new[rows_p:].reshape(dec_b, dec_seq, N_KV_HEADS, HEAD_DIM))
        outs[5].append(ki_new[rows_p:].reshape(dec_b, dec_seq, D_IDX))
        outs[6].append(vn[rows_p:].reshape(dec_b, dec_seq, gmlp_w))

    y_prompt = h[:rows_p].reshape(n_b, seq, d_model)
    y_sample = h[rows_p:].reshape(dec_b, dec_seq, d_model)
    return (y_prompt, y_sample) + tuple(jnp.stack(o) for o in outs)
```

```python
import functools
import math

import jax
import jax.numpy as jnp
from jax import lax
from jax.experimental import pallas as pl
from jax.experimental.pallas import tpu as pltpu

CHUNK = 64
N_HEADS = 8
N_KV_HEADS = 2
HEAD_DIM = 128
H_IDX = 16
D_IDX = 64
TOPK_MAX = 256
GMLP_CHUNK = 128
GMLP_GROUPS = 8
NUM_BUCKETS = 32
MAX_DISTANCE = 128
EPS = 1e-6
LOG2_E = 1.4426950408889634

LANES = 128
SUBLANES = 8
VMEM_LIMIT_BYTES = 56 * 1024 * 1024

ROW_TILE = 768
GMLP_TILE = 512
FF_TILE = 512
Q_TILE = 128
SCORE_TILE = 256
COUNT_TILE = 1024
ATTN_TILE = 1024
BIG_STEP = 2
SCORE_BIG_STEP = 2
SCORE_STEP = 4
NEAR_KEYS = 2 * MAX_DISTANCE
V_ROWS = HEAD_DIM + 16

_MXU_DTYPE = jnp.bfloat16
_INT_MIN = -(2 ** 31)


def _params(*semantics):
    return pltpu.CompilerParams(dimension_semantics=semantics, vmem_limit_bytes=VMEM_LIMIT_BYTES)


def _rms(x, g):
    return x * lax.rsqrt(jnp.mean(x * x, axis=-1, keepdims=True) + EPS) * g


def _ffn_kernel(h_ref, gpre_ref, gpost_ref, wg_ref, wu_ref, wd_ref, o_ref, xn_ref, acc_ref):
    j = pl.program_id(1)

    @pl.when(j == 0)
    def _():
        xn_ref[...] = _rms(h_ref[...], gpre_ref[...]).astype(xn_ref.dtype)
        acc_ref[...] = jnp.zeros_like(acc_ref)

    xn = xn_ref[...]
    gate = jnp.dot(xn, wg_ref[...], preferred_element_type=jnp.float32)
    up = jnp.dot(xn, wu_ref[...], preferred_element_type=jnp.float32)
    act = (jax.nn.silu(gate) * up).astype(wd_ref.dtype)
    acc_ref[...] += jnp.dot(act, wd_ref[...], preferred_element_type=jnp.float32)

    @pl.when(j == pl.num_programs(1) - 1)
    def _():
        o_ref[...] = h_ref[...] + 0.5 * _rms(acc_ref[...], gpost_ref[...])


def _ffn(h, g_pre, g_post, w_gu, w_down, layer):
    rows, d = h.shape
    d_ff = w_down.shape[1]
    nf = d_ff // FF_TILE
    assert rows % ROW_TILE == 0 and d_ff % FF_TILE == 0
    return pl.pallas_call(
        _ffn_kernel,
        out_shape=jax.ShapeDtypeStruct((rows, d), jnp.float32),
        grid=(rows // ROW_TILE, nf),
        in_specs=[
            pl.BlockSpec((ROW_TILE, d), lambda i, j: (i, 0)),
            pl.BlockSpec((1, d), lambda i, j: (0, 0)),
            pl.BlockSpec((1, d), lambda i, j: (0, 0)),
            pl.BlockSpec((None, d, FF_TILE), lambda i, j: (layer, 0, j)),
            pl.BlockSpec((None, d, FF_TILE), lambda i, j: (layer, 0, j + nf)),
            pl.BlockSpec((None, FF_TILE, d), lambda i, j: (layer, j, 0)),
        ],
        out_specs=pl.BlockSpec((ROW_TILE, d), lambda i, j: (i, 0)),
        scratch_shapes=[pltpu.VMEM((ROW_TILE, d), _MXU_DTYPE), pltpu.VMEM((ROW_TILE, d), jnp.float32)],
        compiler_params=_params("parallel", "arbitrary"),
        name="ffn",
    )(h, g_pre, g_post, w_gu, w_gu, w_down)


def _proj_kernel(h_ref, g_ref, w_ref, o_ref, *, head_major, out_scale):
    xn = _rms(h_ref[...], g_ref[...]).astype(w_ref.dtype)
    z = jnp.dot(xn, w_ref[...], preferred_element_type=jnp.float32)
    if out_scale != 1.0:
        z = z * out_scale
    if head_major:
        for hh in range(o_ref.shape[0]):
            o_ref[hh] = z[:, hh * LANES:(hh + 1) * LANES].astype(o_ref.dtype)
    else:
        o_ref[...] = z.astype(o_ref.dtype)


def _proj(h, g, w, out_dtype, head_major, name, out_scale=1.0):
    rows, d = h.shape
    n = w.shape[1]
    if head_major:
        out_shape = jax.ShapeDtypeStruct((n // LANES, rows, LANES), out_dtype)
        out_spec = pl.BlockSpec((n // LANES, ROW_TILE, LANES), lambda i: (0, i, 0))
    else:
        out_shape = jax.ShapeDtypeStruct((rows, n), out_dtype)
        out_spec = pl.BlockSpec((ROW_TILE, n), lambda i: (i, 0))
    return pl.pallas_call(
        functools.partial(_proj_kernel, head_major=head_major, out_scale=out_scale),
        out_shape=out_shape,
        grid=(rows // ROW_TILE,),
        in_specs=[
            pl.BlockSpec((ROW_TILE, d), lambda i: (i, 0)),
            pl.BlockSpec((1, d), lambda i: (0, 0)),
            pl.BlockSpec((d, n), lambda i: (0, 0)),
        ],
        out_specs=out_spec,
        compiler_params=_params("parallel"),
        name=name,
    )(h, g, w)


def _proj_kv_kernel(h_ref, g_ref, w_ref, kf_ref, vf_ref, kif_ref, k_ref, ki_ref, vt_ref, wt_ref):
    kv_w = N_KV_HEADS * HEAD_DIM
    xn = _rms(h_ref[...], g_ref[...]).astype(w_ref.dtype)
    z = jnp.dot(xn, w_ref[...], preferred_element_type=jnp.float32)
    kf_ref[...] = z[:, :kv_w]
    vf_ref[...] = z[:, kv_w:2 * kv_w]
    k_ref[...] = z[:, :kv_w].astype(k_ref.dtype)
    tail = z[:, 2 * kv_w:]
    kif_ref[...] = tail[:, :D_IDX]
    lane = lax.broadcasted_iota(jnp.int32, tail.shape, 1)
    ki_ref[...] = jnp.where(lane < D_IDX, tail, 0.0).astype(ki_ref.dtype)
    wt_ref[...] = tail.T[D_IDX:D_IDX + H_IDX, :]
    row = lax.broadcasted_iota(jnp.int32, (V_ROWS - HEAD_DIM, z.shape[0]), 0)
    for g in range(N_KV_HEADS):
        v_g = z[:, kv_w + g * HEAD_DIM:kv_w + (g + 1) * HEAD_DIM]
        vt_ref[g, 0:HEAD_DIM, :] = v_g.T.astype(vt_ref.dtype)
        vt_ref[g, HEAD_DIM:V_ROWS, :] = jnp.where(row == 0, 1.0, 0.0).astype(vt_ref.dtype)


def _proj_kv(h, g, w):
    rows, d = h.shape
    n = w.shape[1]
    kv_w = N_KV_HEADS * HEAD_DIM
    assert n == 2 * kv_w + LANES
    return pl.pallas_call(
        _proj_kv_kernel,
        out_shape=(jax.ShapeDtypeStruct((rows, kv_w), jnp.float32),
                   jax.ShapeDtypeStruct((rows, kv_w), jnp.float32),
                   jax.ShapeDtypeStruct((rows, D_IDX), jnp.float32),
                   jax.ShapeDtypeStruct((rows, kv_w), _MXU_DTYPE),
                   jax.ShapeDtypeStruct((rows, LANES), _MXU_DTYPE),
                   jax.ShapeDtypeStruct((N_KV_HEADS, V_ROWS, rows), _MXU_DTYPE),
                   jax.ShapeDtypeStruct((H_IDX, rows), jnp.float32)),
        grid=(rows // ROW_TILE,),
        in_specs=[
            pl.BlockSpec((ROW_TILE, d), lambda i: (i, 0)),
            pl.BlockSpec((1, d), lambda i: (0, 0)),
            pl.BlockSpec((d, n), lambda i: (0, 0)),
        ],
        out_specs=(pl.BlockSpec((ROW_TILE, kv_w), lambda i: (i, 0)),
                   pl.BlockSpec((ROW_TILE, kv_w), lambda i: (i, 0)),
                   pl.BlockSpec((ROW_TILE, D_IDX), lambda i: (i, 0)),
                   pl.BlockSpec((ROW_TILE, kv_w), lambda i: (i, 0)),
                   pl.BlockSpec((ROW_TILE, LANES), lambda i: (i, 0)),
                   pl.BlockSpec((N_KV_HEADS, V_ROWS, ROW_TILE), lambda i: (0, 0, i)),
                   pl.BlockSpec((H_IDX, ROW_TILE), lambda i: (0, i))),
        compiler_params=_params("parallel"),
        name="proj_kv",
    )(h, g, w)


def _gmlp_kernel(uv_ref, gain_ref, ws_ref, b_ref, o_ref, vn_ref):
    width = o_ref.shape[1]
    gdim = width // GMLP_GROUPS
    for c in range(uv_ref.shape[0] // GMLP_CHUNK):
        rs = slice(c * GMLP_CHUNK, (c + 1) * GMLP_CHUNK)
        u = jax.nn.gelu(uv_ref[rs, :width])
        vn = _rms(jax.nn.gelu(uv_ref[rs, width:]), gain_ref[...])
        vn_ref[rs, :] = vn
        vnb = vn.astype(ws_ref.dtype)
        for g in range(GMLP_GROUPS):
            cs = slice(g * gdim, (g + 1) * gdim)
            s = jnp.dot(ws_ref[g], vnb[:, cs], preferred_element_type=jnp.float32) + b_ref[:, cs]
            o_ref[rs, cs] = (u[:, cs] * s).astype(o_ref.dtype)


def _gmlp(uv, gain, ws2, b2, n_prompt_rows):
    rows = uv.shape[0]
    width = uv.shape[1] // 2
    tile = GMLP_TILE
    n_prompt_tiles = n_prompt_rows // tile

    def sel(i):
        return jnp.where(i >= n_prompt_tiles, 1, 0)

    return pl.pallas_call(
        _gmlp_kernel,
        out_shape=(jax.ShapeDtypeStruct((rows, width), _MXU_DTYPE),
                   jax.ShapeDtypeStruct((rows, width), jnp.float32)),
        grid=(rows // tile,),
        in_specs=[
            pl.BlockSpec((tile, 2 * width), lambda i: (i, 0)),
            pl.BlockSpec((1, width), lambda i: (0, 0)),
            pl.BlockSpec((None, GMLP_GROUPS, GMLP_CHUNK, GMLP_CHUNK), lambda i: (sel(i), 0, 0, 0)),
            pl.BlockSpec((None, GMLP_CHUNK, width), lambda i: (sel(i), 0, 0)),
        ],
        out_specs=(pl.BlockSpec((tile, width), lambda i: (i, 0)),
                   pl.BlockSpec((tile, width), lambda i: (i, 0))),
        compiler_params=_params("parallel"),
        name="gmlp",
    )(uv, gain, ws2, b2)


def _mixout_kernel(h_ref, a_ref, m_ref, w_ref, g_ref, o_ref):
    y = jnp.dot(a_ref[...], w_ref[0], preferred_element_type=jnp.float32)
    y = y + jnp.dot(m_ref[...], w_ref[1], preferred_element_type=jnp.float32)
    o_ref[...] = h_ref[...] + _rms(y, g_ref[...])


def _mixout(h, attn, gm, w_out2, g, layer):
    rows, d = h.shape
    half = attn.shape[1]
    return pl.pallas_call(
        _mixout_kernel,
        out_shape=jax.ShapeDtypeStruct((rows, d), jnp.float32),
        grid=(rows // ROW_TILE,),
        in_specs=[
            pl.BlockSpec((ROW_TILE, d), lambda i: (i, 0)),
            pl.BlockSpec((ROW_TILE, half), lambda i: (i, 0)),
            pl.BlockSpec((ROW_TILE, half), lambda i: (i, 0)),
            pl.BlockSpec((None, 2, half, d), lambda i: (layer, 0, 0, 0)),
            pl.BlockSpec((1, d), lambda i: (0, 0)),
        ],
        out_specs=pl.BlockSpec((ROW_TILE, d), lambda i: (i, 0)),
        compiler_params=_params("parallel"),
        name="mixout",
    )(h, attn, gm, w_out2, g)


def _ple_kernel(h_ref, p_ref, gpre_ref, gpost_ref, wpg_ref, wpe_ref, o_ref):
    h = h_ref[...]
    xn = _rms(h, gpre_ref[...]).astype(wpg_ref.dtype)
    gate = jax.nn.sigmoid(jnp.dot(xn, wpg_ref[...], preferred_element_type=jnp.float32))
    emb = jnp.dot(p_ref[...].astype(wpe_ref.dtype), wpe_ref[...], preferred_element_type=jnp.float32)
    o_ref[...] = h + _rms(gate * emb, gpost_ref[...])


def _ple(h, p, g_pre, g_post, w_pg, w_pe, layer):
    rows, d = h.shape
    d_ple = p.shape[1]
    return pl.pallas_call(
        _ple_kernel,
        out_shape=jax.ShapeDtypeStruct((rows, d), jnp.float32),
        grid=(rows // ROW_TILE,),
        in_specs=[
            pl.BlockSpec((ROW_TILE, d), lambda i: (i, 0)),
            pl.BlockSpec((ROW_TILE, d_ple), lambda i: (i, 0)),
            pl.BlockSpec((1, d), lambda i: (0, 0)),
            pl.BlockSpec((1, d), lambda i: (0, 0)),
            pl.BlockSpec((None, d, d), lambda i: (layer, 0, 0)),
            pl.BlockSpec((None, d_ple, d), lambda i: (layer, 0, 0)),
        ],
        out_specs=pl.BlockSpec((ROW_TILE, d), lambda i: (i, 0)),
        compiler_params=_params("parallel"),
        name="ple",
    )(h, p, g_pre, g_post, w_pg, w_pe)


def _attn_kernel(q_ref, qi_ref, w_ref, k_ref, vt_ref, ki_ref, bias_ref, o_ref,
                 keys_ref, m_ref, l_ref, acc_ref, s_ref, p_ref, *, t0_base, t0_step, topk, idx_scale, pos_bits,
                 n_queries):
    f32 = jnp.float32
    tq = Q_TILE
    heads_per_kv = N_HEADS // N_KV_HEADS
    nt = (((1,), (1,)), ((), ()))
    t0 = t0_base + pl.program_id(1) * t0_step
    w0 = jnp.maximum(t0 - MAX_DISTANCE, 0)

    def loop(n, fn):
        lax.fori_loop(0, n, lambda j, c: (fn(j), c)[1], 0)

    def reduce_keys(x, op):
        x = op(x.reshape(-1, SUBLANES * SUBLANES, tq), axis=0)
        x = op(x.reshape(SUBLANES, SUBLANES, tq), axis=0)
        return op(x, axis=0, keepdims=True)

    def score_rows(r0, near):
        r0 = pl.multiple_of(r0, MAX_DISTANCE)
        qs = qi_ref[...].reshape(H_IDX * tq, LANES)
        d = lax.dot_general(ki_ref[pl.ds(r0, SCORE_TILE), :], qs, nt, preferred_element_type=f32)
        score = jnp.zeros((SCORE_TILE, tq), f32)
        for h in range(H_IDX):
            score = score + (w_ref[h:h + 1, :] * idx_scale) * jnp.maximum(d[:, h * tq:(h + 1) * tq], 0.0)
        bits = lax.bitcast_convert_type(score, jnp.int32)
        key = bits ^ ((bits >> 31) & 0x7FFFFFFF)
        if near:
            s_pos = r0 + lax.broadcasted_iota(jnp.int32, (SCORE_TILE, tq), 0)
            t_pos = t0 + lax.broadcasted_iota(jnp.int32, (SCORE_TILE, tq), 1)
            key = jnp.where((s_pos // CHUNK) <= (t_pos // CHUNK), key, _INT_MIN)
        keys_ref[pl.ds(r0, SCORE_TILE), :] = key

    def score_step(r0, n_tiles):
        for i in range(n_tiles):
            score_rows(r0 + i * SCORE_TILE, False)

    def far_steps(big):
        n_big = w0 // (big * ATTN_TILE)
        rest = n_big * (big * ATTN_TILE)
        return n_big, rest, (w0 - rest + ATTN_TILE - 1) // ATTN_TILE

    n_big, far_rest, n_small = far_steps(SCORE_BIG_STEP)
    loop(n_big, lambda j: score_step(j * (SCORE_BIG_STEP * ATTN_TILE), SCORE_BIG_STEP * SCORE_STEP))
    loop(n_small, lambda j: score_step(far_rest + j * ATTN_TILE, SCORE_STEP))
    score_rows(w0, True)
    pad_rows = COUNT_TILE - MAX_DISTANCE
    pad = pl.ds(pl.multiple_of(w0 + NEAR_KEYS, MAX_DISTANCE), pad_rows)
    keys_ref[pad, :] = jnp.full((pad_rows, tq), _INT_MIN, jnp.int32)

    n_count = (w0 + NEAR_KEYS + COUNT_TILE - 1) // COUNT_TILE
    wide = SUBLANES * SUBLANES

    def count_ge(cand):
        cand = jnp.concatenate([cand] * (wide // SUBLANES), axis=0)

        def body(j, cnt):
            k = keys_ref[pl.ds(pl.multiple_of(j * COUNT_TILE, COUNT_TILE), COUNT_TILE), :]
            k = k.reshape(COUNT_TILE // wide, wide, tq)
            for i in range(COUNT_TILE // wide):
                cnt = jnp.where(k[i] >= cand, cnt + 1.0, cnt)
            return cnt

        cnt = lax.fori_loop(0, n_count, body, jnp.zeros((wide, tq), f32))
        return jnp.broadcast_to(reduce_keys(cnt, jnp.sum), (SUBLANES, tq))

    def search_step(step, state):
        thr_u, at_thr = state
        cand_u = thr_u | jnp.left_shift(jnp.int32(1), 31 - step)
        cnt = count_ge(cand_u ^ _INT_MIN)
        take = cnt >= topk
        return jnp.where(take, cand_u, thr_u), jnp.where(take, cnt, at_thr)

    q_chunk = (t0 + lax.broadcasted_iota(jnp.int32, (SUBLANES, tq), 1)) // CHUNK
    n_admissible = jnp.minimum((q_chunk + 1) * CHUNK, w0 + NEAR_KEYS).astype(f32)
    blind_steps = 22
    state = lax.fori_loop(0, blind_steps, search_step, (jnp.zeros((SUBLANES, tq), jnp.int32), n_admissible))

    steps_per_test = 2
    real_query = lax.broadcasted_iota(jnp.int32, (SUBLANES, tq), 1) < n_queries

    def unsettled(state):
        step, _, at_thr = state
        settled = jnp.logical_or(jnp.logical_or(at_thr == topk, n_admissible <= topk), jnp.logical_not(real_query))
        settled = jnp.where(settled, 1.0, 0.0)
        return jnp.logical_and(step < 32, jnp.sum(settled) < float(SUBLANES * tq))

    def tested_steps(state):
        step, thr_u, at_thr = state
        for i in range(steps_per_test):
            thr_u, at_thr = search_step(step + i, (thr_u, at_thr))
        return step + steps_per_test, thr_u, at_thr

    _, thr_u, at_thr = lax.while_loop(unsettled, tested_steps, (jnp.int32(blind_steps),) + state)
    thr_s = thr_u ^ _INT_MIN
    thr = jnp.maximum(thr_s, _INT_MIN + 1)[0:1]

    tied = jnp.logical_and(jnp.logical_and(at_thr > topk, n_admissible > topk), real_query)

    @pl.when(jnp.sum(jnp.where(tied, 1.0, 0.0)) > 0.0)
    def _break_ties():
        int_max = -(_INT_MIN + 1)
        is_max = thr_s == int_max
        above = jnp.where(is_max, 0.0, count_ge(jnp.where(is_max, thr_s, thr_s + 1)))
        need = topk - above
        thr_w = jnp.concatenate([thr_s] * (wide // SUBLANES), axis=0)

        def tile_hits(j, cut):
            rows = pl.ds(pl.multiple_of(j * COUNT_TILE, COUNT_TILE), COUNT_TILE)
            k = keys_ref[rows, :].reshape(COUNT_TILE // wide, wide, tq)
            pos = j * COUNT_TILE + lax.broadcasted_iota(jnp.int32, (COUNT_TILE, tq), 0)
            pos = pos.reshape(COUNT_TILE // wide, wide, tq)
            return rows, k, jnp.logical_and(k == thr_w[None], pos < cut[None])

        def tied_before(cut):
            cut = jnp.concatenate([cut] * (wide // SUBLANES), axis=0)

            def body(j, cnt):
                _, _, hit = tile_hits(j, cut)
                return cnt + jnp.sum(jnp.where(hit, 1.0, 0.0), axis=0)

            cnt = lax.fori_loop(0, n_count, body, jnp.zeros((wide, tq), f32))
            return jnp.broadcast_to(reduce_keys(cnt, jnp.sum), (SUBLANES, tq))

        def cut_step(step, last):
            cand = last | jnp.left_shift(jnp.int32(1), pos_bits - 1 - step)
            return jnp.where(tied_before(cand) < need, cand, last)

        last = lax.fori_loop(0, pos_bits, cut_step, jnp.zeros((SUBLANES, tq), jnp.int32))
        keep_before = jnp.where(tied, last + 1, int_max)
        keep_before = jnp.concatenate([keep_before] * (wide // SUBLANES), axis=0)

        def drop(j):
            rows, k, hit = tile_hits(j, keep_before)
            surplus = jnp.logical_and(k == thr_w[None], jnp.logical_not(hit))
            keys_ref[rows, :] = jnp.where(surplus, _INT_MIN, k).reshape(COUNT_TILE, tq)

        loop(n_count, drop)

    m_ref[...] = jnp.full(m_ref.shape, -1e30, f32)
    l_ref[...] = jnp.zeros(l_ref.shape, f32)
    acc_ref[...] = jnp.zeros(acc_ref.shape, f32)

    def attend(r0, nk, near):
        r0 = pl.multiple_of(r0, MAX_DISTANCE)
        rows = pl.ds(r0, nk)
        selected = keys_ref[rows, :] >= thr
        mask_bias = jnp.where(selected, 0.0, -jnp.inf)
        if not near:
            pos = r0 + lax.broadcasted_iota(jnp.int32, (nk, tq), 0)
            mask_bias = jnp.where(pos < w0, mask_bias, -jnp.inf)
        def logits(g):
            cs = slice(g * HEAD_DIM, (g + 1) * HEAD_DIM)
            qg = q_ref[g * heads_per_kv:(g + 1) * heads_per_kv].reshape(heads_per_kv * tq, HEAD_DIM)
            s_all = lax.dot_general(k_ref[rows, cs], qg, nt, preferred_element_type=f32)
            for r in range(heads_per_kv):
                h = g * heads_per_kv + r
                s = s_all[:, r * tq:(r + 1) * tq] + mask_bias
                if near:
                    s = s + bias_ref[jnp.where(t0 == w0, 1, 0), h]
                s_ref[h, 0:nk, :] = s

        def probabilities(g):
            alphas = []
            for r in range(heads_per_kv):
                h = g * heads_per_kv + r
                m_old = m_ref[h]
                m_new = jnp.maximum(m_old, reduce_keys(s_ref[h, 0:nk, :], jnp.max))
                alphas.append(jnp.exp2(m_old - m_new))
                p = jnp.exp2(s_ref[h, 0:nk, :] - m_new[0:1])
                m_ref[h] = m_new
                p_ref[g, 0:nk, r * tq:(r + 1) * tq] = p.astype(p_ref.dtype)
            return alphas

        def accumulate(g, alphas):
            pv = jnp.dot(vt_ref[g, :, rows], p_ref[g, 0:nk, :], preferred_element_type=f32)
            for r in range(heads_per_kv):
                h = g * heads_per_kv + r
                qs_ = slice(r * tq, (r + 1) * tq)
                acc_ref[h] = alphas[r][0:1] * acc_ref[h] + pv[0:HEAD_DIM, qs_]
                l_ref[h] = alphas[r] * l_ref[h] + pv[HEAD_DIM:HEAD_DIM + 1, qs_]

        logits(0)
        alphas = probabilities(0)
        for g in range(1, N_KV_HEADS):
            logits(g)
            accumulate(g - 1, alphas)
            alphas = probabilities(g)
        accumulate(N_KV_HEADS - 1, alphas)

    n_big, far_rest, n_small = far_steps(BIG_STEP)
    loop(n_big, lambda j: attend(j * (BIG_STEP * ATTN_TILE), BIG_STEP * ATTN_TILE, False))
    loop(n_small, lambda j: attend(far_rest + j * ATTN_TILE, ATTN_TILE, False))
    attend(w0, NEAR_KEYS, True)

    for h in range(N_HEADS):
        out_t = acc_ref[h] / l_ref[h][0:1]
        o_ref[:, h * HEAD_DIM:(h + 1) * HEAD_DIM] = out_t.T.astype(o_ref.dtype)


def _attention(q_hm, qi_hm, w_t, k_all, vt_all, ki_all, bias_tiles, *, n_batch, n_qblocks, n_keys, t0_base, t0_step,
               topk, n_queries=Q_TILE):
    tq = Q_TILE
    kv_w = N_KV_HEADS * HEAD_DIM
    once = pl.Buffered(1)
    if k_all.ndim == 3:
        key_specs = [
            pl.BlockSpec((None, n_keys, kv_w), lambda b, i: (b, 0, 0), pipeline_mode=once),
            pl.BlockSpec((None, N_KV_HEADS, V_ROWS, n_keys), lambda b, i: (b, 0, 0, 0), pipeline_mode=once),
            pl.BlockSpec((None, n_keys, LANES), lambda b, i: (b, 0, 0), pipeline_mode=once),
        ]
    else:
        key_specs = [
            pl.BlockSpec((n_keys, kv_w), lambda b, i: (b, 0), pipeline_mode=once),
            pl.BlockSpec((N_KV_HEADS, V_ROWS, n_keys), lambda b, i: (0, 0, b), pipeline_mode=once),
            pl.BlockSpec((n_keys, LANES), lambda b, i: (b, 0), pipeline_mode=once),
        ]
    assert n_keys % COUNT_TILE == 0 and tq == MAX_DISTANCE and NEAR_KEYS == SCORE_TILE
    assert ATTN_TILE % COUNT_TILE == 0 and ATTN_TILE == SCORE_STEP * SCORE_TILE
    assert t0_base % MAX_DISTANCE == 0 and t0_step % MAX_DISTANCE == 0
    last_w0 = max(t0_base + (n_qblocks - 1) * t0_step - MAX_DISTANCE, 0)
    assert -(-last_w0 // ATTN_TILE) * ATTN_TILE <= n_keys and last_w0 + NEAR_KEYS <= n_keys
    kernel = functools.partial(_attn_kernel, t0_base=t0_base, t0_step=t0_step, topk=float(topk),
                               idx_scale=(H_IDX * D_IDX) ** -0.5, pos_bits=(n_keys + COUNT_TILE).bit_length(),
                               n_queries=n_queries)
    return pl.pallas_call(
        kernel,
        out_shape=jax.ShapeDtypeStruct((n_batch * n_qblocks * tq, N_HEADS * HEAD_DIM), _MXU_DTYPE),
        grid=(n_batch, n_qblocks),
        in_specs=[
            pl.BlockSpec((N_HEADS, tq, HEAD_DIM), lambda b, i: (0, b * n_qblocks + i, 0)),
            pl.BlockSpec((H_IDX, tq, LANES), lambda b, i: (0, b * n_qblocks + i, 0)),
            pl.BlockSpec((H_IDX, tq), lambda b, i: (0, b * n_qblocks + i)),
            *key_specs,
            pl.BlockSpec((2, N_HEADS, NEAR_KEYS, tq), lambda b, i: (0, 0, 0, 0), pipeline_mode=once),
        ],
        out_specs=pl.BlockSpec((tq, N_HEADS * HEAD_DIM), lambda b, i: (b * n_qblocks + i, 0)),
        scratch_shapes=[
            pltpu.VMEM((n_keys + COUNT_TILE, tq), jnp.int32),
            pltpu.VMEM((N_HEADS, SUBLANES, tq), jnp.float32),
            pltpu.VMEM((N_HEADS, SUBLANES, tq), jnp.float32),
            pltpu.VMEM((N_HEADS, HEAD_DIM, tq), jnp.float32),
            pltpu.VMEM((N_HEADS, BIG_STEP * ATTN_TILE, tq), jnp.float32),
            pltpu.VMEM((N_KV_HEADS, BIG_STEP * ATTN_TILE, (N_HEADS // N_KV_HEADS) * tq), _MXU_DTYPE),
        ],
        compiler_params=_params("parallel", "arbitrary"),
        name="dsa_attention",
    )(q_hm, qi_hm, w_t, k_all, vt_all, ki_all, bias_tiles)


def _t5_bucket(rel):
    nb = NUM_BUCKETS // 2
    max_exact = nb // 2
    n = jnp.abs(rel)
    nf = jnp.maximum(n, max_exact).astype(jnp.float32)
    large = max_exact + (jnp.log(nf / max_exact) / math.log(MAX_DISTANCE / max_exact) * (nb - max_exact)).astype(jnp.int32)
    large = jnp.minimum(large, nb - 1)
    return jnp.where(rel > 0, nb, 0) + jnp.where(n < max_exact, n, large)


def _near_bias(rel_bias):
    d = jnp.array([-MAX_DISTANCE, 0], jnp.int32)[:, None, None]
    rel = d + jnp.arange(NEAR_KEYS, dtype=jnp.int32)[None, :, None] - jnp.arange(Q_TILE, dtype=jnp.int32)[None, None, :]
    hit = _t5_bucket(rel)[:, None, :, :, None] == jnp.arange(NUM_BUCKETS, dtype=jnp.int32)
    near = jnp.sum(jnp.where(hit, rel_bias.T[None, :, None, None, :], 0.0), axis=-1)
    far = rel_bias[_t5_bucket(jnp.int32(-MAX_DISTANCE))]
    return ((near - far[None, :, None, None]) * LOG2_E).astype(jnp.float32)


def _values_t(v):
    n, keys, _ = v.shape
    vt = jnp.transpose(v.reshape(n, keys, N_KV_HEADS, HEAD_DIM), (0, 2, 3, 1))
    ones = jnp.ones((n, N_KV_HEADS, 1, keys), v.dtype)
    zeros = jnp.zeros((n, N_KV_HEADS, V_ROWS - HEAD_DIM - 1, keys), v.dtype)
    return jnp.concatenate([vt, ones, zeros], axis=2)


def _pad_cols(w, block, width):
    lead = w.shape[:-1]
    n = w.shape[-1] // block
    w = w.reshape(*lead, n, block)
    w = jnp.pad(w, [(0, 0)] * len(lead) + [(0, 0), (0, width - block)])
    return w.reshape(*lead, n * width)


def kernel(x_prompt, x_sample, p_prompt, p_sample, cache_k, cache_v, cache_kidx, norm_g, w_ffn1_gu, w_ffn1_down,
           w_in, w_out, gmlp_ws, gmlp_b, gmlp_vnorm, w_ffn2_gu, w_ffn2_down, w_ple_gate, w_ple, rel_bias):
    depth = w_in.shape[0]
    n_b, seq, d_model = x_prompt.shape
    dec_b, dec_seq, _ = x_sample.shape
    past = cache_k.shape[2]
    rows_p, rows_s = n_b * seq, dec_b * dec_seq
    attn_w = N_HEADS * HEAD_DIM
    kv_w = N_KV_HEADS * HEAD_DIM
    idx_w = H_IDX * D_IDX
    gmlp_w = d_model - attn_w
    mx = _MXU_DTYPE
    assert seq % ATTN_TILE == 0 and dec_seq <= CHUNK and past % MAX_DISTANCE == 0 and dec_seq % SUBLANES == 0
    assert past >= MAX_DISTANCE
    assert rows_p % GMLP_TILE == 0 and rows_s % GMLP_TILE == 0 and GMLP_CHUNK % dec_seq == 0
    assert GMLP_TILE % GMLP_CHUNK == 0 and (rows_p + rows_s) % ROW_TILE == 0

    o1 = attn_w
    o2 = o1 + kv_w
    o3 = o2 + kv_w
    o4 = o3 + idx_w
    o6 = o4 + D_IDX + H_IDX
    w_q = w_in[:, :, :o1].astype(mx)
    w_qi = _pad_cols(w_in[:, :, o3:o4], D_IDX, LANES).astype(mx)
    w_kvki = w_in[:, :, o1:o3]
    w_kvki = jnp.concatenate([w_kvki, w_in[:, :, o4:o6]], axis=-1)
    w_kvki = jnp.pad(w_kvki, ((0, 0), (0, 0), (0, 2 * kv_w + LANES - w_kvki.shape[-1]))).astype(mx)
    w_uv = w_in[:, :, o6:].astype(mx)
    w1_gu, w1_dn = w_ffn1_gu.astype(mx), w_ffn1_down.astype(mx)
    w2_gu, w2_dn = w_ffn2_gu.astype(mx), w_ffn2_down.astype(mx)
    w_out2 = w_out.astype(mx).reshape(depth, 2, attn_w, d_model)
    w_pg, w_pe = w_ple_gate.astype(mx), w_ple.astype(mx)

    causal = jnp.tril(jnp.ones((GMLP_CHUNK, GMLP_CHUNK), bool))
    ws_full = jnp.where(causal, gmlp_ws, 0.0)
    reps = GMLP_CHUNK // dec_seq
    small = jnp.where(causal[:dec_seq, :dec_seq], gmlp_ws[:, :, :dec_seq, :dec_seq], 0.0)
    eye = jnp.eye(reps, dtype=gmlp_ws.dtype)
    ws_small = jnp.einsum('ab,lgij->lgaibj', eye, small).reshape(depth, GMLP_GROUPS, GMLP_CHUNK, GMLP_CHUNK)
    ws2 = jnp.stack([ws_full, ws_small], axis=1).astype(mx)
    b_small = jnp.tile(gmlp_b[:, :, :dec_seq], (1, 1, reps))
    b2 = jnp.stack([gmlp_b, b_small], axis=1)
    b2 = jnp.repeat(jnp.swapaxes(b2, 2, 3), gmlp_w // GMLP_GROUPS, axis=-1)

    near_bias = _near_bias(rel_bias)

    h = jnp.concatenate([x_prompt.reshape(rows_p, d_model), x_sample.reshape(rows_s, d_model)], axis=0)
    p_all = jnp.concatenate([p_prompt.reshape(depth, rows_p, -1), p_sample.reshape(depth, rows_s, -1)], axis=1)

    keys_s = past + dec_seq
    w0_s = past - MAX_DISTANCE
    keys_s_pad = max(-(-(w0_s + NEAR_KEYS) // COUNT_TILE) * COUNT_TILE, -(-w0_s // ATTN_TILE) * ATTN_TILE)
    assert keys_s_pad >= keys_s
    topk_p = min(TOPK_MAX, seq // 4)
    topk_s = min(TOPK_MAX, keys_s // 4)

    def sample_queries(x_hm):
        x = x_hm[:, rows_p:].reshape(x_hm.shape[0], dec_b, dec_seq, x_hm.shape[-1])
        x = jnp.pad(x, ((0, 0), (0, 0), (0, Q_TILE - dec_seq), (0, 0)))
        return x.reshape(x_hm.shape[0], dec_b * Q_TILE, x_hm.shape[-1])

    outs = [[] for _ in range(7)]
    for l in range(depth):
        g = norm_g[l][:, None, :]
        h = _ffn(h, g[0], g[1], w1_gu, w1_dn, l)

        q_hm = _proj(h, g[2], w_q[l], mx, True, "proj_q", out_scale=HEAD_DIM ** -0.5 * LOG2_E)
        qi_hm = _proj(h, g[2], w_qi[l], mx, True, "proj_qi")
        k_new, v_new, ki_new, k_mx, ki_mx, vt_mx, w_t = _proj_kv(h, g[2], w_kvki[l])
        uv = _proj(h, g[2], w_uv[l], jnp.float32, False, "proj_uv")

        attn_p = _attention(
            q_hm, qi_hm, w_t, k_mx, vt_mx, ki_mx, near_bias,
            n_batch=n_b, n_qblocks=seq // Q_TILE, n_keys=seq, t0_base=0, t0_step=Q_TILE, topk=topk_p)

        def keys_s_all(cache, new, width):
            full = jnp.concatenate([cache.reshape(dec_b, past, -1), new[rows_p:].reshape(dec_b, dec_seq, -1)], axis=1)
            full = jnp.pad(full, ((0, 0), (0, keys_s_pad - keys_s), (0, width - full.shape[-1])))
            return full.astype(mx)

        w_t_s = jnp.pad(w_t[:, rows_p:].reshape(H_IDX, dec_b, dec_seq), ((0, 0), (0, 0), (0, Q_TILE - dec_seq)))
        attn_s = _attention(
            sample_queries(q_hm), sample_queries(qi_hm), w_t_s.reshape(H_IDX, dec_b * Q_TILE),
            keys_s_all(cache_k[l], k_new, kv_w), _values_t(keys_s_all(cache_v[l], v_new, kv_w)),
            keys_s_all(cache_kidx[l], ki_new, LANES),
            near_bias, n_batch=dec_b, n_qblocks=1, n_keys=keys_s_pad, t0_base=past, t0_step=0, topk=topk_s,
            n_queries=dec_seq)
        attn_s = attn_s.reshape(dec_b, Q_TILE, attn_w)[:, :dec_seq].reshape(rows_s, attn_w)
        attn = jnp.concatenate([attn_p, attn_s], axis=0)

        gm, vn = _gmlp(uv, gmlp_vnorm[l][None, :], ws2[l], b2[l], rows_p)
        h = _mixout(h, attn, gm, w_out2, g[3], l)
        h = _ffn(h, g[4], g[5], w2_gu, w2_dn, l)
        h = _ple(h, p_all[l], g[6], g[7], w_pg, w_pe, l)

        outs[0].append(k_new[:rows_p].reshape(n_b, seq, N_KV_HEADS, HEAD_DIM))
        outs[1].append(v_new[:rows_p].reshape(n_b, seq, N_KV_HEADS, HEAD_DIM))
        outs[2].append(ki_new[:rows_p].reshape(n_b, seq, D_IDX))
        outs[3].append(k_new[rows_p:].reshape(dec_b, dec_seq, N_KV_HEADS, HEAD_DIM))
        outs[4].append(v_new[rows_p:].reshape(dec_b, dec_seq, N_KV_HEADS, HEAD_DIM))
        outs[5].append(ki_new[rows_p:].reshape(dec_b, dec_seq, D_IDX))
        outs[6].append(vn[rows_p:].reshape(dec_b, dec_seq, gmlp_w))

    y_prompt = h[:rows_p].reshape(n_b, seq, d_model)
    y_sample = h[rows_p:].reshape(dec_b, dec_seq, d_model)
    return (y_prompt, y_sample) + tuple(jnp.stack(o) for o in outs)
```

```python
import functools
import math

import jax
import jax.numpy as jnp
from jax import lax
from jax.experimental import pallas as pl
from jax.experimental.pallas import tpu as pltpu

CHUNK = 64
N_HEADS = 8
N_KV_HEADS = 2
HEAD_DIM = 128
H_IDX = 16
D_IDX = 64
TOPK_MAX = 256
GMLP_CHUNK = 128
GMLP_GROUPS = 8
NUM_BUCKETS = 32
MAX_DISTANCE = 128
EPS = 1e-6
LOG2_E = 1.4426950408889634

LANES = 128
SUBLANES = 8
VMEM_LIMIT_BYTES = 56 * 1024 * 1024

ROW_TILE = 768
GMLP_TILE = 512
FF_TILE = 512
Q_TILE = 128
SCORE_TILE = 256
COUNT_TILE = 1024
ATTN_TILE = 1024
BIG_STEP = 2
SCORE_BIG_STEP = 2
SCORE_STEP = 4
NEAR_KEYS = 2 * MAX_DISTANCE
V_ROWS = HEAD_DIM + 16

_MXU_DTYPE = jnp.bfloat16
_INT_MIN = -(2 ** 31)
FIELD_BITS = 15
_GUARDS = _INT_MIN + 2 ** 15


def _params(*semantics):
    return pltpu.CompilerParams(dimension_semantics=semantics, vmem_limit_bytes=VMEM_LIMIT_BYTES)


def _rms(x, g):
    return x * lax.rsqrt(jnp.mean(x * x, axis=-1, keepdims=True) + EPS) * g


def _ffn_kernel(h_ref, gpre_ref, gpost_ref, wg_ref, wu_ref, wd_ref, o_ref, xn_ref, acc_ref):
    j = pl.program_id(1)

    @pl.when(j == 0)
    def _():
        xn_ref[...] = _rms(h_ref[...], gpre_ref[...]).astype(xn_ref.dtype)
        acc_ref[...] = jnp.zeros_like(acc_ref)

    xn = xn_ref[...]
    gate = jnp.dot(xn, wg_ref[...], preferred_element_type=jnp.float32)
    up = jnp.dot(xn, wu_ref[...], preferred_element_type=jnp.float32)
    act = (jax.nn.silu(gate) * up).astype(wd_ref.dtype)
    acc_ref[...] += jnp.dot(act, wd_ref[...], preferred_element_type=jnp.float32)

    @pl.when(j == pl.num_programs(1) - 1)
    def _():
        o_ref[...] = h_ref[...] + 0.5 * _rms(acc_ref[...], gpost_ref[...])


def _ffn(h, g_pre, g_post, w_gu, w_down, layer):
    rows, d = h.shape
    d_ff = w_down.shape[1]
    nf = d_ff // FF_TILE
    assert rows % ROW_TILE == 0 and d_ff % FF_TILE == 0
    return pl.pallas_call(
        _ffn_kernel,
        out_shape=jax.ShapeDtypeStruct((rows, d), jnp.float32),
        grid=(rows // ROW_TILE, nf),
        in_specs=[
            pl.BlockSpec((ROW_TILE, d), lambda i, j: (i, 0)),
            pl.BlockSpec((1, d), lambda i, j: (0, 0)),
            pl.BlockSpec((1, d), lambda i, j: (0, 0)),
            pl.BlockSpec((None, d, FF_TILE), lambda i, j: (layer, 0, j)),
            pl.BlockSpec((None, d, FF_TILE), lambda i, j: (layer, 0, j + nf)),
            pl.BlockSpec((None, FF_TILE, d), lambda i, j: (layer, j, 0)),
        ],
        out_specs=pl.BlockSpec((ROW_TILE, d), lambda i, j: (i, 0)),
        scratch_shapes=[pltpu.VMEM((ROW_TILE, d), _MXU_DTYPE), pltpu.VMEM((ROW_TILE, d), jnp.float32)],
        compiler_params=_params("parallel", "arbitrary"),
        name="ffn",
    )(h, g_pre, g_post, w_gu, w_gu, w_down)


def _proj_kernel(h_ref, g_ref, w_ref, o_ref, *, head_major, out_scale):
    xn = _rms(h_ref[...], g_ref[...]).astype(w_ref.dtype)
    z = jnp.dot(xn, w_ref[...], preferred_element_type=jnp.float32)
    if out_scale != 1.0:
        z = z * out_scale
    if head_major:
        for hh in range(o_ref.shape[0]):
            o_ref[hh] = z[:, hh * LANES:(hh + 1) * LANES].astype(o_ref.dtype)
    else:
        o_ref[...] = z.astype(o_ref.dtype)


def _proj(h, g, w, out_dtype, head_major, name, out_scale=1.0):
    rows, d = h.shape
    n = w.shape[1]
    if head_major:
        out_shape = jax.ShapeDtypeStruct((n // LANES, rows, LANES), out_dtype)
        out_spec = pl.BlockSpec((n // LANES, ROW_TILE, LANES), lambda i: (0, i, 0))
    else:
        out_shape = jax.ShapeDtypeStruct((rows, n), out_dtype)
        out_spec = pl.BlockSpec((ROW_TILE, n), lambda i: (i, 0))
    return pl.pallas_call(
        functools.partial(_proj_kernel, head_major=head_major, out_scale=out_scale),
        out_shape=out_shape,
        grid=(rows // ROW_TILE,),
        in_specs=[
            pl.BlockSpec((ROW_TILE, d), lambda i: (i, 0)),
            pl.BlockSpec((1, d), lambda i: (0, 0)),
            pl.BlockSpec((d, n), lambda i: (0, 0)),
        ],
        out_specs=out_spec,
        compiler_params=_params("parallel"),
        name=name,
    )(h, g, w)


def _proj_kv_kernel(h_ref, g_ref, w_ref, kf_ref, vf_ref, kif_ref, k_ref, ki_ref, vt_ref, wt_ref):
    kv_w = N_KV_HEADS * HEAD_DIM
    xn = _rms(h_ref[...], g_ref[...]).astype(w_ref.dtype)
    z = jnp.dot(xn, w_ref[...], preferred_element_type=jnp.float32)
    kf_ref[...] = z[:, :kv_w]
    vf_ref[...] = z[:, kv_w:2 * kv_w]
    k_ref[...] = z[:, :kv_w].astype(k_ref.dtype)
    tail = z[:, 2 * kv_w:]
    kif_ref[...] = tail[:, :D_IDX]
    lane = lax.broadcasted_iota(jnp.int32, tail.shape, 1)
    ki_ref[...] = jnp.where(lane < D_IDX, tail, 0.0).astype(ki_ref.dtype)
    wt_ref[...] = tail.T[D_IDX:D_IDX + H_IDX, :]
    row = lax.broadcasted_iota(jnp.int32, (V_ROWS - HEAD_DIM, z.shape[0]), 0)
    for g in range(N_KV_HEADS):
        v_g = z[:, kv_w + g * HEAD_DIM:kv_w + (g + 1) * HEAD_DIM]
        vt_ref[g, 0:HEAD_DIM, :] = v_g.T.astype(vt_ref.dtype)
        vt_ref[g, HEAD_DIM:V_ROWS, :] = jnp.where(row == 0, 1.0, 0.0).astype(vt_ref.dtype)


def _proj_kv(h, g, w):
    rows, d = h.shape
    n = w.shape[1]
    kv_w = N_KV_HEADS * HEAD_DIM
    assert n == 2 * kv_w + LANES
    return pl.pallas_call(
        _proj_kv_kernel,
        out_shape=(jax.ShapeDtypeStruct((rows, kv_w), jnp.float32),
                   jax.ShapeDtypeStruct((rows, kv_w), jnp.float32),
                   jax.ShapeDtypeStruct((rows, D_IDX), jnp.float32),
                   jax.ShapeDtypeStruct((rows, kv_w), _MXU_DTYPE),
                   jax.ShapeDtypeStruct((rows, LANES), _MXU_DTYPE),
                   jax.ShapeDtypeStruct((N_KV_HEADS, V_ROWS, rows), _MXU_DTYPE),
                   jax.ShapeDtypeStruct((H_IDX, rows), jnp.float32)),
        grid=(rows // ROW_TILE,),
        in_specs=[
            pl.BlockSpec((ROW_TILE, d), lambda i: (i, 0)),
            pl.BlockSpec((1, d), lambda i: (0, 0)),
            pl.BlockSpec((d, n), lambda i: (0, 0)),
        ],
        out_specs=(pl.BlockSpec((ROW_TILE, kv_w), lambda i: (i, 0)),
                   pl.BlockSpec((ROW_TILE, kv_w), lambda i: (i, 0)),
                   pl.BlockSpec((ROW_TILE, D_IDX), lambda i: (i, 0)),
                   pl.BlockSpec((ROW_TILE, kv_w), lambda i: (i, 0)),
                   pl.BlockSpec((ROW_TILE, LANES), lambda i: (i, 0)),
                   pl.BlockSpec((N_KV_HEADS, V_ROWS, ROW_TILE), lambda i: (0, 0, i)),
                   pl.BlockSpec((H_IDX, ROW_TILE), lambda i: (0, i))),
        compiler_params=_params("parallel"),
        name="proj_kv",
    )(h, g, w)


def _gmlp_kernel(uv_ref, gain_ref, ws_ref, b_ref, o_ref, vn_ref):
    width = o_ref.shape[1]
    gdim = width // GMLP_GROUPS
    for c in range(uv_ref.shape[0] // GMLP_CHUNK):
        rs = slice(c * GMLP_CHUNK, (c + 1) * GMLP_CHUNK)
        u = jax.nn.gelu(uv_ref[rs, :width])
        vn = _rms(jax.nn.gelu(uv_ref[rs, width:]), gain_ref[...])
        vn_ref[rs, :] = vn
        vnb = vn.astype(ws_ref.dtype)
        for g in range(GMLP_GROUPS):
            cs = slice(g * gdim, (g + 1) * gdim)
            s = jnp.dot(ws_ref[g], vnb[:, cs], preferred_element_type=jnp.float32) + b_ref[:, cs]
            o_ref[rs, cs] = (u[:, cs] * s).astype(o_ref.dtype)


def _gmlp(uv, gain, ws2, b2, n_prompt_rows):
    rows = uv.shape[0]
    width = uv.shape[1] // 2
    tile = GMLP_TILE
    n_prompt_tiles = n_prompt_rows // tile

    def sel(i):
        return jnp.where(i >= n_prompt_tiles, 1, 0)

    return pl.pallas_call(
        _gmlp_kernel,
        out_shape=(jax.ShapeDtypeStruct((rows, width), _MXU_DTYPE),
                   jax.ShapeDtypeStruct((rows, width), jnp.float32)),
        grid=(rows // tile,),
        in_specs=[
            pl.BlockSpec((tile, 2 * width), lambda i: (i, 0)),
            pl.BlockSpec((1, width), lambda i: (0, 0)),
            pl.BlockSpec((None, GMLP_GROUPS, GMLP_CHUNK, GMLP_CHUNK), lambda i: (sel(i), 0, 0, 0)),
            pl.BlockSpec((None, GMLP_CHUNK, width), lambda i: (sel(i), 0, 0)),
        ],
        out_specs=(pl.BlockSpec((tile, width), lambda i: (i, 0)),
                   pl.BlockSpec((tile, width), lambda i: (i, 0))),
        compiler_params=_params("parallel"),
        name="gmlp",
    )(uv, gain, ws2, b2)


def _mixout_kernel(h_ref, a_ref, m_ref, w_ref, g_ref, o_ref):
    y = jnp.dot(a_ref[...], w_ref[0], preferred_element_type=jnp.float32)
    y = y + jnp.dot(m_ref[...], w_ref[1], preferred_element_type=jnp.float32)
    o_ref[...] = h_ref[...] + _rms(y, g_ref[...])


def _mixout(h, attn, gm, w_out2, g, layer):
    rows, d = h.shape
    half = attn.shape[1]
    return pl.pallas_call(
        _mixout_kernel,
        out_shape=jax.ShapeDtypeStruct((rows, d), jnp.float32),
        grid=(rows // ROW_TILE,),
        in_specs=[
            pl.BlockSpec((ROW_TILE, d), lambda i: (i, 0)),
            pl.BlockSpec((ROW_TILE, half), lambda i: (i, 0)),
            pl.BlockSpec((ROW_TILE, half), lambda i: (i, 0)),
            pl.BlockSpec((None, 2, half, d), lambda i: (layer, 0, 0, 0)),
            pl.BlockSpec((1, d), lambda i: (0, 0)),
        ],
        out_specs=pl.BlockSpec((ROW_TILE, d), lambda i: (i, 0)),
        compiler_params=_params("parallel"),
        name="mixout",
    )(h, attn, gm, w_out2, g)


def _ple_kernel(h_ref, p_ref, gpre_ref, gpost_ref, wpg_ref, wpe_ref, o_ref):
    h = h_ref[...]
    xn = _rms(h, gpre_ref[...]).astype(wpg_ref.dtype)
    gate = jax.nn.sigmoid(jnp.dot(xn, wpg_ref[...], preferred_element_type=jnp.float32))
    emb = jnp.dot(p_ref[...].astype(wpe_ref.dtype), wpe_ref[...], preferred_element_type=jnp.float32)
    o_ref[...] = h + _rms(gate * emb, gpost_ref[...])


def _ple(h, p, g_pre, g_post, w_pg, w_pe, layer):
    rows, d = h.shape
    d_ple = p.shape[1]
    return pl.pallas_call(
        _ple_kernel,
        out_shape=jax.ShapeDtypeStruct((rows, d), jnp.float32),
        grid=(rows // ROW_TILE,),
        in_specs=[
            pl.BlockSpec((ROW_TILE, d), lambda i: (i, 0)),
            pl.BlockSpec((ROW_TILE, d_ple), lambda i: (i, 0)),
            pl.BlockSpec((1, d), lambda i: (0, 0)),
            pl.BlockSpec((1, d), lambda i: (0, 0)),
            pl.BlockSpec((None, d, d), lambda i: (layer, 0, 0)),
            pl.BlockSpec((None, d_ple, d), lambda i: (layer, 0, 0)),
        ],
        out_specs=pl.BlockSpec((ROW_TILE, d), lambda i: (i, 0)),
        compiler_params=_params("parallel"),
        name="ple",
    )(h, p, g_pre, g_post, w_pg, w_pe)


def _attn_kernel(q_ref, qi_ref, w_ref, k_ref, vt_ref, ki_ref, bias_ref, o_ref,
                 keys_ref, pk_ref, m_ref, l_ref, acc_ref, s_ref, p_ref, *, t0_base, t0_step, topk, idx_scale, pos_bits,
                 n_queries):
    f32 = jnp.float32
    tq = Q_TILE
    heads_per_kv = N_HEADS // N_KV_HEADS
    nt = (((1,), (1,)), ((), ()))
    t0 = t0_base + pl.program_id(1) * t0_step
    w0 = jnp.maximum(t0 - MAX_DISTANCE, 0)

    def loop(n, fn):
        lax.fori_loop(0, n, lambda j, c: (fn(j), c)[1], 0)

    def reduce_keys(x, op):
        x = op(x.reshape(-1, SUBLANES * SUBLANES, tq), axis=0)
        x = op(x.reshape(SUBLANES, SUBLANES, tq), axis=0)
        return op(x, axis=0, keepdims=True)

    def score_rows(r0, near):
        r0 = pl.multiple_of(r0, MAX_DISTANCE)
        qs = qi_ref[...].reshape(H_IDX * tq, LANES)
        d = lax.dot_general(ki_ref[pl.ds(r0, SCORE_TILE), :], qs, nt, preferred_element_type=f32)
        score = jnp.zeros((SCORE_TILE, tq), f32)
        for h in range(H_IDX):
            score = score + (w_ref[h:h + 1, :] * idx_scale) * jnp.maximum(d[:, h * tq:(h + 1) * tq], 0.0)
        bits = lax.bitcast_convert_type(score, jnp.int32)
        key = bits ^ ((bits >> 31) & 0x7FFFFFFF)
        if near:
            s_pos = r0 + lax.broadcasted_iota(jnp.int32, (SCORE_TILE, tq), 0)
            t_pos = t0 + lax.broadcasted_iota(jnp.int32, (SCORE_TILE, tq), 1)
            key = jnp.where((s_pos // CHUNK) <= (t_pos // CHUNK), key, _INT_MIN)
        keys_ref[pl.ds(r0, SCORE_TILE), :] = key
        field = lax.shift_right_logical(key ^ _INT_MIN, 32 - FIELD_BITS)
        half = SCORE_TILE // 2
        f_lo, f_hi = field[:half], field[half:]
        if near:
            mid_frame = (r0 // half) % 2 == 1
            words = jnp.where(mid_frame, _GUARDS | (f_hi << 16), _GUARDS | (f_lo << 16) | f_hi)
            start = jnp.where(mid_frame, (r0 + half) // 2, r0 // 2)
        else:
            words = _GUARDS | (f_lo << 16) | f_hi
            start = r0 // 2
        pk_ref[pl.ds(pl.multiple_of(start, half // 2), half), :] = words

    def score_step(r0, n_tiles):
        for i in range(n_tiles):
            score_rows(r0 + i * SCORE_TILE, False)

    def far_steps(big):
        n_big = w0 // (big * ATTN_TILE)
        rest = n_big * (big * ATTN_TILE)
        return n_big, rest, (w0 - rest + ATTN_TILE - 1) // ATTN_TILE

    n_big, far_rest, n_small = far_steps(SCORE_BIG_STEP)
    loop(n_big, lambda j: score_step(j * (SCORE_BIG_STEP * ATTN_TILE), SCORE_BIG_STEP * SCORE_STEP))
    loop(n_small, lambda j: score_step(far_rest + j * ATTN_TILE, SCORE_STEP))
    score_rows(w0, True)
    pad_rows = COUNT_TILE - MAX_DISTANCE
    pad = pl.ds(pl.multiple_of(w0 + NEAR_KEYS, MAX_DISTANCE), pad_rows)
    keys_ref[pad, :] = jnp.full((pad_rows, tq), _INT_MIN, jnp.int32)
    mid_frame = (w0 // MAX_DISTANCE) % 2 == 1
    pk_end = jnp.where(mid_frame, (w0 + MAX_DISTANCE) // 2, w0 // 2) + MAX_DISTANCE
    pk_ref[pl.ds(pl.multiple_of(pk_end, MAX_DISTANCE // 2), COUNT_TILE // 2), :] = jnp.full(
        (COUNT_TILE // 2, tq), _GUARDS, jnp.int32)

    n_count = (w0 + NEAR_KEYS + COUNT_TILE - 1) // COUNT_TILE
    wide = SUBLANES * SUBLANES

    def count_ge(cand):
        cand = jnp.concatenate([cand] * (wide // SUBLANES), axis=0)

        def body(j, cnt):
            k = keys_ref[pl.ds(pl.multiple_of(j * COUNT_TILE, COUNT_TILE), COUNT_TILE), :]
            k = k.reshape(COUNT_TILE // wide, wide, tq)
            for i in range(COUNT_TILE // wide):
                cnt = jnp.where(k[i] >= cand, cnt + 1.0, cnt)
            return cnt

        cnt = lax.fori_loop(0, n_count, body, jnp.zeros((wide, tq), f32))
        return jnp.broadcast_to(reduce_keys(cnt, jnp.sum), (SUBLANES, tq))

    def search_step(step, state):
        thr_u, at_thr = state
        cand_u = thr_u | jnp.left_shift(jnp.int32(1), 31 - step)
        cnt = count_ge(cand_u ^ _INT_MIN)
        take = cnt >= topk
        return jnp.where(take, cand_u, thr_u), jnp.where(take, cnt, at_thr)

    q_chunk = (t0 + lax.broadcasted_iota(jnp.int32, (SUBLANES, tq), 1)) // CHUNK
    n_admissible = jnp.minimum((q_chunk + 1) * CHUNK, w0 + NEAR_KEYS).astype(f32)
    blind_steps = 22

    def count_fields_ge(cand):
        cand = jnp.concatenate([(cand << 16) | cand] * (wide // SUBLANES), axis=0)
        half_tile = COUNT_TILE // 2

        def body(j, acc):
            w = pk_ref[pl.ds(pl.multiple_of(j * half_tile, half_tile), half_tile), :]
            w = w.reshape(half_tile // wide, wide, tq)
            for i in range(half_tile // wide):
                acc = acc + (lax.shift_right_logical(w[i] - cand, 15) & 0x00010001)
            return acc

        acc = lax.fori_loop(0, n_count, body, jnp.zeros((wide, tq), jnp.int32))
        cnt = ((acc & 0xFFFF) + lax.shift_right_logical(acc, 16)).astype(f32)
        return jnp.broadcast_to(reduce_keys(cnt, jnp.sum), (SUBLANES, tq))

    def field_step(step, state):
        thr_f, at_thr = state
        cand_f = thr_f | jnp.left_shift(jnp.int32(1), FIELD_BITS - 1 - step)
        cnt = count_fields_ge(cand_f)
        take = cnt >= topk
        return jnp.where(take, cand_f, thr_f), jnp.where(take, cnt, at_thr)

    thr_f, at_thr = lax.fori_loop(0, FIELD_BITS, field_step, (jnp.zeros((SUBLANES, tq), jnp.int32), n_admissible))
    state = lax.fori_loop(FIELD_BITS, blind_steps, search_step, (thr_f << (32 - FIELD_BITS), at_thr))

    steps_per_test = 2
    real_query = lax.broadcasted_iota(jnp.int32, (SUBLANES, tq), 1) < n_queries

    def unsettled(state):
        step, _, at_thr = state
        settled = jnp.logical_or(jnp.logical_or(at_thr == topk, n_admissible <= topk), jnp.logical_not(real_query))
        settled = jnp.where(settled, 1.0, 0.0)
        return jnp.logical_and(step < 32, jnp.sum(settled) < float(SUBLANES * tq))

    def tested_steps(state):
        step, thr_u, at_thr = state
        for i in range(steps_per_test):
            thr_u, at_thr = search_step(step + i, (thr_u, at_thr))
        return step + steps_per_test, thr_u, at_thr

    _, thr_u, at_thr = lax.while_loop(unsettled, tested_steps, (jnp.int32(blind_steps),) + state)
    thr_s = thr_u ^ _INT_MIN
    thr = jnp.maximum(thr_s, _INT_MIN + 1)[0:1]

    tied = jnp.logical_and(jnp.logical_and(at_thr > topk, n_admissible > topk), real_query)

    @pl.when(jnp.sum(jnp.where(tied, 1.0, 0.0)) > 0.0)
    def _break_ties():
        int_max = -(_INT_MIN + 1)
        is_max = thr_s == int_max
        above = jnp.where(is_max, 0.0, count_ge(jnp.where(is_max, thr_s, thr_s + 1)))
        need = topk - above
        thr_w = jnp.concatenate([thr_s] * (wide // SUBLANES), axis=0)

        def tile_hits(j, cut):
            rows = pl.ds(pl.multiple_of(j * COUNT_TILE, COUNT_TILE), COUNT_TILE)
            k = keys_ref[rows, :].reshape(COUNT_TILE // wide, wide, tq)
            pos = j * COUNT_TILE + lax.broadcasted_iota(jnp.int32, (COUNT_TILE, tq), 0)
            pos = pos.reshape(COUNT_TILE // wide, wide, tq)
            return rows, k, jnp.logical_and(k == thr_w[None], pos < cut[None])

        def tied_before(cut):
            cut = jnp.concatenate([cut] * (wide // SUBLANES), axis=0)

            def body(j, cnt):
                _, _, hit = tile_hits(j, cut)
                return cnt + jnp.sum(jnp.where(hit, 1.0, 0.0), axis=0)

            cnt = lax.fori_loop(0, n_count, body, jnp.zeros((wide, tq), f32))
            return jnp.broadcast_to(reduce_keys(cnt, jnp.sum), (SUBLANES, tq))

        def cut_step(step, last):
            cand = last | jnp.left_shift(jnp.int32(1), pos_bits - 1 - step)
            return jnp.where(tied_before(cand) < need, cand, last)

        last = lax.fori_loop(0, pos_bits, cut_step, jnp.zeros((SUBLANES, tq), jnp.int32))
        keep_before = jnp.where(tied, last + 1, int_max)
        keep_before = jnp.concatenate([keep_before] * (wide // SUBLANES), axis=0)

        def drop(j):
            rows, k, hit = tile_hits(j, keep_before)
            surplus = jnp.logical_and(k == thr_w[None], jnp.logical_not(hit))
            keys_ref[rows, :] = jnp.where(surplus, _INT_MIN, k).reshape(COUNT_TILE, tq)

        loop(n_count, drop)

    m_ref[...] = jnp.full(m_ref.shape, -1e30, f32)
    l_ref[...] = jnp.zeros(l_ref.shape, f32)
    acc_ref[...] = jnp.zeros(acc_ref.shape, f32)

    def attend(r0, nk, near):
        r0 = pl.multiple_of(r0, MAX_DISTANCE)
        rows = pl.ds(r0, nk)
        selected = keys_ref[rows, :] >= thr
        mask_bias = jnp.where(selected, 0.0, -jnp.inf)
        if not near:
            pos = r0 + lax.broadcasted_iota(jnp.int32, (nk, tq), 0)
            mask_bias = jnp.where(pos < w0, mask_bias, -jnp.inf)
        def logits(g):
            cs = slice(g * HEAD_DIM, (g + 1) * HEAD_DIM)
            qg = q_ref[g * heads_per_kv:(g + 1) * heads_per_kv].reshape(heads_per_kv * tq, HEAD_DIM)
            s_all = lax.dot_general(k_ref[rows, cs], qg, nt, preferred_element_type=f32)
            for r in range(heads_per_kv):
                h = g * heads_per_kv + r
                s = s_all[:, r * tq:(r + 1) * tq] + mask_bias
                if near:
                    s = s + bias_ref[jnp.where(t0 == w0, 1, 0), h]
                s_ref[h, 0:nk, :] = s

        def probabilities(g):
            alphas = []
            for r in range(heads_per_kv):
                h = g * heads_per_kv + r
                m_old = m_ref[h]
                m_new = jnp.maximum(m_old, reduce_keys(s_ref[h, 0:nk, :], jnp.max))
                alphas.append(jnp.exp2(m_old - m_new))
                p = jnp.exp2(s_ref[h, 0:nk, :] - m_new[0:1])
                m_ref[h] = m_new
                p_ref[g, 0:nk, r * tq:(r + 1) * tq] = p.astype(p_ref.dtype)
            return alphas

        def accumulate(g, alphas):
            pv = jnp.dot(vt_ref[g, :, rows], p_ref[g, 0:nk, :], preferred_element_type=f32)
            for r in range(heads_per_kv):
                h = g * heads_per_kv + r
                qs_ = slice(r * tq, (r + 1) * tq)
                acc_ref[h] = alphas[r][0:1] * acc_ref[h] + pv[0:HEAD_DIM, qs_]
                l_ref[h] = alphas[r] * l_ref[h] + pv[HEAD_DIM:HEAD_DIM + 1, qs_]

        logits(0)
        alphas = probabilities(0)
        for g in range(1, N_KV_HEADS):
            logits(g)
            accumulate(g - 1, alphas)
            alphas = probabilities(g)
        accumulate(N_KV_HEADS - 1, alphas)

    n_big, far_rest, n_small = far_steps(BIG_STEP)
    loop(n_big, lambda j: attend(j * (BIG_STEP * ATTN_TILE), BIG_STEP * ATTN_TILE, False))
    loop(n_small, lambda j: attend(far_rest + j * ATTN_TILE, ATTN_TILE, False))
    attend(w0, NEAR_KEYS, True)

    for h in range(N_HEADS):
        out_t = acc_ref[h] / l_ref[h][0:1]
        o_ref[:, h * HEAD_DIM:(h + 1) * HEAD_DIM] = out_t.T.astype(o_ref.dtype)


def _attention(q_hm, qi_hm, w_t, k_all, vt_all, ki_all, bias_tiles, *, n_batch, n_qblocks, n_keys, t0_base, t0_step,
               topk, n_queries=Q_TILE):
    tq = Q_TILE
    kv_w = N_KV_HEADS * HEAD_DIM
    once = pl.Buffered(1)
    if k_all.ndim == 3:
        key_specs = [
            pl.BlockSpec((None, n_keys, kv_w), lambda b, i: (b, 0, 0), pipeline_mode=once),
            pl.BlockSpec((None, N_KV_HEADS, V_ROWS, n_keys), lambda b, i: (b, 0, 0, 0), pipeline_mode=once),
            pl.BlockSpec((None, n_keys, LANES), lambda b, i: (b, 0, 0), pipeline_mode=once),
        ]
    else:
        key_specs = [
            pl.BlockSpec((n_keys, kv_w), lambda b, i: (b, 0), pipeline_mode=once),
            pl.BlockSpec((N_KV_HEADS, V_ROWS, n_keys), lambda b, i: (0, 0, b), pipeline_mode=once),
            pl.BlockSpec((n_keys, LANES), lambda b, i: (b, 0), pipeline_mode=once),
        ]
    assert n_keys % COUNT_TILE == 0 and tq == MAX_DISTANCE and NEAR_KEYS == SCORE_TILE
    assert ATTN_TILE % COUNT_TILE == 0 and ATTN_TILE == SCORE_STEP * SCORE_TILE
    assert t0_base % MAX_DISTANCE == 0 and t0_step % MAX_DISTANCE == 0
    last_w0 = max(t0_base + (n_qblocks - 1) * t0_step - MAX_DISTANCE, 0)
    assert -(-last_w0 // ATTN_TILE) * ATTN_TILE <= n_keys and last_w0 + NEAR_KEYS <= n_keys
    kernel = functools.partial(_attn_kernel, t0_base=t0_base, t0_step=t0_step, topk=float(topk),
                               idx_scale=(H_IDX * D_IDX) ** -0.5, pos_bits=(n_keys + COUNT_TILE).bit_length(),
                               n_queries=n_queries)
    return pl.pallas_call(
        kernel,
        out_shape=jax.ShapeDtypeStruct((n_batch * n_qblocks * tq, N_HEADS * HEAD_DIM), _MXU_DTYPE),
        grid=(n_batch, n_qblocks),
        in_specs=[
            pl.BlockSpec((N_HEADS, tq, HEAD_DIM), lambda b, i: (0, b * n_qblocks + i, 0)),
            pl.BlockSpec((H_IDX, tq, LANES), lambda b, i: (0, b * n_qblocks + i, 0)),
            pl.BlockSpec((H_IDX, tq), lambda b, i: (0, b * n_qblocks + i)),
            *key_specs,
            pl.BlockSpec((2, N_HEADS, NEAR_KEYS, tq), lambda b, i: (0, 0, 0, 0), pipeline_mode=once),
        ],
        out_specs=pl.BlockSpec((tq, N_HEADS * HEAD_DIM), lambda b, i: (b * n_qblocks + i, 0)),
        scratch_shapes=[
            pltpu.VMEM((n_keys + COUNT_TILE, tq), jnp.int32),
            pltpu.VMEM((n_keys // 2 + COUNT_TILE, tq), jnp.int32),
            pltpu.VMEM((N_HEADS, SUBLANES, tq), jnp.float32),
            pltpu.VMEM((N_HEADS, SUBLANES, tq), jnp.float32),
            pltpu.VMEM((N_HEADS, HEAD_DIM, tq), jnp.float32),
            pltpu.VMEM((N_HEADS, BIG_STEP * ATTN_TILE, tq), jnp.float32),
            pltpu.VMEM((N_KV_HEADS, BIG_STEP * ATTN_TILE, (N_HEADS // N_KV_HEADS) * tq), _MXU_DTYPE),
        ],
        compiler_params=_params("parallel", "arbitrary"),
        name="dsa_attention",
    )(q_hm, qi_hm, w_t, k_all, vt_all, ki_all, bias_tiles)


def _t5_bucket(rel):
    nb = NUM_BUCKETS // 2
    max_exact = nb // 2
    n = jnp.abs(rel)
    nf = jnp.maximum(n, max_exact).astype(jnp.float32)
    large = max_exact + (jnp.log(nf / max_exact) / math.log(MAX_DISTANCE / max_exact) * (nb - max_exact)).astype(jnp.int32)
    large = jnp.minimum(large, nb - 1)
    return jnp.where(rel > 0, nb, 0) + jnp.where(n < max_exact, n, large)


def _near_bias(rel_bias):
    d = jnp.array([-MAX_DISTANCE, 0], jnp.int32)[:, None, None]
    rel = d + jnp.arange(NEAR_KEYS, dtype=jnp.int32)[None, :, None] - jnp.arange(Q_TILE, dtype=jnp.int32)[None, None, :]
    hit = _t5_bucket(rel)[:, None, :, :, None] == jnp.arange(NUM_BUCKETS, dtype=jnp.int32)
    near = jnp.sum(jnp.where(hit, rel_bias.T[None, :, None, None, :], 0.0), axis=-1)
    far = rel_bias[_t5_bucket(jnp.int32(-MAX_DISTANCE))]
    return ((near - far[None, :, None, None]) * LOG2_E).astype(jnp.float32)


def _values_t(v):
    n, keys, _ = v.shape
    vt = jnp.transpose(v.reshape(n, keys, N_KV_HEADS, HEAD_DIM), (0, 2, 3, 1))
    ones = jnp.ones((n, N_KV_HEADS, 1, keys), v.dtype)
    zeros = jnp.zeros((n, N_KV_HEADS, V_ROWS - HEAD_DIM - 1, keys), v.dtype)
    return jnp.concatenate([vt, ones, zeros], axis=2)


def _pad_cols(w, block, width):
    lead = w.shape[:-1]
    n = w.shape[-1] // block
    w = w.reshape(*lead, n, block)
    w = jnp.pad(w, [(0, 0)] * len(lead) + [(0, 0), (0, width - block)])
    return w.reshape(*lead, n * width)


def kernel(x_prompt, x_sample, p_prompt, p_sample, cache_k, cache_v, cache_kidx, norm_g, w_ffn1_gu, w_ffn1_down,
           w_in, w_out, gmlp_ws, gmlp_b, gmlp_vnorm, w_ffn2_gu, w_ffn2_down, w_ple_gate, w_ple, rel_bias):
    depth = w_in.shape[0]
    n_b, seq, d_model = x_prompt.shape
    dec_b, dec_seq, _ = x_sample.shape
    past = cache_k.shape[2]
    rows_p, rows_s = n_b * seq, dec_b * dec_seq
    attn_w = N_HEADS * HEAD_DIM
    kv_w = N_KV_HEADS * HEAD_DIM
    idx_w = H_IDX * D_IDX
    gmlp_w = d_model - attn_w
    mx = _MXU_DTYPE
    assert seq % ATTN_TILE == 0 and dec_seq <= CHUNK and past % MAX_DISTANCE == 0 and dec_seq % SUBLANES == 0
    assert past >= MAX_DISTANCE
    assert rows_p % GMLP_TILE == 0 and rows_s % GMLP_TILE == 0 and GMLP_CHUNK % dec_seq == 0
    assert GMLP_TILE % GMLP_CHUNK == 0 and (rows_p + rows_s) % ROW_TILE == 0

    o1 = attn_w
    o2 = o1 + kv_w
    o3 = o2 + kv_w
    o4 = o3 + idx_w
    o6 = o4 + D_IDX + H_IDX
    w_q = w_in[:, :, :o1].astype(mx)
    w_qi = _pad_cols(w_in[:, :, o3:o4], D_IDX, LANES).astype(mx)
    w_kvki = w_in[:, :, o1:o3]
    w_kvki = jnp.concatenate([w_kvki, w_in[:, :, o4:o6]], axis=-1)
    w_kvki = jnp.pad(w_kvki, ((0, 0), (0, 0), (0, 2 * kv_w + LANES - w_kvki.shape[-1]))).astype(mx)
    w_uv = w_in[:, :, o6:].astype(mx)
    w1_gu, w1_dn = w_ffn1_gu.astype(mx), w_ffn1_down.astype(mx)
    w2_gu, w2_dn = w_ffn2_gu.astype(mx), w_ffn2_down.astype(mx)
    w_out2 = w_out.astype(mx).reshape(depth, 2, attn_w, d_model)
    w_pg, w_pe = w_ple_gate.astype(mx), w_ple.astype(mx)

    causal = jnp.tril(jnp.ones((GMLP_CHUNK, GMLP_CHUNK), bool))
    ws_full = jnp.where(causal, gmlp_ws, 0.0)
    reps = GMLP_CHUNK // dec_seq
    small = jnp.where(causal[:dec_seq, :dec_seq], gmlp_ws[:, :, :dec_seq, :dec_seq], 0.0)
    eye = jnp.eye(reps, dtype=gmlp_ws.dtype)
    ws_small = jnp.einsum('ab,lgij->lgaibj', eye, small).reshape(depth, GMLP_GROUPS, GMLP_CHUNK, GMLP_CHUNK)
    ws2 = jnp.stack([ws_full, ws_small], axis=1).astype(mx)
    b_small = jnp.tile(gmlp_b[:, :, :dec_seq], (1, 1, reps))
    b2 = jnp.stack([gmlp_b, b_small], axis=1)
    b2 = jnp.repeat(jnp.swapaxes(b2, 2, 3), gmlp_w // GMLP_GROUPS, axis=-1)

    near_bias = _near_bias(rel_bias)

    h = jnp.concatenate([x_prompt.reshape(rows_p, d_model), x_sample.reshape(rows_s, d_model)], axis=0)
    p_all = jnp.concatenate([p_prompt.reshape(depth, rows_p, -1), p_sample.reshape(depth, rows_s, -1)], axis=1)

    keys_s = past + dec_seq
    w0_s = past - MAX_DISTANCE
    keys_s_pad = max(-(-(w0_s + NEAR_KEYS) // COUNT_TILE) * COUNT_TILE, -(-w0_s // ATTN_TILE) * ATTN_TILE)
    assert keys_s_pad >= keys_s
    topk_p = min(TOPK_MAX, seq // 4)
    topk_s = min(TOPK_MAX, keys_s // 4)

    def sample_queries(x_hm):
        x = x_hm[:, rows_p:].reshape(x_hm.shape[0], dec_b, dec_seq, x_hm.shape[-1])
        x = jnp.pad(x, ((0, 0), (0, 0), (0, Q_TILE - dec_seq), (0, 0)))
        return x.reshape(x_hm.shape[0], dec_b * Q_TILE, x_hm.shape[-1])

    outs = [[] for _ in range(7)]
    for l in range(depth):
        g = norm_g[l][:, None, :]
        h = _ffn(h, g[0], g[1], w1_gu, w1_dn, l)

        q_hm = _proj(h, g[2], w_q[l], mx, True, "proj_q", out_scale=HEAD_DIM ** -0.5 * LOG2_E)
        qi_hm = _proj(h, g[2], w_qi[l], mx, True, "proj_qi")
        k_new, v_new, ki_new, k_mx, ki_mx, vt_mx, w_t = _proj_kv(h, g[2], w_kvki[l])
        uv = _proj(h, g[2], w_uv[l], jnp.float32, False, "proj_uv")

        attn_p = _attention(
            q_hm, qi_hm, w_t, k_mx, vt_mx, ki_mx, near_bias,
            n_batch=n_b, n_qblocks=seq // Q_TILE, n_keys=seq, t0_base=0, t0_step=Q_TILE, topk=topk_p)

        def keys_s_all(cache, new, width):
            full = jnp.concatenate([cache.reshape(dec_b, past, -1), new[rows_p:].reshape(dec_b, dec_seq, -1)], axis=1)
            full = jnp.pad(full, ((0, 0), (0, keys_s_pad - keys_s), (0, width - full.shape[-1])))
            return full.astype(mx)

        w_t_s = jnp.pad(w_t[:, rows_p:].reshape(H_IDX, dec_b, dec_seq), ((0, 0), (0, 0), (0, Q_TILE - dec_seq)))
        attn_s = _attention(
            sample_queries(q_hm), sample_queries(qi_hm), w_t_s.reshape(H_IDX, dec_b * Q_TILE),
            keys_s_all(cache_k[l], k_new, kv_w), _values_t(keys_s_all(cache_v[l], v_new, kv_w)),
            keys_s_all(cache_kidx[l], ki_new, LANES),
            near_bias, n_batch=dec_b, n_qblocks=1, n_keys=keys_s_pad, t0_base=past, t0_step=0, topk=topk_s,
            n_queries=dec_seq)
        attn_s = attn_s.reshape(dec_b, Q_TILE, attn_w)[:, :dec_seq].reshape(rows_s, attn_w)
        attn = jnp.concatenate([attn_p, attn_s], axis=0)

        gm, vn = _gmlp(uv, gmlp_vnorm[l][None, :], ws2[l], b2[l], rows_p)
        h = _mixout(h, attn, gm, w_out2, g[3], l)
        h = _ffn(h, g[4], g[5], w2_gu, w2_dn, l)
        h = _ple(h, p_all[l], g[6], g[7], w_pg, w_pe, l)

        outs[0].append(k_new[:rows_p].reshape(n_b, seq, N_KV_HEADS, HEAD_DIM))
        outs[1].append(v_new[:rows_p].reshape(n_b, seq, N_KV_HEADS, HEAD_DIM))
        outs[2].append(ki_new[:rows_p].reshape(n_b, seq, D_IDX))
        outs[3].append(k_new[rows_p:].reshape(dec_b, dec_seq, N_KV_HEADS, HEAD_DIM))
        outs[4].append(v_new[rows_p:].reshape(dec_b, dec_seq, N_KV_HEADS, HEAD_DIM))
        outs[5].append(ki_new[rows_p:].reshape(dec_b, dec_seq, D_IDX))
        outs[6].append(vn[rows_p:].reshape(dec_b, dec_seq, gmlp_w))

    y_prompt = h[:rows_p].reshape(n_b, seq, d_model)
    y_sample = h[rows_p:].reshape(dec_b, dec_seq, d_model)
    return (y_prompt, y_sample) + tuple(jnp.stack(o) for o in outs)
```
